```python
import jax
import jax.numpy as jnp
from jax import lax
import numpy as np

D_MODEL = 2048
BATCH = 2
SEQ = 4096
DEPTH = 4
DEC_BATCH = 128
DEC_SEQ = 8
PAST_LEN = 8192
PAGE_SIZE = 128

N_MIXERS = 3
N_GDN_LAYERS = (DEPTH + 2) // 3
N_MLA_LAYERS = (DEPTH + 1) // 3
N_RWKV_LAYERS = DEPTH // 3
D_FF = 4 * D_MODEL
NORM_EPS = 1e-6

GDN_QK_HEADS = 16
GDN_V_HEADS = 32
GDN_HEAD_K = 128
GDN_HEAD_V = 128
GDN_CONV = 4
GDN_CHUNK = 64
GDN_KEY_DIM = GDN_QK_HEADS * GDN_HEAD_K
GDN_VAL_DIM = GDN_V_HEADS * GDN_HEAD_V
GDN_CONV_DIM = 2 * GDN_KEY_DIM + GDN_VAL_DIM
GDN_IN_DIM = GDN_CONV_DIM + GDN_VAL_DIM + 2 * GDN_V_HEADS

MLA_HEADS = 16
MLA_Q_RANK = 512
MLA_KV_RANK = 512
MLA_NOPE = 128
MLA_ROPE = 64
MLA_V = 128
MLA_QK = MLA_NOPE + MLA_ROPE
MLA_IN_DIM = MLA_Q_RANK + MLA_KV_RANK + MLA_ROPE
ROPE_THETA = 10000.0
Q_BLOCK = 128

RWKV_HEAD = 64
RWKV_HEADS = D_MODEL // RWKV_HEAD
RWKV_DECAY_LORA = 96
RWKV_AAA_LORA = 96
RWKV_GATE_LORA = 256
RWKV_GN_EPS = 64e-5

kernel_name = 'hybrid_gdn_mla_rwkv7_decoder_step'


def rms_norm(x, g, eps=NORM_EPS):
    xf = x.astype(jnp.float32)
    y = xf * lax.rsqrt(jnp.mean(jnp.square(xf), axis=-1, keepdims=True) + eps)
    return (y * g.astype(jnp.float32)).astype(x.dtype)


def l2_normalize(x, eps=1e-6):
    xf = x.astype(jnp.float32)
    return xf * lax.rsqrt(jnp.sum(jnp.square(xf), axis=-1, keepdims=True) + eps)


def apply_rope(x, pos):
    half = x.shape[-1] // 2
    inv_freq = 1.0 / (ROPE_THETA ** (jnp.arange(half, dtype=jnp.float32) / half))
    ang = pos.astype(jnp.float32)[:, None] * inv_freq[None, :]
    ang = ang.reshape((pos.shape[0],) + (1,) * (x.ndim - 3) + (half,))
    cos, sin = jnp.cos(ang), jnp.sin(ang)
    xf = x.astype(jnp.float32)
    x1, x2 = xf[..., :half], xf[..., half:]
    return jnp.concatenate([x1 * cos - x2 * sin, x2 * cos + x1 * sin], axis=-1).astype(x.dtype)


def causal_dwconv(x, buf, w):
    xin = jnp.concatenate([buf.astype(x.dtype), x], axis=1)
    y = lax.conv_general_dilated(xin, w[:, None, :].astype(x.dtype), window_strides=(1,), padding='VALID',
                                 dimension_numbers=('NWC', 'WIO', 'NWC'), feature_group_count=x.shape[-1])
    return y, xin[:, -(w.shape[0] - 1):]


def gated_delta_chunked(q, k, v, g, beta, s0):
    bsz, L, H, _ = q.shape
    dv = v.shape[-1]
    C = min(GDN_CHUNK, L)
    n = -(-L // C)
    pad = n * C - L

    def to_blocks(t):
        t = jnp.pad(t, [(0, 0), (0, pad)] + [(0, 0)] * (t.ndim - 2))
        t = jnp.moveaxis(t, 2, 1)
        return t.reshape((bsz, H, n, C) + t.shape[3:])

    q, k, v, g, beta = (to_blocks(t) for t in (q, k, v, g, beta))
    G = jnp.cumsum(g, axis=-1)
    incl = jnp.tril(jnp.ones((C, C), dtype=bool))
    strict = jnp.tril(jnp.ones((C, C), dtype=bool), -1)
    decay = jnp.exp(jnp.where(incl, G[..., :, None] - G[..., None, :], -jnp.inf))
    a_mat = jnp.where(strict, beta[..., :, None] * decay * jnp.einsum('bhnid,bhnjd->bhnij', k, k), 0.0)
    t_mat = a_mat + jnp.eye(C, dtype=jnp.float32)
    rhs = jnp.concatenate([beta[..., None] * v, (beta * jnp.exp(G))[..., None] * k], axis=-1)
    sol = lax.linalg.triangular_solve(t_mat, rhs, left_side=True, lower=True, unit_diagonal=True)
    u_v, w_k = sol[..., :dv], sol[..., dv:]
    p_qk = decay * jnp.einsum('bhnid,bhnjd->bhnij', q, k)
    q_g = q * jnp.exp(G)[..., None]
    k_d = k * jnp.exp(G[..., -1:] - G)[..., None]
    g_tot = jnp.exp(G[..., -1])

    def step(s, blk):
        u_v_c, w_c, p_c, q_c, k_c, gt_c = blk
        u = u_v_c - jnp.einsum('bhck,bhkv->bhcv', w_c, s)
        o = jnp.einsum('bhck,bhkv->bhcv', q_c, s) + jnp.einsum('bhij,bhjv->bhiv', p_c, u)
        s = gt_c[..., None, None] * s + jnp.einsum('bhck,bhcv->bhkv', k_c, u)
        return s, o

    xs = tuple(jnp.moveaxis(t, 2, 0) for t in (u_v, w_k, p_qk, q_g, k_d, g_tot))
    s_final, o = lax.scan(step, s0, xs)
    o = jnp.moveaxis(o, 0, 2).reshape(bsz, H, n * C, dv)[:, :, :L]
    return jnp.moveaxis(o, 1, 2), s_final


def gdn_mixer(h, conv_buf, s0, w_in, conv_w, a_log, dt_bias, o_norm, w_out):
    bsz, L, _ = h.shape
    f32 = jnp.float32
    proj = h @ w_in
    qkv, z, b, a = jnp.split(proj, [GDN_CONV_DIM, GDN_CONV_DIM + GDN_VAL_DIM,
                                    GDN_CONV_DIM + GDN_VAL_DIM + GDN_V_HEADS], axis=-1)
    qkv, new_buf = causal_dwconv(qkv, conv_buf, conv_w)
    qkv = jax.nn.silu(qkv)
    q, k, v = jnp.split(qkv, [GDN_KEY_DIM, 2 * GDN_KEY_DIM], axis=-1)
    rep = GDN_V_HEADS // GDN_QK_HEADS
    q = jnp.repeat(l2_normalize(q.reshape(bsz, L, GDN_QK_HEADS, GDN_HEAD_K)), rep, axis=2) * (GDN_HEAD_K ** -0.5)
    k = jnp.repeat(l2_normalize(k.reshape(bsz, L, GDN_QK_HEADS, GDN_HEAD_K)), rep, axis=2)
    v = v.reshape(bsz, L, GDN_V_HEADS, GDN_HEAD_V).astype(f32)
    beta = jax.nn.sigmoid(b.astype(f32))
    g = -jnp.exp(a_log.astype(f32)) * jax.nn.softplus(a.astype(f32) + dt_bias.astype(f32))
    o, s_new = gated_delta_chunked(q, k, v, g, beta, s0.astype(f32))
    z = z.reshape(bsz, L, GDN_V_HEADS, GDN_HEAD_V).astype(f32)
    o = rms_norm(o, o_norm) * jax.nn.silu(z)
    y = o.reshape(bsz, L, GDN_VAL_DIM).astype(h.dtype) @ w_out
    return y, s_new, new_buf


def mla_project(h, pos, w_in, q_norm, w_uq, kv_norm):
    bsz, L, _ = h.shape
    c_q, c_kv, k_pe = jnp.split(h @ w_in, [MLA_Q_RANK, MLA_Q_RANK + MLA_KV_RANK], axis=-1)
    q = (rms_norm(c_q, q_norm) @ w_uq).reshape(bsz, L, MLA_HEADS, MLA_QK)
    q_nope, q_pe = q[..., :MLA_NOPE], apply_rope(q[..., MLA_NOPE:], pos)
    return q_nope, q_pe, rms_norm(c_kv, kv_norm), apply_rope(k_pe, pos)


def mla_prompt(h, pos, w_in, q_norm, w_uq, kv_norm, w_uk, w_uv, w_o):
    bsz, L, _ = h.shape
    q_nope, q_pe, c_kv, k_pe = mla_project(h, pos, w_in, q_norm, w_uq, kv_norm)
    k_nope = jnp.einsum('blr,rhd->blhd', c_kv, w_uk)
    v = jnp.einsum('blr,rhd->blhd', c_kv, w_uv)
    scale = MLA_QK ** -0.5
    key_pos = jnp.arange(L)

    def block(i):
        start = i * Q_BLOCK
        qn = lax.dynamic_slice_in_dim(q_nope, start, Q_BLOCK, axis=1)
        qp = lax.dynamic_slice_in_dim(q_pe, start, Q_BLOCK, axis=1)
        s = (jnp.einsum('bqhd,bkhd->bhqk', qn, k_nope)
             + jnp.einsum('bqhr,bkr->bhqk', qp, k_pe)).astype(jnp.float32) * scale
        mask = key_pos[None, :] <= (start + jnp.arange(Q_BLOCK))[:, None]
        p = jax.nn.softmax(jnp.where(mask, s, -jnp.inf), axis=-1)
        return jnp.einsum('bhqk,bkhd->bqhd', p.astype(v.dtype), v)

    o = lax.map(block, jnp.arange(L // Q_BLOCK))
    o = jnp.moveaxis(o, 0, 1).reshape(bsz, L, MLA_HEADS * MLA_V)
    return o @ w_o, c_kv, k_pe


def mla_sample(h, pos, past_ckv, past_kpe, w_in, q_norm, w_uq, kv_norm, w_uk, w_uv, w_o):
    bsz, L, _ = h.shape
    q_nope, q_pe, c_kv, k_pe = mla_project(h, pos, w_in, q_norm, w_uq, kv_norm)
    q_lat = jnp.einsum('bqhd,rhd->bqhr', q_nope, w_uk)
    scale = MLA_QK ** -0.5
    s_past = (jnp.einsum('bqhr,btr->bhqt', q_lat, past_ckv)
              + jnp.einsum('bqhr,btr->bhqt', q_pe, past_kpe)).astype(jnp.float32) * scale
    s_new = (jnp.einsum('bqhr,bkr->bhqk', q_lat, c_kv)
             + jnp.einsum('bqhr,bkr->bhqk', q_pe, k_pe)).astype(jnp.float32) * scale
    s_new = jnp.where(jnp.tril(jnp.ones((L, L), dtype=bool)), s_new, -jnp.inf)
    p = jax.nn.softmax(jnp.concatenate([s_past, s_new], axis=-1), axis=-1).astype(c_kv.dtype)
    T = past_ckv.shape[1]
    o_lat = (jnp.einsum('bhqt,btr->bqhr', p[..., :T], past_ckv)
             + jnp.einsum('bhqk,bkr->bqhr', p[..., T:], c_kv))
    o = jnp.einsum('bqhr,rhd->bqhd', o_lat, w_uv).reshape(bsz, L, MLA_HEADS * MLA_V)
    return o @ w_o, c_kv, k_pe


def rwkv7_mixer(h, shift_prev, s0, mu, w_rkv, w0, w1, w2, a0, a1, a2, g1, g2,
                k_k, k_a, r_k, ln_w, ln_b, w_o):
    bsz, L, D = h.shape
    f32 = jnp.float32
    prev = jnp.concatenate([shift_prev[:, None, :].astype(h.dtype), h[:, :-1]], axis=1)
    xx = prev - h
    xmix = h[None] + xx[None] * mu[:, None, None, :]
    r, k, v = jnp.einsum('sbld,sde->sble', xmix[:3], w_rkv)
    xw, xa, xg = xmix[3], xmix[4], xmix[5]
    w = -jax.nn.softplus(-(w0 + jnp.tanh(xw @ w1) @ w2).astype(f32)) - 0.5
    a = jax.nn.sigmoid((a0 + (xa @ a1) @ a2).astype(f32))
    g = jax.nn.sigmoid(xg @ g1) @ g2

    def heads(t):
        return t.reshape(bsz, L, RWKV_HEADS, RWKV_HEAD)

    kk = l2_normalize(heads(k * k_k))
    k = k.astype(f32) * (1.0 + (a - 1.0) * k_a.astype(f32))
    r_h, k_h, v_h, a_h = heads(r.astype(f32)), heads(k), heads(v.astype(f32)), heads(a)
    decay = jnp.exp(-jnp.exp(heads(w)))
    b_h = kk * a_h

    def step(s, inp):
        r_t, d_t, k_t, v_t, kk_t, b_t = inp
        sa = jnp.einsum('bhij,bhj->bhi', s, -kk_t)
        s = s * d_t[:, :, None, :] + sa[..., None] * b_t[:, :, None, :] + v_t[..., None] * k_t[:, :, None, :]
        return s, jnp.einsum('bhij,bhj->bhi', s, r_t)

    xs = tuple(jnp.moveaxis(t, 1, 0) for t in (r_h, decay, k_h, v_h, kk, b_h))
    s_new, y = lax.scan(step, s0.astype(f32), xs)
    y = jnp.moveaxis(y, 0, 1)
    mean = jnp.mean(y, axis=-1, keepdims=True)
    var = jnp.mean(jnp.square(y - mean), axis=-1, keepdims=True)
    y = ((y - mean) * lax.rsqrt(var + RWKV_GN_EPS)).reshape(bsz, L, D) * ln_w.astype(f32) + ln_b.astype(f32)
    bonus = jnp.sum(r_h * k_h * r_k.astype(f32), axis=-1, keepdims=True) * v_h
    y = (y + bonus.reshape(bsz, L, D)) * g.astype(f32)
    return y.astype(h.dtype) @ w_o, s_new, h[:, -1]


def sq_relu_mlp(h, w_up, w_down):
    return jnp.square(jax.nn.relu(h @ w_up)) @ w_down


def setup_inputs(seed: int = 0) -> dict:
    key = jax.random.key(seed)
    ks = iter(jax.random.split(key, 48))
    f32 = jnp.float32

    def nrm(shape, scale):
        return jax.random.normal(next(ks), shape, f32) * scale

    def uni(shape, lo, hi):
        return jax.random.uniform(next(ks), shape, f32, lo, hi)

    n_pages = PAST_LEN // PAGE_SIZE
    n_pool = (DEC_BATCH * n_pages * 5) // 4
    d = D_MODEL
    x_prompt = nrm((BATCH, SEQ, d), 1.0)
    x_sample = nrm((DEC_BATCH, DEC_SEQ, d), 1.0)
    cache_mla_ckv = nrm((n_pool, N_MLA_LAYERS, PAGE_SIZE, MLA_KV_RANK), 1.0)
    cache_mla_kpe = nrm((n_pool, N_MLA_LAYERS, PAGE_SIZE, MLA_ROPE), 1.0)
    page_table = jax.random.permutation(next(ks), n_pool)[:DEC_BATCH * n_pages].reshape(DEC_BATCH, n_pages).astype(jnp.int32)
    state_gdn_s = nrm((DEC_BATCH, N_GDN_LAYERS, GDN_V_HEADS, GDN_HEAD_K, GDN_HEAD_V), 0.1)
    state_gdn_conv = nrm((DEC_BATCH, N_GDN_LAYERS, GDN_CONV - 1, GDN_CONV_DIM), 1.0)
    state_rwkv_wkv = nrm((DEC_BATCH, N_RWKV_LAYERS, RWKV_HEADS, RWKV_HEAD, RWKV_HEAD), 0.1)
    state_rwkv_shift = nrm((DEC_BATCH, N_RWKV_LAYERS, d), 1.0)
    norm_w = 1.0 + nrm((DEPTH, 4, d), 0.05)
    gdn_w_in = nrm((N_GDN_LAYERS, d, GDN_IN_DIM), d ** -0.5)
    gdn_conv_w = nrm((N_GDN_LAYERS, GDN_CONV, GDN_CONV_DIM), GDN_CONV ** -0.5)
    gdn_a_log = jnp.log(uni((N_GDN_LAYERS, GDN_V_HEADS), 1.0, 16.0))
    dt = jnp.exp(uni((N_GDN_LAYERS, GDN_V_HEADS), float(np.log(1e-3)), float(np.log(1e-1))))
    gdn_dt_bias = dt + jnp.log(-jnp.expm1(-dt))
    gdn_o_norm = 1.0 + nrm((N_GDN_LAYERS, GDN_HEAD_V), 0.05)
    gdn_w_out = nrm((N_GDN_LAYERS, GDN_VAL_DIM, d), GDN_VAL_DIM ** -0.5)
    mla_w_in = nrm((N_MLA_LAYERS, d, MLA_IN_DIM), d ** -0.5)
    mla_q_norm = 1.0 + nrm((N_MLA_LAYERS, MLA_Q_RANK), 0.05)
    mla_w_uq = nrm((N_MLA_LAYERS, MLA_Q_RANK, MLA_HEADS * MLA_QK), MLA_Q_RANK ** -0.5)
    mla_kv_norm = 1.0 + nrm((N_MLA_LAYERS, MLA_KV_RANK), 0.05)
    mla_w_uk = nrm((N_MLA_LAYERS, MLA_KV_RANK, MLA_HEADS, MLA_NOPE), MLA_KV_RANK ** -0.5)
    mla_w_uv = nrm((N_MLA_LAYERS, MLA_KV_RANK, MLA_HEADS, MLA_V), MLA_KV_RANK ** -0.5)
    mla_w_o = nrm((N_MLA_LAYERS, MLA_HEADS * MLA_V, d), (MLA_HEADS * MLA_V) ** -0.5)
    rw_mu = uni((N_RWKV_LAYERS, 6, d), 0.0, 1.0)
    rw_w_rkv = nrm((N_RWKV_LAYERS, 3, d, d), d ** -0.5)
    rw_w0 = uni((N_RWKV_LAYERS, d), -6.5, -1.5)
    rw_w1 = nrm((N_RWKV_LAYERS, d, RWKV_DECAY_LORA), d ** -0.5)
    rw_w2 = nrm((N_RWKV_LAYERS, RWKV_DECAY_LORA, d), 0.5 * RWKV_DECAY_LORA ** -0.5)
    rw_a0 = nrm((N_RWKV_LAYERS, d), 0.1)
    rw_a1 = nrm((N_RWKV_LAYERS, d, RWKV_AAA_LORA), d ** -0.5)
    rw_a2 = nrm((N_RWKV_LAYERS, RWKV_AAA_LORA, d), 0.5 * RWKV_AAA_LORA ** -0.5)
    rw_g1 = nrm((N_RWKV_LAYERS, d, RWKV_GATE_LORA), d ** -0.5)
    rw_g2 = nrm((N_RWKV_LAYERS, RWKV_GATE_LORA, d), RWKV_GATE_LORA ** -0.5)
    rw_k_k = 0.85 + nrm((N_RWKV_LAYERS, d), 0.05)
    rw_k_a = 1.0 + nrm((N_RWKV_LAYERS, d), 0.05)
    rw_r_k = nrm((N_RWKV_LAYERS, RWKV_HEADS, RWKV_HEAD), 0.1)
    rw_ln_w = 1.0 + nrm((N_RWKV_LAYERS, d), 0.05)
    rw_ln_b = nrm((N_RWKV_LAYERS, d), 0.02)
    rw_w_o = nrm((N_RWKV_LAYERS, d, d), d ** -0.5)
    mlp_w_up = nrm((DEPTH, d, D_FF), d ** -0.5)
    mlp_w_down = nrm((DEPTH, D_FF, d), D_FF ** -0.5)
    return {'x_prompt': x_prompt, 'x_sample': x_sample,
            'cache_mla_ckv': cache_mla_ckv, 'cache_mla_kpe': cache_mla_kpe, 'page_table': page_table,
            'state_gdn_s': state_gdn_s, 'state_gdn_conv': state_gdn_conv,
            'state_rwkv_wkv': state_rwkv_wkv, 'state_rwkv_shift': state_rwkv_shift,
            'norm_w': norm_w,
            'gdn_w_in': gdn_w_in, 'gdn_conv_w': gdn_conv_w, 'gdn_a_log': gdn_a_log,
            'gdn_dt_bias': gdn_dt_bias, 'gdn_o_norm': gdn_o_norm, 'gdn_w_out': gdn_w_out,
            'mla_w_in': mla_w_in, 'mla_q_norm': mla_q_norm, 'mla_w_uq': mla_w_uq, 'mla_kv_norm': mla_kv_norm,
            'mla_w_uk': mla_w_uk, 'mla_w_uv': mla_w_uv, 'mla_w_o': mla_w_o,
            'rw_mu': rw_mu, 'rw_w_rkv': rw_w_rkv, 'rw_w0': rw_w0, 'rw_w1': rw_w1, 'rw_w2': rw_w2,
            'rw_a0': rw_a0, 'rw_a1': rw_a1, 'rw_a2': rw_a2, 'rw_g1': rw_g1, 'rw_g2': rw_g2,
            'rw_k_k': rw_k_k, 'rw_k_a': rw_k_a, 'rw_r_k': rw_r_k, 'rw_ln_w': rw_ln_w, 'rw_ln_b': rw_ln_b,
            'rw_w_o': rw_w_o, 'mlp_w_up': mlp_w_up, 'mlp_w_down': mlp_w_down}


def reference(x_prompt, x_sample, cache_mla_ckv, cache_mla_kpe, page_table,
              state_gdn_s, state_gdn_conv, state_rwkv_wkv, state_rwkv_shift,
              norm_w, gdn_w_in, gdn_conv_w, gdn_a_log, gdn_dt_bias, gdn_o_norm, gdn_w_out,
              mla_w_in, mla_q_norm, mla_w_uq, mla_kv_norm, mla_w_uk, mla_w_uv, mla_w_o,
              rw_mu, rw_w_rkv, rw_w0, rw_w1, rw_w2, rw_a0, rw_a1, rw_a2, rw_g1, rw_g2,
              rw_k_k, rw_k_a, rw_r_k, rw_ln_w, rw_ln_b, rw_w_o, mlp_w_up, mlp_w_down):
    f32 = jnp.float32
    bp, lp, _ = x_prompt.shape
    bs, ls, _ = x_sample.shape
    n_pages = page_table.shape[1]
    past_len = n_pages * PAGE_SIZE
    pos_p = jnp.arange(lp)
    pos_s = past_len + jnp.arange(ls)
    xp, xs = x_prompt, x_sample
    mla_ckv_p, mla_kpe_p, mla_ckv_s, mla_kpe_s = [], [], [], []
    gdn_s_p, gdn_c_p, gdn_s_s, gdn_c_s = [], [], [], []
    rw_s_p, rw_x_p, rw_s_s, rw_x_s = [], [], [], []
    for i in range(DEPTH):
        kind, j = i % N_MIXERS, i // N_MIXERS
        hp = rms_norm(xp, norm_w[i, 0])
        hs = rms_norm(xs, norm_w[i, 0])
        if kind == 0:
            w = (gdn_w_in[j], gdn_conv_w[j], gdn_a_log[j], gdn_dt_bias[j], gdn_o_norm[j], gdn_w_out[j])
            yp, s_p, c_p = gdn_mixer(hp, jnp.zeros((bp, GDN_CONV - 1, GDN_CONV_DIM), hp.dtype),
                                     jnp.zeros((bp, GDN_V_HEADS, GDN_HEAD_K, GDN_HEAD_V), f32), *w)
            ys, s_s, c_s = gdn_mixer(hs, state_gdn_conv[:, j], state_gdn_s[:, j], *w)
            gdn_s_p.append(s_p)
            gdn_c_p.append(c_p)
            gdn_s_s.append(s_s)
            gdn_c_s.append(c_s)
        elif kind == 1:
            w = (mla_w_in[j], mla_q_norm[j], mla_w_uq[j], mla_kv_norm[j], mla_w_uk[j], mla_w_uv[j], mla_w_o[j])
            yp, ckv_p, kpe_p = mla_prompt(hp, pos_p, *w)
            past_ckv = cache_mla_ckv[page_table, j].reshape(bs, past_len, MLA_KV_RANK)
            past_kpe = cache_mla_kpe[page_table, j].reshape(bs, past_len, MLA_ROPE)
            ys, ckv_s, kpe_s = mla_sample(hs, pos_s, past_ckv, past_kpe, *w)
            mla_ckv_p.append(ckv_p)
            mla_kpe_p.append(kpe_p)
            mla_ckv_s.append(ckv_s)
            mla_kpe_s.append(kpe_s)
        else:
            w = (rw_mu[j], rw_w_rkv[j], rw_w0[j], rw_w1[j], rw_w2[j], rw_a0[j], rw_a1[j], rw_a2[j],
                 rw_g1[j], rw_g2[j], rw_k_k[j], rw_k_a[j], rw_r_k[j], rw_ln_w[j], rw_ln_b[j], rw_w_o[j])
            yp, s_p, x_p = rwkv7_mixer(hp, jnp.zeros((bp, D_MODEL), hp.dtype),
                                       jnp.zeros((bp, RWKV_HEADS, RWKV_HEAD, RWKV_HEAD), f32), *w)
            ys, s_s, x_s = rwkv7_mixer(hs, state_rwkv_shift[:, j], state_rwkv_wkv[:, j], *w)
            rw_s_p.append(s_p)
            rw_x_p.append(x_p)
            rw_s_s.append(s_s)
            rw_x_s.append(x_s)
        xp = xp + rms_norm(yp, norm_w[i, 1])
        xs = xs + rms_norm(ys, norm_w[i, 1])
        xp = xp + rms_norm(sq_relu_mlp(rms_norm(xp, norm_w[i, 2]), mlp_w_up[i], mlp_w_down[i]), norm_w[i, 3])
        xs = xs + rms_norm(sq_relu_mlp(rms_norm(xs, norm_w[i, 2]), mlp_w_up[i], mlp_w_down[i]), norm_w[i, 3])

    def stack(rows, like):
        return jnp.stack(rows, axis=1).astype(like.dtype)

    return (xp, xs,
            stack(mla_ckv_p, cache_mla_ckv), stack(mla_kpe_p, cache_mla_kpe),
            stack(gdn_s_p, state_gdn_s), stack(gdn_c_p, state_gdn_conv),
            stack(rw_s_p, state_rwkv_wkv), stack(rw_x_p, state_rwkv_shift),
            stack(mla_ckv_s, cache_mla_ckv), stack(mla_kpe_s, cache_mla_kpe),
            stack(gdn_s_s, state_gdn_s), stack(gdn_c_s, state_gdn_conv),
            stack(rw_s_s, state_rwkv_wkv), stack(rw_x_s, state_rwkv_shift))
```

```python
import functools
import math

import jax
import jax.numpy as jnp
from jax import lax
from jax.experimental import pallas as pl
from jax.experimental.pallas import tpu as pltpu

F32 = jnp.float32
BF16 = jnp.bfloat16

NORM_EPS = 1e-6
L2_EPS = 1e-6
RWKV_GN_EPS = 64e-5
ROPE_THETA = 10000.0

LANES = 128
SUBLANES = 8

GDN_QK_HEADS = 16
GDN_V_HEADS = 32
GDN_HEAD = 128
GDN_CONV = 4
GDN_CHUNK = 64
GDN_KEY_DIM = GDN_QK_HEADS * GDN_HEAD
GDN_VAL_DIM = GDN_V_HEADS * GDN_HEAD
GDN_CONV_DIM = 2 * GDN_KEY_DIM + GDN_VAL_DIM
GDN_HG = 4

MLA_HEADS = 16
MLA_Q_RANK = 512
MLA_KV_RANK = 512
MLA_NOPE = 128
MLA_ROPE = 64
MLA_V = 128
MLA_QK = MLA_NOPE + MLA_ROPE
PAGE_SIZE = 128
DEC_PAGES_PER_STEP = 8

RWKV_HEAD = 64
RWKV_CHUNK = 64
SEQ_GROUP = 8

VMEM_LIMIT = 48 * 1024 * 1024


def _cparams(sem):
    return pltpu.CompilerParams(dimension_semantics=sem, vmem_limit_bytes=VMEM_LIMIT)


def _row_tile(*row_counts):
    for t in (512, 256, 128, 64, 32, 16, 8):
        if all(r % t == 0 for r in row_counts):
            return t
    raise ValueError(f"row counts {row_counts} are not multiples of {SUBLANES}")


def _col_tile(n):
    for t in (512, 384, 256, 128):
        if n % t == 0:
            return t
    return n


def _dot(a, b):
    return jnp.dot(a.astype(BF16), b.astype(BF16), preferred_element_type=F32)


def _dot_nt(a, b):
    return lax.dot_general(a.astype(BF16), b.astype(BF16), (((1,), (1,)), ((), ())),
                           preferred_element_type=F32)


def _dot_exact_lhs(a_bf16, x):
    hi = x.astype(BF16)
    r1 = x - hi.astype(F32)
    mid = r1.astype(BF16)
    lo = (r1 - mid.astype(F32)).astype(BF16)
    return (jnp.dot(a_bf16, hi, preferred_element_type=F32)
            + jnp.dot(a_bf16, mid, preferred_element_type=F32)
            + jnp.dot(a_bf16, lo, preferred_element_type=F32))


def _rms(x, g, eps=NORM_EPS):
    return x * lax.rsqrt(jnp.mean(x * x, axis=-1, keepdims=True) + eps) * g


def _sigmoid(x):
    return 1.0 / (1.0 + jnp.exp(-x))


def _softplus(x):
    return jnp.maximum(x, 0.0) + jnp.log(1.0 + jnp.exp(-jnp.abs(x)))


def _unit_lower_inverse(a, ii, jj, seq_rows):
    base = min(SUBLANES, seq_rows)
    sh = int(math.log2(base))
    a0 = jnp.where(jnp.right_shift(ii, sh) == jnp.right_shift(jj, sh), a, 0.0)
    x = jnp.where(ii == jj, 1.0, 0.0) - a0
    p = a0
    k = 1
    while 2 * k < base:
        p = _dot(p, p)
        x = x + _dot(x, p)
        k *= 2
    s = base
    while s < seq_rows:
        sh = int(math.log2(s))
        lower_left = jnp.logical_and(
            jnp.right_shift(ii, sh + 1) == jnp.right_shift(jj, sh + 1),
            jnp.logical_and(jnp.bitwise_and(jnp.right_shift(ii, sh), 1) == 1,
                            jnp.bitwise_and(jnp.right_shift(jj, sh), 1) == 0))
        e = jnp.where(lower_left, a, 0.0)
        x = x - _dot(_dot(x, e), x)
        s *= 2
    return x


def _linear_kernel(*refs, mode):
    if mode == "norm":
        x_ref, g_ref, w_ref, o_ref, xs_ref = refs
    elif mode == "mix":
        x_ref, p_ref, mu_ref, w_ref, o_ref, xs_ref = refs
    else:
        x_ref, w_ref, o_ref, xs_ref = refs

    @pl.when(pl.program_id(1) == 0)
    def _():
        x = x_ref[...]
        if mode == "norm":
            x = _rms(x, g_ref[...])
        elif mode == "mix":
            x = x + (p_ref[...] - x) * mu_ref[...]
        xs_ref[...] = x.astype(BF16)

    o_ref[...] = jnp.dot(xs_ref[...], w_ref[...], preferred_element_type=F32)


def _linear(x, w, *, mode="none", g=None, prev=None, mu=None, xcol=0, row0=0, rows=None, tm):
    k, n = w.shape
    rows = x.shape[0] - row0 if rows is None else rows
    tn = _col_tile(n)
    rb0 = row0 // tm
    xmap = lambda i, j: (i + rb0, xcol)
    vec = pl.BlockSpec((1, k), lambda i, j: (0, 0))
    in_specs, args = [pl.BlockSpec((tm, k), xmap)], [x]
    if mode == "norm":
        in_specs.append(vec)
        args.append(g.reshape(1, k))
    elif mode == "mix":
        in_specs += [pl.BlockSpec((tm, k), xmap), vec]
        args += [prev, mu.reshape(1, k)]
    in_specs.append(pl.BlockSpec((k, tn), lambda i, j: (0, j)))
    args.append(w)
    return pl.pallas_call(
        functools.partial(_linear_kernel, mode=mode),
        grid=(rows // tm, n // tn),
        in_specs=in_specs,
        out_specs=pl.BlockSpec((tm, tn), lambda i, j: (i, j)),
        out_shape=jax.ShapeDtypeStruct((rows, n), F32),
        scratch_shapes=[pltpu.VMEM((tm, k), BF16)],
        compiler_params=_cparams(("parallel", "arbitrary")),
        name=f"linear_{mode}",
    )(*args)


def _norm_kernel(x_ref, g_ref, o_ref):
    o_ref[...] = _rms(x_ref[...], g_ref[...])


def _norm_rows(x, g, *, tm):
    t, d = x.shape
    return pl.pallas_call(
        _norm_kernel,
        grid=(t // tm,),
        in_specs=[pl.BlockSpec((tm, d), lambda i: (i, 0)), pl.BlockSpec((1, d), lambda i: (0, 0))],
        out_specs=pl.BlockSpec((tm, d), lambda i: (i, 0)),
        out_shape=jax.ShapeDtypeStruct((t, d), F32),
        compiler_params=_cparams(("parallel",)),
        name="rms_norm",
    )(x, g.reshape(1, d))


def _out_proj_kernel(yp_ref, ys_ref, w_ref, r_ref, g_ref, o_ref, acc_ref, *, npb, nk):
    i = pl.program_id(0)
    k = pl.program_id(1)

    @pl.when(k == 0)
    def _():
        acc_ref[...] = jnp.zeros_like(acc_ref)

    @pl.when(i < npb)
    def _():
        acc_ref[...] += jnp.dot(yp_ref[...].astype(BF16), w_ref[...], preferred_element_type=F32)

    @pl.when(i >= npb)
    def _():
        acc_ref[...] += jnp.dot(ys_ref[...].astype(BF16), w_ref[...], preferred_element_type=F32)

    @pl.when(k == nk - 1)
    def _():
        o_ref[...] = r_ref[...] + _rms(acc_ref[...], g_ref[...])


def _out_proj(y_p, y_s, w, resid, g, *, tm):
    kdim, d = w.shape
    tk = _col_tile(kdim)
    npb, nsb, nk = y_p.shape[0] // tm, y_s.shape[0] // tm, kdim // tk
    return pl.pallas_call(
        functools.partial(_out_proj_kernel, npb=npb, nk=nk),
        grid=(npb + nsb, nk),
        in_specs=[
            pl.BlockSpec((tm, tk), lambda i, k: (jnp.minimum(i, npb - 1), jnp.where(i < npb, k, nk - 1))),
            pl.BlockSpec((tm, tk), lambda i, k: (jnp.maximum(i - npb, 0), jnp.where(i >= npb, k, 0))),
            pl.BlockSpec((tk, d), lambda i, k: (k, 0)),
            pl.BlockSpec((tm, d), lambda i, k: (i, 0)),
            pl.BlockSpec((1, d), lambda i, k: (0, 0)),
        ],
        out_specs=pl.BlockSpec((tm, d), lambda i, k: (i, 0)),
        out_shape=jax.ShapeDtypeStruct(resid.shape, F32),
        scratch_shapes=[pltpu.VMEM((tm, d), F32)],
        compiler_params=_cparams(("parallel", "arbitrary")),
        name="out_proj_resnorm",
    )(y_p, y_s, w, resid, g.reshape(1, d))


def _mlp_kernel(x_ref, g2_ref, wu_ref, wd_ref, g3_ref, o_ref, xs_ref, acc_ref, *, nf):
    j = pl.program_id(1)

    @pl.when(j == 0)
    def _():
        xs_ref[...] = _rms(x_ref[...], g2_ref[...]).astype(BF16)
        acc_ref[...] = jnp.zeros_like(acc_ref)

    h = jnp.dot(xs_ref[...], wu_ref[...], preferred_element_type=F32)
    h = jnp.square(jnp.maximum(h, 0.0))
    acc_ref[...] += jnp.dot(h.astype(BF16), wd_ref[...], preferred_element_type=F32)

    @pl.when(j == nf - 1)
    def _():
        o_ref[...] = x_ref[...] + _rms(acc_ref[...], g3_ref[...])


def _mlp(x, g2, w_up, w_down, g3, *, tm):
    t, d = x.shape
    f = w_up.shape[1]
    tf = _col_tile(f)
    vec = pl.BlockSpec((1, d), lambda i, j: (0, 0))
    return pl.pallas_call(
        functools.partial(_mlp_kernel, nf=f // tf),
        grid=(t // tm, f // tf),
        in_specs=[pl.BlockSpec((tm, d), lambda i, j: (i, 0)), vec,
                  pl.BlockSpec((d, tf), lambda i, j: (0, j)),
                  pl.BlockSpec((tf, d), lambda i, j: (j, 0)), vec],
        out_specs=pl.BlockSpec((tm, d), lambda i, j: (i, 0)),
        out_shape=jax.ShapeDtypeStruct((t, d), F32),
        scratch_shapes=[pltpu.VMEM((tm, d), BF16), pltpu.VMEM((tm, d), F32)],
        compiler_params=_cparams(("parallel", "arbitrary")),
        name="sq_relu_mlp",
    )(x, g2.reshape(1, d), w_up, w_down, g3.reshape(1, d))


def _head_linear_kernel(x_ref, w_ref, o_ref):
    o_ref[...] = jnp.dot(x_ref[...].astype(BF16), w_ref[...], preferred_element_type=F32)


def _head_linear(x, w, *, row0, rows, tm):
    nh, k, n = w.shape
    rb0 = row0 // tm
    return pl.pallas_call(
        _head_linear_kernel,
        grid=(rows // tm, nh),
        in_specs=[pl.BlockSpec((tm, k), lambda i, h: (i + rb0, h)),
                  pl.BlockSpec((None, k, n), lambda i, h: (h, 0, 0))],
        out_specs=pl.BlockSpec((tm, n), lambda i, h: (i, h)),
        out_shape=jax.ShapeDtypeStruct((rows, nh * n), F32),
        compiler_params=_cparams(("parallel", "parallel")),
        name="head_linear",
    )(x, w)


def _gdn_gate_kernel(x_ref, alog_ref, dtb_ref, o_ref):
    x = x_ref[...]
    lane = lax.broadcasted_iota(jnp.int32, x.shape, 1)
    g = -jnp.exp(alog_ref[...]) * _softplus(x + dtb_ref[...])
    o_ref[...] = jnp.where(lane < GDN_V_HEADS, g, _sigmoid(x))


def _gdn_gates(ab, a_log, dt_bias, *, tm):
    t, n = ab.shape
    pad = jnp.zeros((GDN_V_HEADS,), F32)
    vec = pl.BlockSpec((1, n), lambda i: (0, 0))
    return pl.pallas_call(
        _gdn_gate_kernel,
        grid=(t // tm,),
        in_specs=[pl.BlockSpec((tm, n), lambda i: (i, 0)), vec, vec],
        out_specs=pl.BlockSpec((tm, n), lambda i: (i, 0)),
        out_shape=jax.ShapeDtypeStruct((t, n), F32),
        compiler_params=_cparams(("parallel",)),
        name="gdn_gates",
    )(ab, jnp.concatenate([a_log, pad]).reshape(1, n), jnp.concatenate([dt_bias, pad]).reshape(1, n))


def _gdn_conv_kernel(x_ref, prev_ref, w_ref, o_ref, *, tm, tc, seq_len, nq, nqk, sample):
    i = pl.program_id(0)
    j = pl.program_id(1)
    x = x_ref[...]
    w = w_ref[...]
    acc = x * w[GDN_CONV - 1:GDN_CONV]
    if sample:
        hist = prev_ref[...]
        tok = jnp.bitwise_and(lax.broadcasted_iota(jnp.int32, x.shape, 0), SUBLANES - 1)
        for s in range(1, GDN_CONV):
            xs = jnp.where(tok >= s, pltpu.roll(x, s, 0), pltpu.roll(hist, tm - SUBLANES + s, 0))
            acc = acc + xs * w[GDN_CONV - 1 - s:GDN_CONV - s]
    else:
        first = lax.rem(i * tm, seq_len) == 0
        prev = jnp.where(first, 0.0, prev_ref[...])
        row = lax.broadcasted_iota(jnp.int32, prev.shape, 0)
        for s in range(1, GDN_CONV):
            xs = pltpu.roll(x, s, 0)
            top = jnp.where(row < s, pltpu.roll(prev, s, 0), xs[:SUBLANES])
            xs = jnp.concatenate([top, xs[SUBLANES:]], axis=0) if tm > SUBLANES else top
            acc = acc + xs * w[GDN_CONV - 1 - s:GDN_CONV - s]
    y = acc * _sigmoid(acc)
    is_qk = j < nqk
    qscale = jnp.where(j < nq, GDN_HEAD ** -0.5, 1.0)
    for hh in range(tc // GDN_HEAD):
        sl = slice(hh * GDN_HEAD, (hh + 1) * GDN_HEAD)
        yh = y[:, sl]
        nrm = lax.rsqrt(jnp.sum(yh * yh, axis=-1, keepdims=True) + L2_EPS) * qscale
        o_ref[:, sl] = yh * jnp.where(is_qk, nrm, 1.0)


def _gdn_conv(proj, hist, conv_w, *, row0, rows, seq_len, sample, tm):
    tc = 512
    rb0 = row0 // tm
    if sample:
        prev_spec = pl.BlockSpec((tm, tc), lambda i, j: (i, j))
        prev = hist
    else:
        per = tm // SUBLANES
        prev_spec = pl.BlockSpec((SUBLANES, tc), lambda i, j: (jnp.maximum((i + rb0) * per - 1, 0), j))
        prev = proj
    return pl.pallas_call(
        functools.partial(_gdn_conv_kernel, tm=tm, tc=tc, seq_len=seq_len, nq=GDN_KEY_DIM // tc,
                          nqk=2 * GDN_KEY_DIM // tc, sample=sample),
        grid=(rows // tm, GDN_CONV_DIM // tc),
        in_specs=[pl.BlockSpec((tm, tc), lambda i, j: (i + rb0, j)), prev_spec,
                  pl.BlockSpec((GDN_CONV, tc), lambda i, j: (0, j))],
        out_specs=pl.BlockSpec((tm, tc), lambda i, j: (i, j)),
        out_shape=jax.ShapeDtypeStruct((rows, GDN_CONV_DIM), F32),
        compiler_params=_cparams(("parallel", "parallel")),
        name="gdn_conv_sample" if sample else "gdn_conv_prompt",
    )(proj, prev, conv_w)


def _pick_col(x, lane, idx):
    return jnp.sum(jnp.where(lane == idx, x, 0.0), axis=1, keepdims=True)


def _gdn_scan_kernel(*refs, rows, seq_rows, nseq, nchunks, has_s0):
    if has_s0:
        q_ref, k_ref, v_ref, z_ref, gb_ref, on_ref, s0_ref, o_ref, sout_ref, s_scr = refs
    else:
        q_ref, k_ref, v_ref, z_ref, gb_ref, on_ref, o_ref, sout_ref, s_scr = refs
    hg = pl.program_id(1)
    c = pl.program_id(2)

    @pl.when(c == 0)
    def _():
        if has_s0:
            s_scr[...] = s0_ref[...]
        else:
            s_scr[...] = jnp.zeros_like(s_scr)

    ii = lax.broadcasted_iota(jnp.int32, (rows, rows), 0)
    jj = lax.broadcasted_iota(jnp.int32, (rows, rows), 1)
    shift = int(math.log2(seq_rows))
    same = jnp.right_shift(ii, shift) == jnp.right_shift(jj, shift)
    incl = jnp.logical_and(same, ii >= jj)
    strict = jnp.logical_and(same, ii > jj)
    eye = ii == jj
    rowseq = jnp.right_shift(lax.broadcasted_iota(jnp.int32, (rows, 1), 0), shift)
    last_in_seq = jnp.bitwise_and(jj, seq_rows - 1) == seq_rows - 1

    gb = gb_ref[...]
    lane_gb = lax.broadcasted_iota(jnp.int32, gb.shape, 1)
    on_w = on_ref[...]
    for hh in range(GDN_HG):
        h = hg * GDN_HG + hh
        g_col = _pick_col(gb, lane_gb, h)
        beta = _pick_col(gb, lane_gb, h + GDN_V_HEADS)
        cum_row = jnp.sum(jnp.where(jnp.logical_and(same, ii <= jj), g_col, 0.0), axis=0, keepdims=True)
        cum_col = jnp.sum(jnp.where(eye, cum_row, 0.0), axis=1, keepdims=True)
        tot_col = jnp.sum(jnp.where(jnp.logical_and(same, last_in_seq), cum_row, 0.0), axis=1, keepdims=True)
        decay = jnp.where(incl, jnp.exp(jnp.where(incl, cum_col - cum_row, 0.0)), 0.0)
        qs = slice((hh // 2) * GDN_HEAD, (hh // 2 + 1) * GDN_HEAD)
        vs = slice(hh * GDN_HEAD, (hh + 1) * GDN_HEAD)
        q = q_ref[:, qs]
        k = k_ref[:, qs]
        v = v_ref[:, vs]
        z = z_ref[:, vs]
        a_mat = jnp.where(strict, beta * decay * _dot_nt(k, k), 0.0)
        t_inv = _unit_lower_inverse(a_mat, ii, jj, seq_rows)
        u_v = _dot(t_inv, beta * v)
        w_k = _dot(t_inv, (beta * jnp.exp(cum_col)) * k)
        p_qk = decay * _dot_nt(q, k)
        q_g = q * jnp.exp(cum_col)
        k_d = k * jnp.exp(tot_col - cum_col)
        g_tot = jnp.exp(tot_col)
        k_d_t = k_d.T
        u = jnp.zeros_like(v)
        o = jnp.zeros_like(v)
        for n in range(nseq):
            s = s_scr[n, hh]
            sb = s.astype(BF16)
            u_n = u_v - _dot(w_k, sb)
            if nseq == 1:
                u, k_t = u_n, k_d_t
                gt = g_tot[rows - 1:rows]
            else:
                mine = rowseq == n
                u_n = jnp.where(mine, u_n, 0.0)
                u = u + u_n
                gt = g_tot[(n + 1) * seq_rows - 1:(n + 1) * seq_rows]
            o_n = _dot(q_g, sb)
            o = o_n if nseq == 1 else o + jnp.where(mine, o_n, 0.0)
            s_scr[n, hh] = gt * s + _dot(k_d_t, u_n)
        o = o + _dot(p_qk, u)
        o = _rms(o, on_w)
        o_ref[:, vs] = o * (z * _sigmoid(z))

    @pl.when(c == nchunks - 1)
    def _():
        sout_ref[...] = s_scr[...]


def _gdn_scan(act, proj, gates, o_norm, s0, layer, *, batch, seq_len, row0, sample):
    hgw = GDN_HG * GDN_HEAD
    qkw = hgw // 2
    if sample:
        nseq, seq_rows, nchunks = SEQ_GROUP, seq_len, 1
        rows = nseq * seq_rows
        grid = (batch // nseq, GDN_V_HEADS // GDN_HG, 1)
    else:
        nseq, seq_rows, rows = 1, GDN_CHUNK, GDN_CHUNK
        nchunks = seq_len // rows
        grid = (batch, GDN_V_HEADS // GDN_HG, nchunks)
    rb0 = row0 // rows
    rmap = lambda b, g, c: b * nchunks + c
    in_specs = [
        pl.BlockSpec((rows, qkw), lambda b, g, c: (rmap(b, g, c), g)),
        pl.BlockSpec((rows, qkw), lambda b, g, c: (rmap(b, g, c), GDN_KEY_DIM // qkw + g)),
        pl.BlockSpec((rows, hgw), lambda b, g, c: (rmap(b, g, c), 2 * GDN_KEY_DIM // hgw + g)),
        pl.BlockSpec((rows, hgw), lambda b, g, c: (rmap(b, g, c) + rb0, GDN_CONV_DIM // hgw + g)),
        pl.BlockSpec((rows, 2 * GDN_V_HEADS), lambda b, g, c: (rmap(b, g, c) + rb0, 0)),
        pl.BlockSpec((1, GDN_HEAD), lambda b, g, c: (0, 0)),
    ]
    args = [act, act, act, proj, gates, o_norm.reshape(1, GDN_HEAD)]
    if sample:
        in_specs.append(pl.BlockSpec((nseq, None, GDN_HG, GDN_HEAD, GDN_HEAD),
                                     lambda b, g, c: (b, layer, g, 0, 0)))
        args.append(s0)
    return pl.pallas_call(
        functools.partial(_gdn_scan_kernel, rows=rows, seq_rows=seq_rows, nseq=nseq, nchunks=nchunks,
                          has_s0=sample),
        grid=grid,
        in_specs=in_specs,
        out_specs=[pl.BlockSpec((rows, hgw), lambda b, g, c: (rmap(b, g, c), g)),
                   pl.BlockSpec((nseq, GDN_HG, GDN_HEAD, GDN_HEAD), lambda b, g, c: (b, g, 0, 0))],
        out_shape=[jax.ShapeDtypeStruct((batch * seq_len, GDN_VAL_DIM), F32),
                   jax.ShapeDtypeStruct((batch, GDN_V_HEADS, GDN_HEAD, GDN_HEAD), F32)],
        scratch_shapes=[pltpu.VMEM((nseq, GDN_HG, GDN_HEAD, GDN_HEAD), F32)],
        compiler_params=_cparams(("parallel", "parallel", "arbitrary")),
        name="gdn_scan_sample" if sample else "gdn_scan_prompt",
    )(*args)


def _mla_post_kernel(ckv_in_ref, kpe_in_ref, kpe_sw_ref, qpe_ref, qsw_ref, cos_ref, sin_ref, g_ref,
                     ckv_ref, kpe_ref, qp_ref):
    ckv_ref[...] = _rms(ckv_in_ref[...], g_ref[...])
    cos = cos_ref[...]
    sin = sin_ref[...]
    kpe_ref[...] = kpe_in_ref[...] * cos + kpe_sw_ref[...] * sin
    for h in range(MLA_HEADS):
        sl = slice(h * LANES, (h + 1) * LANES)
        qp_ref[:, sl] = qpe_ref[:, sl] * cos + qsw_ref[:, sl] * sin


def _mla_post(c, q_raw, cos2, sin2, kv_norm, *, tm):
    t = c.shape[0]
    hw = MLA_HEADS * LANES
    c0 = MLA_Q_RANK // MLA_KV_RANK
    k0 = (MLA_Q_RANK + MLA_KV_RANK) // LANES
    row = lambda w, col: pl.BlockSpec((tm, w), lambda i: (i, col))
    return pl.pallas_call(
        _mla_post_kernel,
        grid=(t // tm,),
        in_specs=[row(MLA_KV_RANK, c0), row(LANES, k0), row(LANES, k0 + 1),
                  row(hw, 1), row(hw, 2), row(LANES, 0), row(LANES, 0),
                  pl.BlockSpec((1, MLA_KV_RANK), lambda i: (0, 0))],
        out_specs=[row(MLA_KV_RANK, 0), row(LANES, 0), row(hw, 0)],
        out_shape=[jax.ShapeDtypeStruct((t, MLA_KV_RANK), F32), jax.ShapeDtypeStruct((t, LANES), F32),
                   jax.ShapeDtypeStruct((t, hw), F32)],
        compiler_params=_cparams(("parallel",)),
        name="mla_post",
    )(c, c, c, q_raw, q_raw, cos2, sin2, kv_norm.reshape(1, MLA_KV_RANK))


def _flash_kernel(qn_ref, qp_ref, kn_ref, kp_ref, v_ref, o_ref, qs_ref, m_ref, l_ref, acc_ref, *, tq, scale):
    qi = pl.program_id(2)
    ki = pl.program_id(3)

    @pl.when(ki == 0)
    def _():
        qs_ref[:, :LANES] = (qn_ref[...] * scale).astype(BF16)
        qs_ref[:, LANES:] = (qp_ref[...] * scale).astype(BF16)
        m_ref[...] = jnp.full_like(m_ref, -jnp.inf)
        l_ref[...] = jnp.zeros_like(l_ref)
        acc_ref[...] = jnp.zeros_like(acc_ref)

    @pl.when(ki <= qi)
    def _():
        s = _dot_nt(qs_ref[:, :LANES], kn_ref[...]) + _dot_nt(qs_ref[:, LANES:], kp_ref[...])
        row = lax.broadcasted_iota(jnp.int32, s.shape, 0)
        col = lax.broadcasted_iota(jnp.int32, s.shape, 1)
        s = jnp.where(jnp.logical_or(ki < qi, row >= col), s, -jnp.inf)
        m_new = jnp.maximum(m_ref[...], jnp.max(s, axis=-1, keepdims=True))
        alpha = jnp.exp(m_ref[...] - m_new)
        p = jnp.exp(s - m_new)
        l_ref[...] = alpha * l_ref[...] + jnp.sum(p, axis=-1, keepdims=True)
        acc_ref[...] = alpha * acc_ref[...] + _dot(p, v_ref[...])
        m_ref[...] = m_new

    @pl.when(ki == qi)
    def _():
        o_ref[...] = acc_ref[...] / l_ref[...]


def _mla_flash(q_raw, qp, kv, kpe, *, batch, seq_len):
    tq = _row_tile(seq_len)
    nq = seq_len // tq
    qmap = lambda b, h, qi, ki: (b * nq + qi, h)
    kmap = lambda off: (lambda b, h, qi, ki: (b * nq + jnp.minimum(ki, qi), off + h))
    return pl.pallas_call(
        functools.partial(_flash_kernel, tq=tq, scale=MLA_QK ** -0.5),
        grid=(batch, MLA_HEADS, nq, nq),
        in_specs=[pl.BlockSpec((tq, LANES), qmap), pl.BlockSpec((tq, LANES), qmap),
                  pl.BlockSpec((tq, LANES), kmap(0)),
                  pl.BlockSpec((tq, LANES), lambda b, h, qi, ki: (b * nq + jnp.minimum(ki, qi), 0)),
                  pl.BlockSpec((tq, LANES), kmap(MLA_HEADS))],
        out_specs=pl.BlockSpec((tq, LANES), qmap),
        out_shape=jax.ShapeDtypeStruct((batch * seq_len, MLA_HEADS * MLA_V), F32),
        scratch_shapes=[pltpu.VMEM((tq, 2 * LANES), BF16), pltpu.VMEM((tq, 1), F32),
                        pltpu.VMEM((tq, 1), F32), pltpu.VMEM((tq, MLA_V), F32)],
        compiler_params=_cparams(("parallel", "parallel", "parallel", "arbitrary")),
        name="mla_flash_prompt",
    )(q_raw, qp, kv, kpe, kv)


def _decode_kernel(*refs, npg, ngroups, new_len, scale):
    pt_ref = refs[0]
    ql_ref, qp_ref = refs[1], refs[2]
    ckv_refs = refs[3:3 + npg]
    kpe_refs = refs[3 + npg:3 + 2 * npg]
    cnew_ref, knew_ref = refs[3 + 2 * npg], refs[4 + 2 * npg]
    o_ref = refs[5 + 2 * npg]
    qls_ref, qps_ref, m_ref, l_ref, acc_ref = refs[6 + 2 * npg:]
    del pt_ref
    g = pl.program_id(1)

    @pl.when(g == 0)
    def _():
        qls_ref[...] = (ql_ref[...] * scale).astype(BF16)
        qps_ref[...] = (qp_ref[:, :MLA_ROPE] * scale).astype(BF16)
        m_ref[...] = jnp.full_like(m_ref, -jnp.inf)
        l_ref[...] = jnp.zeros_like(l_ref)
        acc_ref[...] = jnp.zeros_like(acc_ref)

    def update(keys, s_parts):
        s = jnp.concatenate(s_parts, axis=-1) if len(s_parts) > 1 else s_parts[0]
        m_new = jnp.maximum(m_ref[...], jnp.max(s, axis=-1, keepdims=True))
        alpha = jnp.exp(m_ref[...] - m_new)
        p = jnp.exp(s - m_new)
        l_ref[...] = alpha * l_ref[...] + jnp.sum(p, axis=-1, keepdims=True)
        acc = alpha * acc_ref[...]
        for i, kb in enumerate(keys):
            acc = acc + jnp.dot(p[:, i * PAGE_SIZE:(i + 1) * PAGE_SIZE].astype(BF16), kb,
                                preferred_element_type=F32)
        acc_ref[...] = acc
        m_ref[...] = m_new

    keys, parts = [], []
    for i in range(npg):
        kb = ckv_refs[i][...].astype(BF16)
        keys.append(kb)
        parts.append(_dot_nt(qls_ref[...], kb) + _dot_nt(qps_ref[...], kpe_refs[i][...]))
    update(keys, parts)

    @pl.when(g == ngroups - 1)
    def _():
        kb = cnew_ref[...].astype(BF16)
        s = _dot_nt(qls_ref[...], kb) + _dot_nt(qps_ref[...], knew_ref[...])
        tq = jnp.right_shift(lax.broadcasted_iota(jnp.int32, s.shape, 0), int(math.log2(MLA_HEADS)))
        tk = lax.broadcasted_iota(jnp.int32, s.shape, 1)
        s = jnp.where(jnp.logical_and(tk < new_len, tk <= tq), s, -jnp.inf)
        update([kb], [s])
        o_ref[...] = acc_ref[...] / l_ref[...]


def _mla_decode(q_lat, qp, cache_ckv, cache_kpe, page_table, ckv_new, kpe_new, layer, *, new_len):
    bsz, n_pages = page_table.shape
    nrow = q_lat.shape[1]
    npg = min(DEC_PAGES_PER_STEP, n_pages)
    ngroups = n_pages // npg
    qp_rb0 = qp.shape[0] // nrow - bsz
    page = lambda i, w: pl.BlockSpec((None, None, PAGE_SIZE, w),
                                     lambda b, g, pt: (pt[b, g * npg + i], layer, 0, 0))
    in_specs = [pl.BlockSpec((None, nrow, MLA_KV_RANK), lambda b, g, pt: (b, 0, 0)),
                pl.BlockSpec((nrow, LANES), lambda b, g, pt: (qp_rb0 + b, 0))]
    in_specs += [page(i, MLA_KV_RANK) for i in range(npg)]
    in_specs += [page(i, MLA_ROPE) for i in range(npg)]
    in_specs += [pl.BlockSpec((None, PAGE_SIZE, MLA_KV_RANK), lambda b, g, pt: (b, 0, 0)),
                 pl.BlockSpec((None, PAGE_SIZE, MLA_ROPE), lambda b, g, pt: (b, 0, 0))]
    grid_spec = pltpu.PrefetchScalarGridSpec(
        num_scalar_prefetch=1,
        grid=(bsz, ngroups),
        in_specs=in_specs,
        out_specs=pl.BlockSpec((None, nrow, MLA_KV_RANK), lambda b, g, pt: (b, 0, 0)),
        scratch_shapes=[pltpu.VMEM((nrow, MLA_KV_RANK), BF16), pltpu.VMEM((nrow, MLA_ROPE), BF16),
                        pltpu.VMEM((nrow, 1), F32), pltpu.VMEM((nrow, 1), F32),
                        pltpu.VMEM((nrow, MLA_KV_RANK), F32)],
    )
    return pl.pallas_call(
        functools.partial(_decode_kernel, npg=npg, ngroups=ngroups, new_len=new_len, scale=MLA_QK ** -0.5),
        grid_spec=grid_spec,
        out_shape=jax.ShapeDtypeStruct(q_lat.shape, F32),
        compiler_params=_cparams(("parallel", "arbitrary")),
        name="mla_decode",
    )(page_table, q_lat, qp, *([cache_ckv] * npg), *([cache_kpe] * npg), ckv_new, kpe_new)


def _lora_kernel(x_ref, p_ref, mu_ref, w1_ref, w2_ref, b_ref, o_ref, *, kind):
    x = x_ref[...]
    xm = x + (p_ref[...] - x) * mu_ref[...]
    t = jnp.dot(xm.astype(BF16), w1_ref[...], preferred_element_type=F32)
    if kind == "decay":
        t = jnp.tanh(t)
    elif kind == "gate":
        t = _sigmoid(t)
    y = jnp.dot(t.astype(BF16), w2_ref[...], preferred_element_type=F32)
    if kind == "decay":
        o_ref[...] = -_softplus(-(b_ref[...] + y)) - 0.5
    elif kind == "aaa":
        o_ref[...] = _sigmoid(b_ref[...] + y)
    else:
        o_ref[...] = y


def _lora(h, prev, mu, w1, w2, bias, *, kind, tm):
    t, d = h.shape
    r = w1.shape[1]
    row = pl.BlockSpec((tm, d), lambda i: (i, 0))
    vec = pl.BlockSpec((1, d), lambda i: (0, 0))
    return pl.pallas_call(
        functools.partial(_lora_kernel, kind=kind),
        grid=(t // tm,),
        in_specs=[row, row, vec, pl.BlockSpec((d, r), lambda i: (0, 0)),
                  pl.BlockSpec((r, d), lambda i: (0, 0)), vec],
        out_specs=row,
        out_shape=jax.ShapeDtypeStruct((t, d), F32),
        compiler_params=_cparams(("parallel",)),
        name=f"rwkv_lora_{kind}",
    )(h, prev, mu.reshape(1, d), w1, w2, bias.reshape(1, d))


def _rwkv_scan_kernel(*refs, rows, seq_rows, nseq, nchunks, has_h0):
    r_ref, k_ref, v_ref, w_ref, a_ref, g_ref, kk_ref, ka_ref, rk_ref, lw_ref, lb_ref = refs[:11]
    if has_h0:
        h0_ref, y_ref, hout_ref, h_scr = refs[11:]
    else:
        y_ref, hout_ref, h_scr = refs[11:]
    c = pl.program_id(2)

    @pl.when(c == 0)
    def _():
        if has_h0:
            h_scr[...] = h0_ref[...]
        else:
            h_scr[...] = jnp.zeros_like(h_scr)

    ii = lax.broadcasted_iota(jnp.int32, (rows, rows), 0)
    jj = lax.broadcasted_iota(jnp.int32, (rows, rows), 1)
    shift = int(math.log2(seq_rows))
    same = jnp.right_shift(ii, shift) == jnp.right_shift(jj, shift)
    incl = jnp.logical_and(same, ii >= jj)
    strict = jnp.logical_and(same, ii > jj)
    rowseq = jnp.right_shift(lax.broadcasted_iota(jnp.int32, (rows, 1), 0), shift)
    lane =lax.broadcasted_iota(jnp.int32, (rows, LANES), 1)
    head0 = lane < RWKV_HEAD

    def head_sum(x):
        s0 = jnp.sum(jnp.where(head0, x, 0.0), axis=-1, keepdims=True)
        s1 = jnp.sum(jnp.where(head0, 0.0, x), axis=-1, keepdims=True)
        return jnp.where(head0, s0, s1)

    r = r_ref[...]
    k = k_ref[...]
    v = v_ref[...]
    a = a_ref[...]
    kx = k * kk_ref[...]
    kk = kx * lax.rsqrt(head_sum(kx * kx) + L2_EPS)
    k = k * (1.0 + (a - 1.0) * ka_ref[...])
    b = kk * a
    log_d = -jnp.exp(w_ref[...])
    cum = _dot_exact_lhs(jnp.where(incl, 1.0, 0.0).astype(BF16), log_d)
    last = jnp.bitwise_and(jj, seq_rows - 1) == seq_rows - 1
    tot = _dot_exact_lhs(jnp.where(jnp.logical_and(same, last), 1.0, 0.0).astype(BF16), cum)
    e_pos = jnp.exp(cum)
    e_neg = jnp.exp(-cum)
    e_end = jnp.exp(tot - cum)
    a_t = -kk * jnp.exp(cum - log_d)
    b_t = b * e_neg
    k_t = k * e_neg
    r_t = r * e_pos
    bd_t = (b * e_end).T
    kd_t = (k * e_end).T
    gamma = jnp.exp(tot)

    i2 = lax.broadcasted_iota(jnp.int32, (LANES, LANES), 0)
    j2 = lax.broadcasted_iota(jnp.int32, (LANES, LANES), 1)
    eye2 = i2 == j2
    block = (i2 < RWKV_HEAD) == (j2 < RWKV_HEAD)

    u = jnp.zeros_like(v)
    y = jnp.zeros_like(v)
    heads = []
    for hh in range(2):
        m = head0 if hh == 0 else jnp.logical_not(head0)
        a_m = jnp.where(m, a_t, 0.0)
        r_m = jnp.where(m, r_t, 0.0)
        a_ab = jnp.where(strict, _dot_nt(a_m, b_t), 0.0)
        a_ak = jnp.where(strict, _dot_nt(a_m, k_t), 0.0)
        p_rb = jnp.where(incl, _dot_nt(r_m, b_t), 0.0)
        p_rk = jnp.where(incl, _dot_nt(r_m, k_t), 0.0)
        t_inv = _unit_lower_inverse(-a_ab, ii, jj, seq_rows)
        heads.append((m, a_m, r_m, t_inv, _dot(a_ak, v), p_rb, _dot(p_rk, v)))

    for n in range(nseq):
        hs = h_scr[n]
        hb = hs.astype(BF16)
        mine = rowseq == n
        u_n = jnp.zeros_like(v)
        for m, a_m, r_m, t_inv, akv, p_rb, prkv in heads:
            u_h = _dot(t_inv, _dot(a_m, hb) + akv)
            y_h = _dot(r_m, hb) + _dot(p_rb, u_h) + prkv
            sel = m if nseq == 1 else jnp.logical_and(m, mine)
            u_n = jnp.where(sel, u_h, u_n)
            y = jnp.where(sel, y_h, y)
        v_n = v if nseq == 1 else jnp.where(mine, v, 0.0)
        g_row = gamma[(n + 1) * seq_rows - 1:(n + 1) * seq_rows]
        g_col = jnp.sum(jnp.where(eye2, g_row, 0.0), axis=1, keepdims=True)
        h_scr[n] = g_col * hs + jnp.where(block, _dot(bd_t, u_n) + _dot(kd_t, v_n), 0.0)

    mean = head_sum(y) * (1.0 / RWKV_HEAD)
    d = y - mean
    var = head_sum(d * d) * (1.0 / RWKV_HEAD)
    yn = d * lax.rsqrt(var + RWKV_GN_EPS) * lw_ref[...] + lb_ref[...]
    bonus = head_sum(r * k * rk_ref[...]) * v
    y_ref[...] = (yn + bonus) * g_ref[...]

    @pl.when(c == nchunks - 1)
    def _():
        hout_ref[...] = h_scr[...]


def _rwkv_scan(r, k, v, w, a, g, k_k, k_a, r_k, ln_w, ln_b, h0, *, batch, seq_len, row0, sample):
    d = r.shape[1]
    npairs = d // LANES
    if sample:
        nseq, seq_rows, nchunks = SEQ_GROUP, seq_len, 1
        rows = nseq * seq_rows
        grid = (batch // nseq, npairs, 1)
    else:
        nseq, seq_rows, rows = 1, RWKV_CHUNK, RWKV_CHUNK
        nchunks = seq_len // rows
        grid = (batch, npairs, nchunks)
    rb0 = row0 // rows
    tile = pl.BlockSpec((rows, LANES), lambda b, p, c: (b * nchunks + c + rb0, p))
    vec = pl.BlockSpec((1, LANES), lambda b, p, c: (0, p))
    state = pl.BlockSpec((nseq, None, LANES, LANES), lambda b, p, c: (b, p, 0, 0))
    in_specs = [tile] * 6 + [vec] * 5
    args = [r, k, v, w, a, g] + [x.reshape(1, d) for x in (k_k, k_a, r_k, ln_w, ln_b)]
    if sample:
        in_specs.append(state)
        args.append(h0)
    return pl.pallas_call(
        functools.partial(_rwkv_scan_kernel, rows=rows, seq_rows=seq_rows, nseq=nseq, nchunks=nchunks,
                          has_h0=sample),
        grid=grid,
        in_specs=in_specs,
        out_specs=[pl.BlockSpec((rows, LANES), lambda b, p, c: (b * nchunks + c, p)), state],
        out_shape=[jax.ShapeDtypeStruct((batch * seq_len, d), F32),
                   jax.ShapeDtypeStruct((batch, npairs, LANES, LANES), F32)],
        scratch_shapes=[pltpu.VMEM((nseq, LANES, LANES), F32)],
        compiler_params=_cparams(("parallel", "parallel", "arbitrary")),
        name="rwkv_scan_sample" if sample else "rwkv_scan_prompt",
    )(*args)


def _pairs_from_heads(s):
    b, h, n, _ = s.shape
    st = jnp.swapaxes(s, -1, -2).reshape(b, h // 2, 2, n, n)
    bd = jnp.einsum("bpeij,ef->bpeifj", st, jnp.eye(2, dtype=s.dtype))
    return bd.reshape(b, h // 2, 2 * n, 2 * n)


def _heads_from_pairs(hp):
    b, p, n2, _ = hp.shape
    n = n2 // 2
    hr = hp.reshape(b, p, 2, n, 2, n)
    st = jnp.stack([hr[:, :, 0, :, 0, :], hr[:, :, 1, :, 1, :]], axis=2).reshape(b, 2 * p, n, n)
    return jnp.swapaxes(st, -1, -2)


def _pad_cols(w, n):
    return jnp.pad(w, ((0, 0), (0, n - w.shape[1])))


def kernel(x_prompt, x_sample, cache_mla_ckv, cache_mla_kpe, page_table, state_gdn_s, state_gdn_conv, state_rwkv_wkv, state_rwkv_shift, norm_w, gdn_w_in, gdn_conv_w, gdn_a_log, gdn_dt_bias, gdn_o_norm, gdn_w_out, mla_w_in, mla_q_norm, mla_w_uq, mla_kv_norm, mla_w_uk, mla_w_uv, mla_w_o, rw_mu, rw_w_rkv, rw_w0, rw_w1, rw_w2, rw_a0, rw_a1, rw_a2, rw_g1, rw_g2, rw_k_k, rw_k_a, rw_r_k, rw_ln_w, rw_ln_b, rw_w_o, mlp_w_up, mlp_w_down):
    bp, lp, d = x_prompt.shape
    bs, ls, _ = x_sample.shape
    tp, ts = bp * lp, bs * ls
    depth = norm_w.shape[0]
    n_pages = page_table.shape[1]
    past_len = n_pages * PAGE_SIZE
    tm = _row_tile(tp, ts)
    assert ls == SUBLANES and bs % SEQ_GROUP == 0 and lp % GDN_CHUNK == 0 and lp % RWKV_CHUNK == 0

    x = jnp.concatenate([x_prompt.reshape(tp, d), x_sample.reshape(ts, d)], axis=0)

    half = MLA_ROPE // 2
    inv_freq = 1.0 / (ROPE_THETA ** (jnp.arange(half, dtype=F32) / half))
    pos = jnp.concatenate([jnp.tile(jnp.arange(lp), bp), jnp.tile(past_len + jnp.arange(ls), bs)])
    ang = pos.astype(F32)[:, None] * inv_freq[None, :]
    cos, sin = jnp.cos(ang), jnp.sin(ang)
    zpad = jnp.zeros((tp + ts, LANES - MLA_ROPE), F32)
    cos2 = jnp.concatenate([cos, cos, zpad], axis=1)
    sin2 = jnp.concatenate([-sin, sin, zpad], axis=1)

    outs = {k: [] for k in ("ckv_p", "kpe_p", "ckv_s", "kpe_s", "gs_p", "gc_p", "gs_s", "gc_s",
                            "rs_p", "rx_p", "rs_s", "rx_s")}
    for i in range(depth):
        kind, j = i % 3, i // 3
        if kind == 0:
            w_in = gdn_w_in[j]
            w_main = w_in[:, :GDN_CONV_DIM + GDN_VAL_DIM].astype(BF16)
            nb = GDN_CONV_DIM + GDN_VAL_DIM
            w_ab = jnp.concatenate([w_in[:, nb + GDN_V_HEADS:], w_in[:, nb:nb + GDN_V_HEADS]], axis=1).astype(BF16)
            proj = _linear(x, w_main, mode="norm", g=norm_w[i, 0], tm=tm)
            ab = _linear(x, w_ab, mode="norm", g=norm_w[i, 0], tm=tm)
            gates = _gdn_gates(ab, gdn_a_log[j], gdn_dt_bias[j], tm=tm)
            hist = jnp.pad(state_gdn_conv[:, j], ((0, 0), (SUBLANES - GDN_CONV + 1, 0), (0, 0)))
            hist = hist.reshape(ts, GDN_CONV_DIM)
            act_p = _gdn_conv(proj, None, gdn_conv_w[j], row0=0, rows=tp, seq_len=lp, sample=False, tm=tm)
            act_s = _gdn_conv(proj, hist, gdn_conv_w[j], row0=tp, rows=ts, seq_len=ls, sample=True, tm=tm)
            y_p, s_p = _gdn_scan(act_p, proj, gates, gdn_o_norm[j], None, j, batch=bp, seq_len=lp,
                                 row0=0, sample=False)
            y_s, s_s = _gdn_scan(act_s, proj, gates, gdn_o_norm[j], state_gdn_s, j, batch=bs, seq_len=ls,
                                 row0=tp, sample=True)
            w_out = gdn_w_out[j].astype(BF16)
            qkv_p = proj[:tp, :GDN_CONV_DIM].reshape(bp, lp, GDN_CONV_DIM)
            qkv_s = proj[tp:, :GDN_CONV_DIM].reshape(bs, ls, GDN_CONV_DIM)
            outs["gs_p"].append(s_p)
            outs["gc_p"].append(qkv_p[:, lp - (GDN_CONV - 1):])
            outs["gs_s"].append(s_s)
            outs["gc_s"].append(qkv_s[:, ls - (GDN_CONV - 1):])
        elif kind == 1:
            w_in = mla_w_in[j]
            w_kpe = w_in[:, MLA_Q_RANK + MLA_KV_RANK:]
            w_c = jnp.concatenate([w_in[:, :MLA_Q_RANK + MLA_KV_RANK], _pad_cols(w_kpe, LANES),
                                   _pad_cols(jnp.roll(w_kpe, half, axis=1), LANES)], axis=1).astype(BF16)
            uq = mla_w_uq[j].reshape(MLA_Q_RANK, MLA_HEADS, MLA_QK)
            uq_pe = uq[:, :, MLA_NOPE:]
            padh = ((0, 0), (0, 0), (0, LANES - MLA_ROPE))
            w_q = jnp.concatenate([
                uq[:, :, :MLA_NOPE].reshape(MLA_Q_RANK, -1),
                jnp.pad(uq_pe, padh).reshape(MLA_Q_RANK, -1),
                jnp.pad(jnp.roll(uq_pe, half, axis=2), padh).reshape(MLA_Q_RANK, -1)], axis=1).astype(BF16)
            c = _linear(x, w_c, mode="norm", g=norm_w[i, 0], tm=tm)
            q_raw = _linear(c, w_q, mode="norm", g=mla_q_norm[j], tm=tm)
            ckv, kpe, qp = _mla_post(c, q_raw, cos2, sin2, mla_kv_norm[j], tm=min(tm, 256))
            w_kv = jnp.concatenate([mla_w_uk[j].reshape(MLA_KV_RANK, -1),
                                    mla_w_uv[j].reshape(MLA_KV_RANK, -1)], axis=1).astype(BF16)
            kv = _linear(ckv, w_kv, rows=tp, tm=tm)
            y_p = _mla_flash(q_raw, qp, kv, kpe, batch=bp, seq_len=lp)
            w_ukt = jnp.transpose(mla_w_uk[j], (1, 2, 0)).astype(BF16)
            w_uvh = jnp.transpose(mla_w_uv[j], (1, 0, 2)).astype(BF16)
            q_lat = _head_linear(q_raw, w_ukt, row0=tp, rows=ts, tm=tm)
            pad_keys = ((0, 0), (0, PAGE_SIZE - ls), (0, 0))
            ckv_new = jnp.pad(ckv[tp:].reshape(bs, ls, MLA_KV_RANK), pad_keys)
            kpe_new = jnp.pad(kpe[tp:, :MLA_ROPE].reshape(bs, ls, MLA_ROPE), pad_keys)
            o_lat = _mla_decode(q_lat.reshape(bs, ls * MLA_HEADS, MLA_KV_RANK),
                                qp.reshape((tp + ts) * MLA_HEADS, LANES), cache_mla_ckv, cache_mla_kpe,
                                page_table, ckv_new, kpe_new, j, new_len=ls)
            y_s = _head_linear(o_lat.reshape(ts, MLA_HEADS * MLA_KV_RANK), w_uvh, row0=0, rows=ts, tm=tm)
            w_out = mla_w_o[j].astype(BF16)
            outs["ckv_p"].append(ckv[:tp].reshape(bp, lp, MLA_KV_RANK))
            outs["kpe_p"].append(kpe[:tp, :MLA_ROPE].reshape(bp, lp, MLA_ROPE))
            outs["ckv_s"].append(ckv[tp:].reshape(bs, ls, MLA_KV_RANK))
            outs["kpe_s"].append(kpe[tp:, :MLA_ROPE].reshape(bs, ls, MLA_ROPE))
        else:
            h = _norm_rows(x, norm_w[i, 0], tm=tm)
            h_p = h[:tp].reshape(bp, lp, d)
            h_s = h[tp:].reshape(bs, ls, d)
            prev_p = jnp.pad(h_p[:, :-1], ((0, 0), (1, 0), (0, 0)))
            prev_s = jnp.concatenate([state_rwkv_shift[:, j][:, None, :], h_s[:, :-1]], axis=1)
            prev = jnp.concatenate([prev_p.reshape(tp, d), prev_s.reshape(ts, d)], axis=0)
            mu = rw_mu[j]
            wr, wk, wv = (rw_w_rkv[j, s].astype(BF16) for s in range(3))
            r = _linear(h, wr, mode="mix", prev=prev, mu=mu[0], tm=tm)
            k = _linear(h, wk, mode="mix", prev=prev, mu=mu[1], tm=tm)
            v = _linear(h, wv, mode="mix", prev=prev, mu=mu[2], tm=tm)
            rank = lambda n: -(-n // LANES) * LANES
            lora_w = lambda w1, w2: (_pad_cols(w1, rank(w1.shape[1])).astype(BF16),
                                     jnp.pad(w2, ((0, rank(w2.shape[0]) - w2.shape[0]), (0, 0))).astype(BF16))
            w = _lora(h, prev, mu[3], *lora_w(rw_w1[j], rw_w2[j]), rw_w0[j], kind="decay", tm=tm)
            a = _lora(h, prev, mu[4], *lora_w(rw_a1[j], rw_a2[j]), rw_a0[j], kind="aaa", tm=tm)
            g = _lora(h, prev, mu[5], *lora_w(rw_g1[j], rw_g2[j]), jnp.zeros((d,), F32), kind="gate", tm=tm)
            params = (rw_k_k[j], rw_k_a[j], rw_r_k[j].reshape(d), rw_ln_w[j], rw_ln_b[j])
            y_p, hp_p = _rwkv_scan(r, k, v, w, a, g, *params, None, batch=bp, seq_len=lp, row0=0, sample=False)
            y_s, hp_s = _rwkv_scan(r, k, v, w, a, g, *params, _pairs_from_heads(state_rwkv_wkv[:, j]),
                                   batch=bs, seq_len=ls, row0=tp, sample=True)
            w_out = rw_w_o[j].astype(BF16)
            outs["rs_p"].append(_heads_from_pairs(hp_p))
            outs["rx_p"].append(h_p[:, -1])
            outs["rs_s"].append(_heads_from_pairs(hp_s))
            outs["rx_s"].append(h_s[:, -1])
        x = _out_proj(y_p, y_s, w_out, x, norm_w[i, 1], tm=tm)
        x = _mlp(x, norm_w[i, 2], mlp_w_up[i].astype(BF16), mlp_w_down[i].astype(BF16), norm_w[i, 3], tm=tm)

    stack = lambda key: jnp.stack(outs[key], axis=1)
    return (x[:tp].reshape(bp, lp, d), x[tp:].reshape(bs, ls, d),
            stack("ckv_p"), stack("kpe_p"), stack("gs_p"), stack("gc_p"), stack("rs_p"), stack("rx_p"),
            stack("ckv_s"), stack("kpe_s"), stack("gs_s"), stack("gc_s"), stack("rs_s"), stack("rx_s"))
```

```python
import functools
import math

import jax
import jax.numpy as jnp
from jax import lax
from jax.experimental import pallas as pl
from jax.experimental.pallas import tpu as pltpu

F32 = jnp.float32
BF16 = jnp.bfloat16

NORM_EPS = 1e-6
L2_EPS = 1e-6
RWKV_GN_EPS = 64e-5
ROPE_THETA = 10000.0

LANES = 128
SUBLANES = 8

GDN_QK_HEADS = 16
GDN_V_HEADS = 32
GDN_HEAD = 128
GDN_CONV = 4
GDN_CHUNK = 64
GDN_KEY_DIM = GDN_QK_HEADS * GDN_HEAD
GDN_VAL_DIM = GDN_V_HEADS * GDN_HEAD
GDN_CONV_DIM = 2 * GDN_KEY_DIM + GDN_VAL_DIM
GDN_HB_PROMPT = 32
GDN_HB_SAMPLE = 4

MLA_HEADS = 16
MLA_Q_RANK = 512
MLA_KV_RANK = 512
MLA_NOPE = 128
MLA_ROPE = 64
MLA_V = 128
MLA_QK = MLA_NOPE + MLA_ROPE
PAGE_SIZE = 128
DEC_PAGES_PER_STEP = 16
MLA_FLASH_TILE = 1024

RWKV_HEAD = 64
RWKV_CHUNK = 64
RWKV_PB_PROMPT = 16
RWKV_PB_SAMPLE = 4
SEQ_GROUP = 8

VMEM_LIMIT = 48 * 1024 * 1024


def _cparams(sem):
    return pltpu.CompilerParams(dimension_semantics=sem, vmem_limit_bytes=VMEM_LIMIT)


def _row_tile(*row_counts):
    for t in (512, 256, 128, 64, 32, 16, 8):
        if all(r % t == 0 for r in row_counts):
            return t
    raise ValueError(f"row counts {row_counts} are not multiples of {SUBLANES}")


def _col_tile(n):
    for t in (1024, 640, 512, 384, 256, 128):
        if n % t == 0:
            return t
    return n


def _dot(a, b):
    return jnp.dot(a.astype(BF16), b.astype(BF16), preferred_element_type=F32)


def _dot_nt(a, b):
    return lax.dot_general(a.astype(BF16), b.astype(BF16), (((1,), (1,)), ((), ())),
                           preferred_element_type=F32)


def _mm(a, b):
    if a.ndim == 2:
        return _dot(a, b)
    return jnp.einsum("bmk,bkn->bmn", a.astype(BF16), b.astype(BF16), preferred_element_type=F32)


def _mm_nt(a, b):
    if a.ndim == 2:
        return _dot_nt(a, b)
    return jnp.einsum("bmk,bnk->bmn", a.astype(BF16), b.astype(BF16), preferred_element_type=F32)


def _split3(x):
    hi = x.astype(BF16)
    r1 = x - hi.astype(F32)
    mid = r1.astype(BF16)
    lo = (r1 - mid.astype(F32)).astype(BF16)
    return hi, mid, lo


def _dot_exact_lhs(a_bf16, x):
    hi, mid, lo = _split3(x)
    return (jnp.dot(a_bf16, hi, preferred_element_type=F32)
            + jnp.dot(a_bf16, mid, preferred_element_type=F32)
            + jnp.dot(a_bf16, lo, preferred_element_type=F32))


def _dot_exact_rhs(x, b_bf16):
    hi, mid, lo = _split3(x)
    return (jnp.dot(hi, b_bf16, preferred_element_type=F32)
            + jnp.dot(mid, b_bf16, preferred_element_type=F32)
            + jnp.dot(lo, b_bf16, preferred_element_type=F32))


def _rms(x, g, eps=NORM_EPS):
    return x * lax.rsqrt(jnp.mean(x * x, axis=-1, keepdims=True) + eps) * g


def _sigmoid(x):
    return 1.0 / (1.0 + jnp.exp(-x))


def _softplus(x):
    return jnp.maximum(x, 0.0) + jnp.log(1.0 + jnp.exp(-jnp.abs(x)))


def _unit_lower_inverse(a, ii, jj, seq_rows):
    base = min(SUBLANES, seq_rows)
    sh = int(math.log2(base))
    a0 = jnp.where(jnp.right_shift(ii, sh) == jnp.right_shift(jj, sh), a, 0.0)
    x = jnp.where(ii == jj, 1.0, 0.0) - a0
    p = a0
    k = 1
    while 2 * k < base:
        p = _mm(p, p)
        x = x + _mm(x, p)
        k *= 2
    s = base
    while s < seq_rows:
        sh = int(math.log2(s))
        lower_left = jnp.logical_and(
            jnp.right_shift(ii, sh + 1) == jnp.right_shift(jj, sh + 1),
            jnp.logical_and(jnp.bitwise_and(jnp.right_shift(ii, sh), 1) == 1,
                            jnp.bitwise_and(jnp.right_shift(jj, sh), 1) == 0))
        e = jnp.where(lower_left, a, 0.0)
        x = x - _mm(_mm(x, e), x)
        s *= 2
    return x


def _linear_kernel(*refs, mode):
    if mode == "norm":
        x_ref, g_ref, w_ref, o_ref, xs_ref = refs
    elif mode == "mix":
        x_ref, p_ref, mu_ref, w_ref, o_ref, xs_ref = refs
    else:
        x_ref, w_ref, o_ref, xs_ref = refs

    @pl.when(pl.program_id(1) == 0)
    def _():
        x = x_ref[...]
        if mode == "norm":
            x = _rms(x, g_ref[...])
        elif mode == "mix":
            x = x + (p_ref[...] - x) * mu_ref[...]
        xs_ref[...] = x.astype(BF16)

    o_ref[...] = jnp.dot(xs_ref[...], w_ref[...], preferred_element_type=F32)


def _linear(x, w, *, mode="none", g=None, prev=None, mu=None, xcol=0, row0=0, rows=None, tm):
    k, n = w.shape
    rows = x.shape[0] - row0 if rows is None else rows
    tn = _col_tile(n)
    rb0 = row0 // tm
    xmap = lambda i, j: (i + rb0, xcol)
    vec = pl.BlockSpec((1, k), lambda i, j: (0, 0))
    in_specs, args = [pl.BlockSpec((tm, k), xmap)], [x]
    if mode == "norm":
        in_specs.append(vec)
        args.append(g.reshape(1, k))
    elif mode == "mix":
        in_specs += [pl.BlockSpec((tm, k), xmap), vec]
        args += [prev, mu.reshape(1, k)]
    in_specs.append(pl.BlockSpec((k, tn), lambda i, j: (0, j)))
    args.append(w)
    return pl.pallas_call(
        functools.partial(_linear_kernel, mode=mode),
        grid=(rows // tm, n // tn),
        in_specs=in_specs,
        out_specs=pl.BlockSpec((tm, tn), lambda i, j: (i, j)),
        out_shape=jax.ShapeDtypeStruct((rows, n), F32),
        scratch_shapes=[pltpu.VMEM((tm, k), BF16)],
        compiler_params=_cparams(("parallel", "arbitrary")),
        name=f"linear_{mode}",
    )(*args)


def _norm_shift_kernel(x_ref, x8_ref, st_ref, g_ref, h_ref, p_ref, *, tm, npb, seq_len):
    i = pl.program_id(0)
    g = g_ref[...]
    h = _rms(x_ref[...], g)
    h_ref[...] = h
    rolled = pltpu.roll(h, 1, 0)
    row = lax.broadcasted_iota(jnp.int32, h.shape, 0)
    first = lax.rem(i * tm, seq_len) == 0
    carry = jnp.where(first, 0.0, _rms(x8_ref[...], g)[SUBLANES - 1:SUBLANES])
    prev_p = jnp.where(row == 0, carry, rolled)
    prev_s = jnp.where(jnp.bitwise_and(row, SUBLANES - 1) == 0, st_ref[...], rolled)
    p_ref[...] = jnp.where(i < npb, prev_p, prev_s)


def _norm_shift(x, g, shift_rows, *, tp, seq_len, tm):
    t, d = x.shape
    npb = tp // tm
    per = tm // SUBLANES
    row = pl.BlockSpec((tm, d), lambda i: (i, 0))
    return pl.pallas_call(
        functools.partial(_norm_shift_kernel, tm=tm, npb=npb, seq_len=seq_len),
        grid=(t // tm,),
        in_specs=[row, pl.BlockSpec((SUBLANES, d), lambda i: (jnp.maximum(i * per - 1, 0), 0)),
                  pl.BlockSpec((tm, d), lambda i: (jnp.maximum(i - npb, 0), 0)),
                  pl.BlockSpec((1, d), lambda i: (0, 0))],
        out_specs=[row, row],
        out_shape=[jax.ShapeDtypeStruct((t, d), F32), jax.ShapeDtypeStruct((t, d), F32)],
        compiler_params=_cparams(("parallel",)),
        name="rms_norm_shift",
    )(x, x, shift_rows, g.reshape(1, d))


def _out_proj_kernel(yp_ref, ys_ref, w_ref, r_ref, g_ref, o_ref, acc_ref, *, npb, nk):
    i = pl.program_id(0)
    k = pl.program_id(1)

    @pl.when(k == 0)
    def _():
        acc_ref[...] = jnp.zeros_like(acc_ref)

    @pl.when(i < npb)
    def _():
        acc_ref[...] += jnp.dot(yp_ref[...].astype(BF16), w_ref[...], preferred_element_type=F32)

    @pl.when(i >= npb)
    def _():
        acc_ref[...] += jnp.dot(ys_ref[...].astype(BF16), w_ref[...], preferred_element_type=F32)

    @pl.when(k == nk - 1)
    def _():
        o_ref[...] = r_ref[...] + _rms(acc_ref[...], g_ref[...])


def _out_proj(y_p, y_s, w, resid, g, *, tm):
    kdim, d = w.shape
    tk = _col_tile(kdim)
    npb, nsb, nk = y_p.shape[0] // tm, y_s.shape[0] // tm, kdim // tk
    return pl.pallas_call(
        functools.partial(_out_proj_kernel, npb=npb, nk=nk),
        grid=(npb + nsb, nk),
        in_specs=[
            pl.BlockSpec((tm, tk), lambda i, k: (jnp.minimum(i, npb - 1), jnp.where(i < npb, k, nk - 1))),
            pl.BlockSpec((tm, tk), lambda i, k: (jnp.maximum(i - npb, 0), jnp.where(i >= npb, k, 0))),
            pl.BlockSpec((tk, d), lambda i, k: (k, 0)),
            pl.BlockSpec((tm, d), lambda i, k: (i, 0)),
            pl.BlockSpec((1, d), lambda i, k: (0, 0)),
        ],
        out_specs=pl.BlockSpec((tm, d), lambda i, k: (i, 0)),
        out_shape=jax.ShapeDtypeStruct(resid.shape, F32),
        scratch_shapes=[pltpu.VMEM((tm, d), F32)],
        compiler_params=_cparams(("parallel", "arbitrary")),
        name="out_proj_resnorm",
    )(y_p, y_s, w, resid, g.reshape(1, d))


def _mlp_kernel(x_ref, g2_ref, wu_ref, wd_ref, g3_ref, o_ref, xs_ref, acc_ref, *, nf):
    j = pl.program_id(1)

    @pl.when(j == 0)
    def _():
        xs_ref[...] = _rms(x_ref[...], g2_ref[...]).astype(BF16)
        acc_ref[...] = jnp.zeros_like(acc_ref)

    h = jnp.dot(xs_ref[...], wu_ref[...], preferred_element_type=F32)
    h = jnp.square(jnp.maximum(h, 0.0))
    acc_ref[...] += jnp.dot(h.astype(BF16), wd_ref[...], preferred_element_type=F32)

    @pl.when(j == nf - 1)
    def _():
        o_ref[...] = x_ref[...] + _rms(acc_ref[...], g3_ref[...])


def _mlp(x, g2, w_up, w_down, g3, *, tm):
    t, d = x.shape
    f = w_up.shape[1]
    tf = _col_tile(f)
    vec = pl.BlockSpec((1, d), lambda i, j: (0, 0))
    return pl.pallas_call(
        functools.partial(_mlp_kernel, nf=f // tf),
        grid=(t // tm, f // tf),
        in_specs=[pl.BlockSpec((tm, d), lambda i, j: (i, 0)), vec,
                  pl.BlockSpec((d, tf), lambda i, j: (0, j)),
                  pl.BlockSpec((tf, d), lambda i, j: (j, 0)), vec],
        out_specs=pl.BlockSpec((tm, d), lambda i, j: (i, 0)),
        out_shape=jax.ShapeDtypeStruct((t, d), F32),
        scratch_shapes=[pltpu.VMEM((tm, d), BF16), pltpu.VMEM((tm, d), F32)],
        compiler_params=_cparams(("parallel", "arbitrary")),
        name="sq_relu_mlp",
    )(x, g2.reshape(1, d), w_up, w_down, g3.reshape(1, d))


def _head_linear_kernel(x_ref, w_ref, o_ref):
    o_ref[...] = jnp.dot(x_ref[...].astype(BF16), w_ref[...], preferred_element_type=F32)


def _head_linear(x, w, *, row0, rows, tm):
    nh, k, n = w.shape
    rb0 = row0 // tm
    return pl.pallas_call(
        _head_linear_kernel,
        grid=(rows // tm, nh),
        in_specs=[pl.BlockSpec((tm, k), lambda i, h: (i + rb0, h)),
                  pl.BlockSpec((None, k, n), lambda i, h: (h, 0, 0))],
        out_specs=pl.BlockSpec((tm, n), lambda i, h: (i, h)),
        out_shape=jax.ShapeDtypeStruct((rows, nh * n), F32),
        compiler_params=_cparams(("parallel", "parallel")),
        name="head_linear",
    )(x, w)


def _gdn_gate_kernel(x_ref, alog_ref, dtb_ref, e_ref, o_ref):
    x = x_ref[...]
    lane = lax.broadcasted_iota(jnp.int32, x.shape, 1)
    g = -jnp.exp(alog_ref[...]) * _softplus(x + dtb_ref[...])
    gb = jnp.where(lane < GDN_V_HEADS, g, _sigmoid(x))
    o_ref[...] = _dot_exact_rhs(gb, e_ref[...])


def _gdn_gates(ab, a_log, dt_bias, *, tm):
    t, n = ab.shape
    tn = 16 * LANES
    pad = jnp.zeros((GDN_V_HEADS,), F32)
    vec = pl.BlockSpec((1, n), lambda i, j: (0, 0))
    expand = jnp.repeat(jnp.eye(n, dtype=BF16), LANES, axis=1)
    return pl.pallas_call(
        _gdn_gate_kernel,
        grid=(t // tm, n * LANES // tn),
        in_specs=[pl.BlockSpec((tm, n), lambda i, j: (i, 0)), vec, vec,
                  pl.BlockSpec((n, tn), lambda i, j: (0, j))],
        out_specs=pl.BlockSpec((tm, tn), lambda i, j: (i, j)),
        out_shape=jax.ShapeDtypeStruct((t, n * LANES), F32),
        compiler_params=_cparams(("parallel", "parallel")),
        name="gdn_gates",
    )(ab, jnp.concatenate([a_log, pad]).reshape(1, n), jnp.concatenate([dt_bias, pad]).reshape(1, n), expand)


def _gdn_conv_kernel(x_ref, prev_ref, w_ref, o_ref, *, tm, tc, seq_len, nq, nqk, sample):
    i = pl.program_id(0)
    j = pl.program_id(1)
    x = x_ref[...]
    w = w_ref[...]
    acc = x * w[GDN_CONV - 1:GDN_CONV]
    if sample:
        hist = prev_ref[...]
        tok = jnp.bitwise_and(lax.broadcasted_iota(jnp.int32, x.shape, 0), SUBLANES - 1)
        for s in range(1, GDN_CONV):
            xs = jnp.where(tok >= s, pltpu.roll(x, s, 0), pltpu.roll(hist, tm - SUBLANES + s, 0))
            acc = acc + xs * w[GDN_CONV - 1 - s:GDN_CONV - s]
    else:
        first = lax.rem(i * tm, seq_len) == 0
        prev = jnp.where(first, 0.0, prev_ref[...])
        row = lax.broadcasted_iota(jnp.int32, prev.shape, 0)
        for s in range(1, GDN_CONV):
            xs = pltpu.roll(x, s, 0)
            top = jnp.where(row < s, pltpu.roll(prev, s, 0), xs[:SUBLANES])
            xs = jnp.concatenate([top, xs[SUBLANES:]], axis=0) if tm > SUBLANES else top
            acc = acc + xs * w[GDN_CONV - 1 - s:GDN_CONV - s]
    y = acc * _sigmoid(acc)
    is_qk = j < nqk
    qscale = jnp.where(j < nq, GDN_HEAD ** -0.5, 1.0)
    for hh in range(tc // GDN_HEAD):
        sl = slice(hh * GDN_HEAD, (hh + 1) * GDN_HEAD)
        yh = y[:, sl]
        nrm = lax.rsqrt(jnp.sum(yh * yh, axis=-1, keepdims=True) + L2_EPS) * qscale
        o_ref[:, sl] = yh * jnp.where(is_qk, nrm, 1.0)


def _gdn_conv(proj, hist, conv_w, *, row0, rows, seq_len, sample, tm):
    tc = 512
    rb0 = row0 // tm
    if sample:
        prev_spec = pl.BlockSpec((tm, tc), lambda i, j: (i, j))
        prev = hist
    else:
        per = tm // SUBLANES
        prev_spec = pl.BlockSpec((SUBLANES, tc), lambda i, j: (jnp.maximum((i + rb0) * per - 1, 0), j))
        prev = proj
    return pl.pallas_call(
        functools.partial(_gdn_conv_kernel, tm=tm, tc=tc, seq_len=seq_len, nq=GDN_KEY_DIM // tc,
                          nqk=2 * GDN_KEY_DIM // tc, sample=sample),
        grid=(rows // tm, GDN_CONV_DIM // tc),
        in_specs=[pl.BlockSpec((tm, tc), lambda i, j: (i + rb0, j)), prev_spec,
                  pl.BlockSpec((GDN_CONV, tc), lambda i, j: (0, j))],
        out_specs=pl.BlockSpec((tm, tc), lambda i, j: (i, j)),
        out_shape=jax.ShapeDtypeStruct((rows, GDN_CONV_DIM), F32),
        compiler_params=_cparams(("parallel", "parallel")),
        name="gdn_conv_sample" if sample else "gdn_conv_prompt",
    )(proj, prev, conv_w)


def _stack_lanes(x, n):
    return jnp.stack([x[:, i * LANES:(i + 1) * LANES] for i in range(n)])


def _stack_cols(x, lo, n):
    return jnp.stack([x[:, lo + i:lo + i + 1] for i in range(n)])


def _over_seqs(x, nseq):
    return jnp.broadcast_to(x[None], (nseq,) + x.shape).reshape((nseq * x.shape[0],) + x.shape[1:])


def _chunk_masks(rows, seq_rows):
    ii = lax.broadcasted_iota(jnp.int32, (rows, rows), 0)
    jj = lax.broadcasted_iota(jnp.int32, (rows, rows), 1)
    shift = int(math.log2(seq_rows))
    same = jnp.right_shift(ii, shift) == jnp.right_shift(jj, shift)
    incl = jnp.logical_and(same, ii >= jj)
    strict = jnp.logical_and(same, ii > jj)
    last = jnp.logical_and(same, jnp.bitwise_and(jj, seq_rows - 1) == seq_rows - 1)
    return ii, jj, incl, strict, last


def _seq_mask(nseq, rows, seq_rows):
    shape = (nseq, 1, rows, 1)
    n = lax.broadcasted_iota(jnp.int32, shape, 0)
    r = lax.broadcasted_iota(jnp.int32, shape, 2)
    return jnp.right_shift(r, int(math.log2(seq_rows))) == n


def _gdn_scan_kernel(*refs, rows, seq_rows, nseq, nchunks, has_s0, hb, nprev):
    q_ref, k_ref, v_ref, z_ref, g_ref, b_ref, on_ref = refs[:7]
    s0_ref = refs[7] if has_s0 else None
    prev_refs = refs[7 + has_s0:7 + has_s0 + nprev]
    o_ref, sout_ref, s_scr = refs[7 + has_s0 + nprev:]
    c = pl.program_id(2)
    nb = nseq * hb

    @pl.when(c == 0)
    def _():
        if has_s0:
            s_scr[...] = s0_ref[...].reshape(nb, GDN_HEAD, GDN_HEAD)
        else:
            s_scr[...] = jnp.zeros_like(s_scr)

    ii, jj, incl, strict, last = _chunk_masks(rows, seq_rows)
    cum_all = _dot_exact_lhs(jnp.where(incl, 1.0, 0.0).astype(BF16), g_ref[...])
    if nseq == 1:
        tot_all = cum_all[rows - 1:rows]
    else:
        tot_all = _dot_exact_lhs(jnp.where(last, 1.0, 0.0).astype(BF16), cum_all)
    cum = _stack_lanes(cum_all, hb)
    e_cum = _stack_lanes(jnp.exp(cum_all), hb)
    e_tail = _stack_lanes(jnp.exp(tot_all - cum_all), hb)
    g_tot = _stack_lanes(jnp.exp(tot_all), hb)
    beta = _stack_lanes(b_ref[...], hb)
    cum_c = cum[:, :, :rows]
    cum_r = jnp.sum(jnp.where(ii == jj, cum_c, 0.0), axis=1, keepdims=True)
    decay = jnp.where(incl, jnp.exp(jnp.where(incl, cum_c - cum_r, 0.0)), 0.0)

    hq = hb // 2
    q2 = _stack_lanes(q_ref[...], hq)
    k2 = _stack_lanes(k_ref[...], hq)
    per_v_head = lambda x: jnp.stack([x[h // 2] for h in range(hb)])
    kk = per_v_head(_mm_nt(k2, k2))
    qk = per_v_head(_mm_nt(q2, k2))
    q = per_v_head(q2)
    k = per_v_head(k2)
    v = _stack_lanes(v_ref[...], hb)
    z = _stack_lanes(z_ref[...], hb)

    a_mat = jnp.where(strict, beta[:, :, :rows] * decay * kk, 0.0)
    t_inv = _unit_lower_inverse(a_mat, ii, jj, seq_rows)
    u_v = _mm(t_inv, beta * v)
    w_k = _mm(t_inv, (beta * e_cum) * k)
    p_qk = decay * qk
    q_g = q * e_cum
    k_d_t = jnp.swapaxes(k * e_tail, 1, 2)
    s = s_scr[...]
    sb = s.astype(BF16)
    if nseq == 1:
        u = u_v - _mm(w_k, sb)
        o = _mm(q_g, sb) + _mm(p_qk, u)
        s_scr[...] = g_tot * s + _mm(k_d_t, u)
    else:
        mine = _seq_mask(nseq, rows, seq_rows)
        pick = lambda x: jnp.sum(jnp.where(mine, x.reshape((nseq, hb) + x.shape[1:]), 0.0), axis=0)
        u = u_v - pick(_mm(_over_seqs(w_k, nseq), sb))
        o = pick(_mm(_over_seqs(q_g, nseq), sb)) + _mm(p_qk, u)
        u_n = jnp.where(mine, u[None], 0.0).reshape(nb, rows, GDN_HEAD)
        gt = jnp.stack([g_tot[:, (n + 1) * seq_rows - 1:(n + 1) * seq_rows] for n in range(nseq)])
        s_scr[...] = gt.reshape(nb, 1, GDN_HEAD) * s + _mm(_over_seqs(k_d_t, nseq), u_n)
    o = _rms(o, on_ref[...]) * (z * _sigmoid(z))
    for h in range(hb):
        o_ref[:, h * GDN_HEAD:(h + 1) * GDN_HEAD] = o[h]

    @pl.when(c == nchunks - 1)
    def _():
        new = s_scr[...].reshape(nseq, hb, GDN_HEAD, GDN_HEAD)
        if nprev:
            for l, p_ref in enumerate(prev_refs):
                sout_ref[:, l] = p_ref[...]
            sout_ref[:, nprev] = new
        else:
            sout_ref[...] = new


def _gdn_scan(act, proj, gates, o_norm, s0, layer, *, batch, seq_len, row0, sample, prev_states=()):
    hb = GDN_HB_SAMPLE if sample else GDN_HB_PROMPT
    ng = GDN_V_HEADS // hb
    hgw = hb * GDN_HEAD
    qkw = hgw // 2
    if sample:
        nseq, seq_rows, nchunks = SEQ_GROUP, seq_len, 1
        rows = nseq * seq_rows
        grid = (batch // nseq, ng, 1)
    else:
        nseq, seq_rows, rows = 1, GDN_CHUNK, GDN_CHUNK
        nchunks = seq_len // rows
        grid = (batch, ng, nchunks)
    rb0 = row0 // rows
    rmap = lambda b, g, c: b * nchunks + c
    in_specs = [
        pl.BlockSpec((rows, qkw), lambda b, g, c: (rmap(b, g, c), g)),
        pl.BlockSpec((rows, qkw), lambda b, g, c: (rmap(b, g, c), GDN_KEY_DIM // qkw + g)),
        pl.BlockSpec((rows, hgw), lambda b, g, c: (rmap(b, g, c), 2 * GDN_KEY_DIM // hgw + g)),
        pl.BlockSpec((rows, hgw), lambda b, g, c: (rmap(b, g, c) + rb0, GDN_CONV_DIM // hgw + g)),
        pl.BlockSpec((rows, hgw), lambda b, g, c: (rmap(b, g, c) + rb0, g)),
        pl.BlockSpec((rows, hgw), lambda b, g, c: (rmap(b, g, c) + rb0, ng + g)),
        pl.BlockSpec((1, GDN_HEAD), lambda b, g, c: (0, 0)),
    ]
    args = [act, act, act, proj, gates, gates, o_norm.reshape(1, GDN_HEAD)]
    if sample:
        in_specs.append(pl.BlockSpec((nseq, None, hb, GDN_HEAD, GDN_HEAD),
                                     lambda b, g, c: (b, layer, g, 0, 0)))
        args.append(s0)
    state_spec = pl.BlockSpec((nseq, hb, GDN_HEAD, GDN_HEAD), lambda b, g, c: (b, g, 0, 0))
    state_shape = jax.ShapeDtypeStruct((batch, GDN_V_HEADS, GDN_HEAD, GDN_HEAD), F32)
    nprev = len(prev_states)
    if nprev:
        in_specs += [state_spec] * nprev
        args += list(prev_states)
        state_spec = pl.BlockSpec((nseq, nprev + 1, hb, GDN_HEAD, GDN_HEAD), lambda b, g, c: (b, 0, g, 0, 0))
        state_shape = jax.ShapeDtypeStruct((batch, nprev + 1, GDN_V_HEADS, GDN_HEAD, GDN_HEAD), F32)
    return pl.pallas_call(
        functools.partial(_gdn_scan_kernel, rows=rows, seq_rows=seq_rows, nseq=nseq, nchunks=nchunks,
                          has_s0=sample, hb=hb, nprev=nprev),
        grid=grid,
        in_specs=in_specs,
        out_specs=[pl.BlockSpec((rows, hgw), lambda b, g, c: (rmap(b, g, c), g)), state_spec],
        out_shape=[jax.ShapeDtypeStruct((batch * seq_len, GDN_VAL_DIM), F32), state_shape],
        scratch_shapes=[pltpu.VMEM((nseq * hb, GDN_HEAD, GDN_HEAD), F32)],
        compiler_params=_cparams(("parallel", "parallel", "arbitrary")),
        name="gdn_scan_sample" if sample else "gdn_scan_prompt",
    )(*args)


def _mla_post_kernel(ckv_in_ref, kpe_in_ref, kpe_sw_ref, qpe_ref, qsw_ref, cos_ref, sin_ref, g_ref,
                     ckv_ref, kpe_ref, qp_ref):
    ckv_ref[...] = _rms(ckv_in_ref[...], g_ref[...])
    cos = cos_ref[...]
    sin = sin_ref[...]
    kpe_ref[...] = kpe_in_ref[...] * cos + kpe_sw_ref[...] * sin
    for h in range(MLA_HEADS):
        sl = slice(h * LANES, (h + 1) * LANES)
        qp_ref[:, sl] = qpe_ref[:, sl] * cos + qsw_ref[:, sl] * sin


def _mla_post(c, q_raw, cos2, sin2, kv_norm, *, tm):
    t = c.shape[0]
    hw = MLA_HEADS * LANES
    c0 = MLA_Q_RANK // MLA_KV_RANK
    k0 = (MLA_Q_RANK + MLA_KV_RANK) // LANES
    row = lambda w, col: pl.BlockSpec((tm, w), lambda i: (i, col))
    return pl.pallas_call(
        _mla_post_kernel,
        grid=(t // tm,),
        in_specs=[row(MLA_KV_RANK, c0), row(LANES, k0), row(LANES, k0 + 1),
                  row(hw, 1), row(hw, 2), row(LANES, 0), row(LANES, 0),
                  pl.BlockSpec((1, MLA_KV_RANK), lambda i: (0, 0))],
        out_specs=[row(MLA_KV_RANK, 0), row(LANES, 0), row(hw, 0)],
        out_shape=[jax.ShapeDtypeStruct((t, MLA_KV_RANK), F32), jax.ShapeDtypeStruct((t, LANES), F32),
                   jax.ShapeDtypeStruct((t, hw), F32)],
        compiler_params=_cparams(("parallel",)),
        name="mla_post",
    )(c, c, c, q_raw, q_raw, cos2, sin2, kv_norm.reshape(1, MLA_KV_RANK))


def _flash_kernel(qi_ref, ki_ref, qn_ref, qp_ref, kn_ref, kp_ref, v_ref, o_ref, qs_ref, m_ref, l_ref, acc_ref,
                  *, scale):
    t = pl.program_id(2)
    qi = qi_ref[t]
    ki = ki_ref[t]

    @pl.when(ki == 0)
    def _():
        qs_ref[:, :LANES] = (qn_ref[...] * scale).astype(BF16)
        qs_ref[:, LANES:] = (qp_ref[...] * scale).astype(BF16)
        m_ref[...] = jnp.full_like(m_ref, -jnp.inf)
        l_ref[...] = jnp.zeros_like(l_ref)
        acc_ref[...] = jnp.zeros_like(acc_ref)

    def block(diagonal):
        s = _dot_nt(qs_ref[:, :LANES], kn_ref[...]) + _dot_nt(qs_ref[:, LANES:], kp_ref[...])
        if diagonal:
            row = lax.broadcasted_iota(jnp.int32, s.shape, 0)
            col = lax.broadcasted_iota(jnp.int32, s.shape, 1)
            s = jnp.where(row >= col, s, -jnp.inf)
        m_new = jnp.maximum(m_ref[...], jnp.max(s, axis=-1, keepdims=True))
        alpha = jnp.exp(m_ref[...] - m_new)
        p = jnp.exp(s - m_new)
        l_ref[...] = alpha * l_ref[...] + jnp.sum(p, axis=-1, keepdims=True)
        acc_ref[...] = alpha * acc_ref[...] + _dot(p, v_ref[...])
        m_ref[...] = m_new

    @pl.when(ki < qi)
    def _():
        block(False)

    @pl.when(ki == qi)
    def _():
        block(True)
        o_ref[...] = acc_ref[...] / l_ref[...]


def _mla_flash(q_raw, qp, kv, kpe, *, batch, seq_len):
    tq = next(t for t in (1024, 512, 256, 128, 64) if t <= MLA_FLASH_TILE and seq_len % t == 0)
    nq = seq_len // tq
    pairs = [(qi, ki) for qi in range(nq) for ki in range(qi + 1)]
    qi_of = jnp.asarray([p[0] for p in pairs], jnp.int32)
    ki_of = jnp.asarray([p[1] for p in pairs], jnp.int32)
    qmap = lambda b, h, t, qi, ki: (b * nq + qi[t], h)
    kmap = lambda off: (lambda b, h, t, qi, ki: (b * nq + ki[t], off + h))
    grid_spec = pltpu.PrefetchScalarGridSpec(
        num_scalar_prefetch=2,
        grid=(batch, MLA_HEADS, len(pairs)),
        in_specs=[pl.BlockSpec((tq, LANES), qmap), pl.BlockSpec((tq, LANES), qmap),
                  pl.BlockSpec((tq, LANES), kmap(0)),
                  pl.BlockSpec((tq, LANES), lambda b, h, t, qi, ki: (b * nq + ki[t], 0)),
                  pl.BlockSpec((tq, LANES), kmap(MLA_HEADS))],
        out_specs=pl.BlockSpec((tq, LANES), qmap),
        scratch_shapes=[pltpu.VMEM((tq, 2 * LANES), BF16), pltpu.VMEM((tq, 1), F32),
                        pltpu.VMEM((tq, 1), F32), pltpu.VMEM((tq, MLA_V), F32)],
    )
    return pl.pallas_call(
        functools.partial(_flash_kernel, scale=MLA_QK ** -0.5),
        grid_spec=grid_spec,
        out_shape=jax.ShapeDtypeStruct((batch * seq_len, MLA_HEADS * MLA_V), F32),
        compiler_params=_cparams(("parallel", "parallel", "arbitrary")),
        name="mla_flash_prompt",
    )(qi_of, ki_of, q_raw, qp, kv, kpe, kv)


def _decode_kernel(*refs, npg, ngroups, new_len, scale):
    pt_ref = refs[0]
    ql_ref, qp_ref = refs[1], refs[2]
    ckv_refs = refs[3:3 + npg]
    kpe_refs = refs[3 + npg:3 + 2 * npg]
    cnew_ref, knew_ref = refs[3 + 2 * npg], refs[4 + 2 * npg]
    o_ref = refs[5 + 2 * npg]
    qls_ref, qps_ref, m_ref, l_ref, acc_ref, kc_ref, pc_ref = refs[6 + 2 * npg:]
    del pt_ref
    g = pl.program_id(1)

    @pl.when(g == 0)
    def _():
        qls_ref[...] = (ql_ref[...] * scale).astype(BF16)
        qps_ref[...] = (qp_ref[:, :MLA_ROPE] * scale).astype(BF16)
        m_ref[...] = jnp.full_like(m_ref, -jnp.inf)
        l_ref[...] = jnp.zeros_like(l_ref)
        acc_ref[...] = jnp.zeros_like(acc_ref)

    def update(s, keys):
        m_new = jnp.maximum(m_ref[...], jnp.max(s, axis=-1, keepdims=True))
        alpha = jnp.exp(m_ref[...] - m_new)
        p = jnp.exp(s - m_new)
        l_ref[...] = alpha * l_ref[...] + jnp.sum(p, axis=-1, keepdims=True)
        acc_ref[...] = alpha * acc_ref[...] + jnp.dot(p.astype(BF16), keys, preferred_element_type=F32)
        m_ref[...] = m_new

    for i in range(npg):
        sl = slice(i * PAGE_SIZE, (i + 1) * PAGE_SIZE)
        kc_ref[sl, :] = ckv_refs[i][...].astype(BF16)
        pc_ref[:, sl] = kpe_refs[i][...].astype(BF16)
    keys = kc_ref[...]
    update(_dot_nt(qls_ref[...], keys) + _dot(qps_ref[...], pc_ref[...]), keys)

    @pl.when(g == ngroups - 1)
    def _():
        kb = cnew_ref[...].astype(BF16)
        s = _dot_nt(qls_ref[...], kb) + _dot(qps_ref[...], knew_ref[...])
        tq = jnp.right_shift(lax.broadcasted_iota(jnp.int32, s.shape, 0), int(math.log2(MLA_HEADS)))
        tk = lax.broadcasted_iota(jnp.int32, s.shape, 1)
        update(jnp.where(jnp.logical_and(tk < new_len, tk <= tq), s, -jnp.inf), kb)
        o_ref[...] = acc_ref[...] / l_ref[...]


def _mla_decode(q_lat, qp, cache_ckv, cache_kpe, page_table, ckv_new, kpe_new, layer, *, new_len):
    bsz, n_pages = page_table.shape
    nrow = q_lat.shape[1]
    npg = min(DEC_PAGES_PER_STEP, n_pages)
    ngroups = n_pages // npg
    qp_rb0 = qp.shape[0] // nrow - bsz
    page = lambda i, shape: pl.BlockSpec((None, None) + shape,
                                         lambda b, g, pt: (pt[b, g * npg + i], layer, 0, 0))
    in_specs = [pl.BlockSpec((None, nrow, MLA_KV_RANK), lambda b, g, pt: (b, 0, 0)),
                pl.BlockSpec((nrow, LANES), lambda b, g, pt: (qp_rb0 + b, 0))]
    in_specs += [page(i, (PAGE_SIZE, MLA_KV_RANK)) for i in range(npg)]
    in_specs += [page(i, (MLA_ROPE, PAGE_SIZE)) for i in range(npg)]
    in_specs += [pl.BlockSpec((None, PAGE_SIZE, MLA_KV_RANK), lambda b, g, pt: (b, 0, 0)),
                 pl.BlockSpec((None, MLA_ROPE, PAGE_SIZE), lambda b, g, pt: (b, 0, 0))]
    grid_spec = pltpu.PrefetchScalarGridSpec(
        num_scalar_prefetch=1,
        grid=(bsz, ngroups),
        in_specs=in_specs,
        out_specs=pl.BlockSpec((None, nrow, MLA_KV_RANK), lambda b, g, pt: (b, 0, 0)),
        scratch_shapes=[pltpu.VMEM((nrow, MLA_KV_RANK), BF16), pltpu.VMEM((nrow, MLA_ROPE), BF16),
                        pltpu.VMEM((nrow, 1), F32), pltpu.VMEM((nrow, 1), F32),
                        pltpu.VMEM((nrow, MLA_KV_RANK), F32),
                        pltpu.VMEM((npg * PAGE_SIZE, MLA_KV_RANK), BF16),
                        pltpu.VMEM((MLA_ROPE, npg * PAGE_SIZE), BF16)],
    )
    return pl.pallas_call(
        functools.partial(_decode_kernel, npg=npg, ngroups=ngroups, new_len=new_len, scale=MLA_QK ** -0.5),
        grid_spec=grid_spec,
        out_shape=jax.ShapeDtypeStruct(q_lat.shape, F32),
        compiler_params=_cparams(("parallel", "arbitrary")),
        name="mla_decode",
    )(page_table, q_lat, qp, *([cache_ckv] * npg), *([cache_kpe] * npg), ckv_new, kpe_new)


def _lora_kernel(x_ref, p_ref, mu_ref, w1_ref, w2_ref, b_ref, o_ref, *, kind):
    x = x_ref[...]
    xm = x + (p_ref[...] - x) * mu_ref[...]
    t = jnp.dot(xm.astype(BF16), w1_ref[...], preferred_element_type=F32)
    if kind == "decay":
        t = jnp.tanh(t)
    elif kind == "gate":
        t = _sigmoid(t)
    y = jnp.dot(t.astype(BF16), w2_ref[...], preferred_element_type=F32)
    if kind == "decay":
        o_ref[...] = -_softplus(-(b_ref[...] + y)) - 0.5
    elif kind == "aaa":
        o_ref[...] = _sigmoid(b_ref[...] + y)
    else:
        o_ref[...] = y


def _lora(h, prev, mu, w1, w2, bias, *, kind, tm):
    t, d = h.shape
    r = w1.shape[1]
    row = pl.BlockSpec((tm, d), lambda i: (i, 0))
    vec = pl.BlockSpec((1, d), lambda i: (0, 0))
    return pl.pallas_call(
        functools.partial(_lora_kernel, kind=kind),
        grid=(t // tm,),
        in_specs=[row, row, vec, pl.BlockSpec((d, r), lambda i: (0, 0)),
                  pl.BlockSpec((r, d), lambda i: (0, 0)), vec],
        out_specs=row,
        out_shape=jax.ShapeDtypeStruct((t, d), F32),
        compiler_params=_cparams(("parallel",)),
        name=f"rwkv_lora_{kind}",
    )(h, prev, mu.reshape(1, d), w1, w2, bias.reshape(1, d))


def _rwkv_scan_kernel(*refs, rows, seq_rows, nseq, nchunks, has_h0, pb):
    r_ref, k_ref, v_ref, w_ref, a_ref, g_ref, kk_ref, ka_ref, rk_ref, lw_ref, lb_ref = refs[:11]
    if has_h0:
        h0_ref, y_ref, hout_ref, h_scr = refs[11:]
    else:
        y_ref, hout_ref, h_scr = refs[11:]
    c = pl.program_id(2)
    nb = nseq * pb

    @pl.when(c == 0)
    def _():
        if has_h0:
            h_scr[...] = h0_ref[...].reshape(nb, LANES, LANES)
        else:
            h_scr[...] = jnp.zeros_like(h_scr)

    ii, jj, incl, strict, last = _chunk_masks(rows, seq_rows)
    head0 = lax.broadcasted_iota(jnp.int32, (1, 1, LANES), 2) < RWKV_HEAD

    def head_sum(x):
        s0 = jnp.sum(jnp.where(head0, x, 0.0), axis=-1, keepdims=True)
        s1 = jnp.sum(jnp.where(head0, 0.0, x), axis=-1, keepdims=True)
        return jnp.where(head0, s0, s1)

    log_d2 = -jnp.exp(w_ref[...])
    cum2 = _dot_exact_lhs(jnp.where(incl, 1.0, 0.0).astype(BF16), log_d2)
    if nseq == 1:
        tot2 = cum2[rows - 1:rows]
    else:
        tot2 = _dot_exact_lhs(jnp.where(last, 1.0, 0.0).astype(BF16), cum2)
    stack = lambda x: _stack_lanes(x, pb)
    e_pos = stack(jnp.exp(cum2))
    e_neg = stack(jnp.exp(-cum2))
    e_prev = stack(jnp.exp(cum2 - log_d2))
    e_end = stack(jnp.exp(tot2 - cum2))
    gamma = stack(jnp.exp(tot2))

    r = stack(r_ref[...])
    k = stack(k_ref[...])
    v = stack(v_ref[...])
    a = stack(a_ref[...])
    kx = k * stack(kk_ref[...])
    kk = kx * lax.rsqrt(head_sum(kx * kx) + L2_EPS)
    k = k * (1.0 + (a - 1.0) * stack(ka_ref[...]))
    b = kk * a
    a_t = -kk * e_prev
    b_t = b * e_neg
    k_t = k * e_neg
    r_t = r * e_pos
    bd_t = jnp.swapaxes(b * e_end, 1, 2)
    kd_t = jnp.swapaxes(k * e_end, 1, 2)

    i2 = lax.broadcasted_iota(jnp.int32, (LANES, LANES), 0)
    j2 = lax.broadcasted_iota(jnp.int32, (LANES, LANES), 1)
    eye2 = i2 == j2
    block = (i2 < RWKV_HEAD) == (j2 < RWKV_HEAD)

    halves = lambda x: jnp.concatenate([jnp.where(head0, x, 0.0), jnp.where(head0, 0.0, x)], axis=0)
    twice = lambda x: jnp.concatenate([x, x], axis=0)
    join = lambda x2: jnp.where(head0, x2[:pb], x2[pb:])
    a_m = halves(a_t)
    r_m = halves(r_t)
    b2, k2, v2 = twice(b_t), twice(k_t), twice(v)
    a_ab = jnp.where(strict, _mm_nt(a_m, b2), 0.0)
    a_ak = jnp.where(strict, _mm_nt(a_m, k2), 0.0)
    p_rb = jnp.where(incl, _mm_nt(r_m, b2), 0.0)
    p_rk = jnp.where(incl, _mm_nt(r_m, k2), 0.0)
    t_inv = _unit_lower_inverse(-a_ab, ii, jj, seq_rows)
    akv = _mm(a_ak, v2)
    prkv = _mm(p_rk, v2)

    hs = h_scr[...]
    hbf = hs.astype(BF16)
    if nseq == 1:
        h2 = twice(hbf)
        u2 = _mm(t_inv, _mm(a_m, h2) + akv)
        y2 = _mm(r_m, h2) + _mm(p_rb, u2) + prkv
        u = join(u2)
        y = join(y2)
        g_col = jnp.sum(jnp.where(eye2, gamma, 0.0), axis=2, keepdims=True)
        h_scr[...] = g_col * hs + jnp.where(block, _mm(bd_t, u) + _mm(kd_t, v), 0.0)
    else:
        mine = _seq_mask(nseq, rows, seq_rows)
        h4 = hbf.reshape(nseq, pb, LANES, LANES)
        h2 = jnp.concatenate([h4, h4], axis=1).reshape(nseq * 2 * pb, LANES, LANES)
        pick = lambda x: jnp.sum(jnp.where(mine, x.reshape((nseq, 2 * pb) + x.shape[1:]), 0.0), axis=0)
        u2 = _mm(t_inv, pick(_mm(_over_seqs(a_m, nseq), h2)) + akv)
        y2 = pick(_mm(_over_seqs(r_m, nseq), h2)) + _mm(p_rb, u2) + prkv
        u = join(u2)
        y = join(y2)
        u_n = jnp.where(mine, u[None], 0.0).reshape(nb, rows, LANES)
        v_n = jnp.where(mine, v[None], 0.0).reshape(nb, rows, LANES)
        g_rows = jnp.stack([gamma[:, (n + 1) * seq_rows - 1:(n + 1) * seq_rows] for n in range(nseq)])
        g_col = jnp.sum(jnp.where(eye2, g_rows.reshape(nb, 1, LANES), 0.0), axis=2, keepdims=True)
        h_scr[...] = g_col * hs + jnp.where(
            block, _mm(_over_seqs(bd_t, nseq), u_n) + _mm(_over_seqs(kd_t, nseq), v_n), 0.0)

    mean = head_sum(y) * (1.0 / RWKV_HEAD)
    d = y - mean
    var = head_sum(d * d) * (1.0 / RWKV_HEAD)
    yn = d * lax.rsqrt(var + RWKV_GN_EPS) * stack(lw_ref[...]) + stack(lb_ref[...])
    bonus = head_sum(r * k * stack(rk_ref[...])) * v
    out = (yn + bonus) * stack(g_ref[...])
    for p in range(pb):
        y_ref[:, p * LANES:(p + 1) * LANES] = out[p]

    @pl.when(c == nchunks - 1)
    def _():
        hout_ref[...] = h_scr[...].reshape(hout_ref.shape)


def _rwkv_scan(r, k, v, w, a, g, k_k, k_a, r_k, ln_w, ln_b, h0, *, batch, seq_len, row0, sample):
    d = r.shape[1]
    npairs = d // LANES
    pb = min(RWKV_PB_SAMPLE if sample else RWKV_PB_PROMPT, npairs)
    if sample:
        nseq, seq_rows, nchunks = SEQ_GROUP, seq_len, 1
        rows = nseq * seq_rows
        grid = (batch // nseq, npairs // pb, 1)
    else:
        nseq, seq_rows, rows = 1, RWKV_CHUNK, RWKV_CHUNK
        nchunks = seq_len // rows
        grid = (batch, npairs // pb, nchunks)
    rb0 = row0 // rows
    tile = pl.BlockSpec((rows, pb * LANES), lambda b, p, c: (b * nchunks + c + rb0, p))
    vec = pl.BlockSpec((1, pb * LANES), lambda b, p, c: (0, p))
    state = pl.BlockSpec((nseq, pb, LANES, LANES), lambda b, p, c: (b, p, 0, 0))
    in_specs = [tile] * 6 + [vec] * 5
    args = [r, k, v, w, a, g] + [x.reshape(1, d) for x in (k_k, k_a, r_k, ln_w, ln_b)]
    if sample:
        in_specs.append(state)
        args.append(h0)
    return pl.pallas_call(
        functools.partial(_rwkv_scan_kernel, rows=rows, seq_rows=seq_rows, nseq=nseq, nchunks=nchunks,
                          has_h0=sample, pb=pb),
        grid=grid,
        in_specs=in_specs,
        out_specs=[pl.BlockSpec((rows, pb * LANES), lambda b, p, c: (b * nchunks + c, p)), state],
        out_shape=[jax.ShapeDtypeStruct((batch * seq_len, d), F32),
                   jax.ShapeDtypeStruct((batch, npairs, LANES, LANES), F32)],
        scratch_shapes=[pltpu.VMEM((nseq * pb, LANES, LANES), F32)],
        compiler_params=_cparams(("parallel", "parallel", "arbitrary")),
        name="rwkv_scan_sample" if sample else "rwkv_scan_prompt",
    )(*args)


def _pairs_from_heads(s):
    b, h, n, _ = s.shape
    st = jnp.swapaxes(s, -1, -2).reshape(b, h // 2, 2, n, n)
    bd = jnp.einsum("bpeij,ef->bpeifj", st, jnp.eye(2, dtype=s.dtype))
    return bd.reshape(b, h // 2, 2 * n, 2 * n)


def _heads_from_pairs(hp):
    b, p, n2, _ = hp.shape
    n = n2 // 2
    hr = hp.reshape(b, p, 2, n, 2, n)
    st = jnp.stack([hr[:, :, 0, :, 0, :], hr[:, :, 1, :, 1, :]], axis=2).reshape(b, 2 * p, n, n)
    return jnp.swapaxes(st, -1, -2)


def _pad_cols(w, n):
    return jnp.pad(w, ((0, 0), (0, n - w.shape[1])))


def kernel(x_prompt, x_sample, cache_mla_ckv, cache_mla_kpe, page_table, state_gdn_s, state_gdn_conv, state_rwkv_wkv, state_rwkv_shift, norm_w, gdn_w_in, gdn_conv_w, gdn_a_log, gdn_dt_bias, gdn_o_norm, gdn_w_out, mla_w_in, mla_q_norm, mla_w_uq, mla_kv_norm, mla_w_uk, mla_w_uv, mla_w_o, rw_mu, rw_w_rkv, rw_w0, rw_w1, rw_w2, rw_a0, rw_a1, rw_a2, rw_g1, rw_g2, rw_k_k, rw_k_a, rw_r_k, rw_ln_w, rw_ln_b, rw_w_o, mlp_w_up, mlp_w_down):
    bp, lp, d = x_prompt.shape
    bs, ls, _ = x_sample.shape
    tp, ts = bp * lp, bs * ls
    depth = norm_w.shape[0]
    n_pages = page_table.shape[1]
    past_len = n_pages * PAGE_SIZE
    tm = _row_tile(tp, ts)
    assert ls == SUBLANES and bs % SEQ_GROUP == 0 and lp % GDN_CHUNK == 0 and lp % RWKV_CHUNK == 0

    x = jnp.concatenate([x_prompt.reshape(tp, d), x_sample.reshape(ts, d)], axis=0)

    half = MLA_ROPE // 2
    inv_freq = 1.0 / (ROPE_THETA ** (jnp.arange(half, dtype=F32) / half))
    pos = jnp.concatenate([jnp.tile(jnp.arange(lp), bp), jnp.tile(past_len + jnp.arange(ls), bs)])
    ang = pos.astype(F32)[:, None] * inv_freq[None, :]
    cos, sin = jnp.cos(ang), jnp.sin(ang)
    zpad = jnp.zeros((tp + ts, LANES - MLA_ROPE), F32)
    cos2 = jnp.concatenate([cos, cos, zpad], axis=1)
    sin2 = jnp.concatenate([-sin, sin, zpad], axis=1)

    outs = {k: [] for k in ("ckv_p", "kpe_p", "ckv_s", "kpe_s", "gs_p", "gc_p", "gs_s", "gc_s",
                            "rs_p", "rx_p", "rs_s", "rx_s")}
    for i in range(depth):
        kind, j = i % 3, i // 3
        if kind == 0:
            w_in = gdn_w_in[j]
            w_main = w_in[:, :GDN_CONV_DIM + GDN_VAL_DIM].astype(BF16)
            nb = GDN_CONV_DIM + GDN_VAL_DIM
            w_ab = jnp.concatenate([w_in[:, nb + GDN_V_HEADS:], w_in[:, nb:nb + GDN_V_HEADS]], axis=1).astype(BF16)
            proj = _linear(x, w_main, mode="norm", g=norm_w[i, 0], tm=tm)
            ab = _linear(x, w_ab, mode="norm", g=norm_w[i, 0], tm=tm)
            gates = _gdn_gates(ab, gdn_a_log[j], gdn_dt_bias[j], tm=tm)
            hist = jnp.pad(state_gdn_conv[:, j], ((0, 0), (SUBLANES - GDN_CONV + 1, 0), (0, 0)))
            hist = hist.reshape(ts, GDN_CONV_DIM)
            act_p = _gdn_conv(proj, None, gdn_conv_w[j], row0=0, rows=tp, seq_len=lp, sample=False, tm=tm)
            act_s = _gdn_conv(proj, hist, gdn_conv_w[j], row0=tp, rows=ts, seq_len=ls, sample=True, tm=tm)
            y_p, s_p = _gdn_scan(act_p, proj, gates, gdn_o_norm[j], None, j, batch=bp, seq_len=lp,
                                 row0=0, sample=False)
            earlier = tuple(outs["gs_s"]) if j == gdn_w_in.shape[0] - 1 else ()
            y_s, s_s = _gdn_scan(act_s, proj, gates, gdn_o_norm[j], state_gdn_s, j, batch=bs, seq_len=ls,
                                 row0=tp, sample=True, prev_states=earlier)
            if earlier:
                outs["gs_s"] = []
            w_out = gdn_w_out[j].astype(BF16)
            tail = GDN_CONV - 1
            conv_p = jnp.stack([lax.slice(proj, ((b + 1) * lp - tail, 0), ((b + 1) * lp, GDN_CONV_DIM))
                                for b in range(bp)])
            conv_s = lax.slice(proj, (tp, 0), (tp + ts, GDN_CONV_DIM)).reshape(bs, ls, GDN_CONV_DIM)[:, ls - tail:]
            outs["gs_p"].append(s_p)
            outs["gc_p"].append(conv_p)
            outs["gs_s"].append(s_s)
            outs["gc_s"].append(conv_s)
        elif kind == 1:
            w_in = mla_w_in[j]
            w_kpe = w_in[:, MLA_Q_RANK + MLA_KV_RANK:]
            w_c = jnp.concatenate([w_in[:, :MLA_Q_RANK + MLA_KV_RANK], _pad_cols(w_kpe, LANES),
                                   _pad_cols(jnp.roll(w_kpe, half, axis=1), LANES)], axis=1).astype(BF16)
            uq = mla_w_uq[j].reshape(MLA_Q_RANK, MLA_HEADS, MLA_QK)
            uq_pe = uq[:, :, MLA_NOPE:]
            padh = ((0, 0), (0, 0), (0, LANES - MLA_ROPE))
            w_q = jnp.concatenate([
                uq[:, :, :MLA_NOPE].reshape(MLA_Q_RANK, -1),
                jnp.pad(uq_pe, padh).reshape(MLA_Q_RANK, -1),
                jnp.pad(jnp.roll(uq_pe, half, axis=2), padh).reshape(MLA_Q_RANK, -1)], axis=1).astype(BF16)
            c = _linear(x, w_c, mode="norm", g=norm_w[i, 0], tm=tm)
            q_raw = _linear(c, w_q, mode="norm", g=mla_q_norm[j], tm=tm)
            ckv, kpe, qp = _mla_post(c, q_raw, cos2, sin2, mla_kv_norm[j], tm=min(tm, 256))
            w_kv = jnp.concatenate([mla_w_uk[j].reshape(MLA_KV_RANK, -1),
                                    mla_w_uv[j].reshape(MLA_KV_RANK, -1)], axis=1).astype(BF16)
            kv = _linear(ckv, w_kv, rows=tp, tm=tm)
            y_p = _mla_flash(q_raw, qp, kv, kpe, batch=bp, seq_len=lp)
            w_ukt = jnp.transpose(mla_w_uk[j], (1, 2, 0)).astype(BF16)
            w_uvh = jnp.transpose(mla_w_uv[j], (1, 0, 2)).astype(BF16)
            q_lat = _head_linear(q_raw, w_ukt, row0=tp, rows=ts, tm=tm)
            pad_keys = ((0, 0), (0, PAGE_SIZE - ls), (0, 0))
            ckv_new = jnp.pad(ckv[tp:].reshape(bs, ls, MLA_KV_RANK), pad_keys)
            kpe_new = jnp.pad(kpe[tp:, :MLA_ROPE].reshape(bs, ls, MLA_ROPE), pad_keys)
            o_lat = _mla_decode(q_lat.reshape(bs, ls * MLA_HEADS, MLA_KV_RANK),
                                qp.reshape((tp + ts) * MLA_HEADS, LANES), cache_mla_ckv,
                                jnp.swapaxes(cache_mla_kpe, 2, 3), page_table, ckv_new,
                                jnp.swapaxes(kpe_new, 1, 2), j, new_len=ls)
            y_s = _head_linear(o_lat.reshape(ts, MLA_HEADS * MLA_KV_RANK), w_uvh, row0=0, rows=ts, tm=tm)
            w_out = mla_w_o[j].astype(BF16)
            outs["ckv_p"].append(ckv[:tp].reshape(bp, lp, MLA_KV_RANK))
            outs["kpe_p"].append(kpe[:tp, :MLA_ROPE].reshape(bp, lp, MLA_ROPE))
            outs["ckv_s"].append(ckv[tp:].reshape(bs, ls, MLA_KV_RANK))
            outs["kpe_s"].append(kpe[tp:, :MLA_ROPE].reshape(bs, ls, MLA_ROPE))
        else:
            shift_rows = jnp.pad(state_rwkv_shift[:, j][:, None, :], ((0, 0), (0, ls - 1), (0, 0)))
            h, prev = _norm_shift(x, norm_w[i, 0], shift_rows.reshape(ts, d), tp=tp, seq_len=lp, tm=tm)
            mu = rw_mu[j]
            wr, wk, wv = (rw_w_rkv[j, s].astype(BF16) for s in range(3))
            r = _linear(h, wr, mode="mix", prev=prev, mu=mu[0], tm=tm)
            k = _linear(h, wk, mode="mix", prev=prev, mu=mu[1], tm=tm)
            v = _linear(h, wv, mode="mix", prev=prev, mu=mu[2], tm=tm)
            rank = lambda n: -(-n // LANES) * LANES
            lora_w = lambda w1, w2: (_pad_cols(w1, rank(w1.shape[1])).astype(BF16),
                                     jnp.pad(w2, ((0, rank(w2.shape[0]) - w2.shape[0]), (0, 0))).astype(BF16))
            w = _lora(h, prev, mu[3], *lora_w(rw_w1[j], rw_w2[j]), rw_w0[j], kind="decay", tm=tm)
            a = _lora(h, prev, mu[4], *lora_w(rw_a1[j], rw_a2[j]), rw_a0[j], kind="aaa", tm=tm)
            g = _lora(h, prev, mu[5], *lora_w(rw_g1[j], rw_g2[j]), jnp.zeros((d,), F32), kind="gate", tm=tm)
            params = (rw_k_k[j], rw_k_a[j], rw_r_k[j].reshape(d), rw_ln_w[j], rw_ln_b[j])
            y_p, hp_p = _rwkv_scan(r, k, v, w, a, g, *params, None, batch=bp, seq_len=lp, row0=0, sample=False)
            y_s, hp_s = _rwkv_scan(r, k, v, w, a, g, *params, _pairs_from_heads(state_rwkv_wkv[:, j]),
                                   batch=bs, seq_len=ls, row0=tp, sample=True)
            w_out = rw_w_o[j].astype(BF16)
            outs["rs_p"].append(_heads_from_pairs(hp_p))
            outs["rx_p"].append(jnp.concatenate([lax.slice(h, ((b + 1) * lp - 1, 0), ((b + 1) * lp, d))
                                                 for b in range(bp)]))
            outs["rs_s"].append(_heads_from_pairs(hp_s))
            outs["rx_s"].append(lax.slice(h, (tp, 0), (tp + ts, d)).reshape(bs, ls, d)[:, -1])
        x = _out_proj(y_p, y_s, w_out, x, norm_w[i, 1], tm=tm)
        x = _mlp(x, norm_w[i, 2], mlp_w_up[i].astype(BF16), mlp_w_down[i].astype(BF16), norm_w[i, 3], tm=tm)

    def stack(key):
        rows = outs[key]
        if key == "gs_s" and rows[0].ndim == state_gdn_s.ndim:
            return rows[0]
        return jnp.stack(rows, axis=1)

    return (x[:tp].reshape(bp, lp, d), x[tp:].reshape(bs, ls, d),
            stack("ckv_p"), stack("kpe_p"), stack("gs_p"), stack("gc_p"), stack("rs_p"), stack("rx_p"),
            stack("ckv_s"), stack("kpe_s"), stack("gs_s"), stack("gc_s"), stack("rs_s"), stack("rx_s"))
```

```python
import functools
import math

import jax
import jax.numpy as jnp
from jax import lax
from jax.experimental import pallas as pl
from jax.experimental.pallas import tpu as pltpu

F32 = jnp.float32
BF16 = jnp.bfloat16

NORM_EPS = 1e-6
L2_EPS = 1e-6
RWKV_GN_EPS = 64e-5
ROPE_THETA = 10000.0

LANES = 128
SUBLANES = 8

GDN_QK_HEADS = 16
GDN_V_HEADS = 32
GDN_HEAD = 128
GDN_CONV = 4
GDN_CHUNK = 64
GDN_KEY_DIM = GDN_QK_HEADS * GDN_HEAD
GDN_VAL_DIM = GDN_V_HEADS * GDN_HEAD
GDN_CONV_DIM = 2 * GDN_KEY_DIM + GDN_VAL_DIM
GDN_HB_PROMPT = 32
GDN_HB_SAMPLE = 4

MLA_HEADS = 16
MLA_Q_RANK = 512
MLA_KV_RANK = 512
MLA_NOPE = 128
MLA_ROPE = 64
MLA_V = 128
MLA_QK = MLA_NOPE + MLA_ROPE
PAGE_SIZE = 128
DEC_PAGES_PER_STEP = 16
DEC_PAGES_PER_CHUNK = 16
MLA_FLASH_HEADS = 2
MLA_FLASH_TILE = 1024

RWKV_HEAD = 64
RWKV_CHUNK = 64
RWKV_PB_PROMPT = 16
RWKV_PB_SAMPLE = 4
SEQ_GROUP = 8

VMEM_LIMIT = 48 * 1024 * 1024
WIDE_ROW_TILE = 1024


def _cparams(sem):
    return pltpu.CompilerParams(dimension_semantics=sem, vmem_limit_bytes=VMEM_LIMIT)


def _row_tile(*row_counts):
    for t in (512, 256, 128, 64, 32, 16, 8):
        if all(r % t == 0 for r in row_counts):
            return t
    raise ValueError(f"row counts {row_counts} are not multiples of {SUBLANES}")


def _col_tile(n):
    for t in (1024, 640, 512, 384, 256, 128):
        if n % t == 0:
            return t
    return n


def _dot(a, b):
    return jnp.dot(a.astype(BF16), b.astype(BF16), preferred_element_type=F32)


def _dot_nt(a, b):
    return lax.dot_general(a.astype(BF16), b.astype(BF16), (((1,), (1,)), ((), ())),
                           preferred_element_type=F32)


def _mm(a, b):
    if a.ndim == 2:
        return _dot(a, b)
    return jnp.einsum("bmk,bkn->bmn", a.astype(BF16), b.astype(BF16), preferred_element_type=F32)


def _mm_nt(a, b):
    if a.ndim == 2:
        return _dot_nt(a, b)
    return jnp.einsum("bmk,bnk->bmn", a.astype(BF16), b.astype(BF16), preferred_element_type=F32)


def _split3(x):
    hi = x.astype(BF16)
    r1 = x - hi.astype(F32)
    mid = r1.astype(BF16)
    lo = (r1 - mid.astype(F32)).astype(BF16)
    return hi, mid, lo


def _dot_exact_lhs(a_bf16, x):
    hi, mid, lo = _split3(x)
    return (jnp.dot(a_bf16, hi, preferred_element_type=F32)
            + jnp.dot(a_bf16, mid, preferred_element_type=F32)
            + jnp.dot(a_bf16, lo, preferred_element_type=F32))


def _dot_exact_rhs(x, b_bf16):
    hi, mid, lo = _split3(x)
    return (jnp.dot(hi, b_bf16, preferred_element_type=F32)
            + jnp.dot(mid, b_bf16, preferred_element_type=F32)
            + jnp.dot(lo, b_bf16, preferred_element_type=F32))


def _rms(x, g, eps=NORM_EPS):
    return x * lax.rsqrt(jnp.mean(x * x, axis=-1, keepdims=True) + eps) * g


def _sigmoid(x):
    return 1.0 / (1.0 + jnp.exp(-x))


def _softplus(x):
    return jnp.maximum(x, 0.0) + jnp.log(1.0 + jnp.exp(-jnp.abs(x)))


def _unit_lower_inverse(a, ii, jj, seq_rows):
    base = min(SUBLANES, seq_rows)
    sh = int(math.log2(base))
    a0 = jnp.where(jnp.right_shift(ii, sh) == jnp.right_shift(jj, sh), a, 0.0)
    x = jnp.where(ii == jj, 1.0, 0.0) - a0
    p = a0
    k = 1
    while 2 * k < base:
        p = _mm(p, p)
        x = x + _mm(x, p)
        k *= 2
    s = base
    while s < seq_rows:
        sh = int(math.log2(s))
        lower_left = jnp.logical_and(
            jnp.right_shift(ii, sh + 1) == jnp.right_shift(jj, sh + 1),
            jnp.logical_and(jnp.bitwise_and(jnp.right_shift(ii, sh), 1) == 1,
                            jnp.bitwise_and(jnp.right_shift(jj, sh), 1) == 0))
        e = jnp.where(lower_left, a, 0.0)
        x = x - _mm(_mm(x, e), x)
        s *= 2
    return x


def _linear_kernel(*refs, mode):
    if mode == "norm":
        x_ref, g_ref, w_ref, o_ref, xs_ref = refs
    elif mode == "mix":
        x_ref, p_ref, mu_ref, w_ref, o_ref, xs_ref = refs
    else:
        x_ref, w_ref, o_ref, xs_ref = refs

    @pl.when(pl.program_id(1) == 0)
    def _():
        x = x_ref[...]
        if mode == "norm":
            x = _rms(x, g_ref[...])
        elif mode == "mix":
            x = x + (p_ref[...] - x) * mu_ref[...]
        xs_ref[...] = x.astype(BF16)

    o_ref[...] = jnp.dot(xs_ref[...], w_ref[...].astype(BF16), preferred_element_type=F32)


def _linear(x, w, *, mode="none", g=None, prev=None, mu=None, xcol=0, row0=0, rows=None, wsel=None, ncols=None, tm):
    stacked = w.ndim == 3
    k = w.shape[-2]
    n = ncols if stacked else w.shape[1]
    rows = x.shape[0] - row0 if rows is None else rows
    if stacked and rows % WIDE_ROW_TILE == 0 and row0 % WIDE_ROW_TILE == 0:
        tm = WIDE_ROW_TILE
    tn = _col_tile(n) if not stacked else next(t for t in (512, 256, 128) if n % t == 0)
    rb0 = row0 // tm
    xmap = lambda i, j: (i + rb0, xcol)
    vec = pl.BlockSpec((1, k), lambda i, j: (0, 0))
    in_specs, args = [pl.BlockSpec((tm, k), xmap)], [x]
    if mode == "norm":
        in_specs.append(vec)
        args.append(g.reshape(1, k))
    elif mode == "mix":
        in_specs += [pl.BlockSpec((tm, k), xmap), vec]
        args += [prev, mu.reshape(1, k)]
    if stacked:
        in_specs.append(pl.BlockSpec((None, k, tn), lambda i, j: (wsel, 0, j)))
    else:
        in_specs.append(pl.BlockSpec((k, tn), lambda i, j: (0, j)))
    args.append(w)
    return pl.pallas_call(
        functools.partial(_linear_kernel, mode=mode),
        grid=(rows // tm, n // tn),
        in_specs=in_specs,
        out_specs=pl.BlockSpec((tm, tn), lambda i, j: (i, j)),
        out_shape=jax.ShapeDtypeStruct((rows, n), F32),
        scratch_shapes=[pltpu.VMEM((tm, k), BF16)],
        compiler_params=_cparams(("parallel", "arbitrary")),
        name=f"linear_{mode}",
    )(*args)


def _norm_shift_kernel(x_ref, x8_ref, st_ref, g_ref, h_ref, p_ref, *, tm, npb, seq_len):
    i = pl.program_id(0)
    g = g_ref[...]
    h = _rms(x_ref[...], g)
    h_ref[...] = h
    rolled = pltpu.roll(h, 1, 0)
    row = lax.broadcasted_iota(jnp.int32, h.shape, 0)
    first = lax.rem(i * tm, seq_len) == 0
    carry = jnp.where(first, 0.0, _rms(x8_ref[...], g)[SUBLANES - 1:SUBLANES])
    prev_p = jnp.where(row == 0, carry, rolled)
    prev_s = jnp.where(jnp.bitwise_and(row, SUBLANES - 1) == 0, st_ref[...], rolled)
    p_ref[...] = jnp.where(i < npb, prev_p, prev_s)


def _norm_shift(x, g, shift_rows, *, tp, seq_len, tm):
    t, d = x.shape
    npb = tp // tm
    per = tm // SUBLANES
    row = pl.BlockSpec((tm, d), lambda i: (i, 0))
    return pl.pallas_call(
        functools.partial(_norm_shift_kernel, tm=tm, npb=npb, seq_len=seq_len),
        grid=(t // tm,),
        in_specs=[row, pl.BlockSpec((SUBLANES, d), lambda i: (jnp.maximum(i * per - 1, 0), 0)),
                  pl.BlockSpec((tm, d), lambda i: (jnp.maximum(i - npb, 0), 0)),
                  pl.BlockSpec((1, d), lambda i: (0, 0))],
        out_specs=[row, row],
        out_shape=[jax.ShapeDtypeStruct((t, d), F32), jax.ShapeDtypeStruct((t, d), F32)],
        compiler_params=_cparams(("parallel",)),
        name="rms_norm_shift",
    )(x, x, shift_rows, g.reshape(1, d))


def _out_proj_kernel(yp_ref, ys_ref, w_ref, r_ref, g_ref, o_ref, acc_ref, *, npb, nk):
    i = pl.program_id(0)
    k = pl.program_id(1)

    @pl.when(k == 0)
    def _():
        acc_ref[...] = jnp.zeros_like(acc_ref)

    @pl.when(i < npb)
    def _():
        acc_ref[...] += jnp.dot(yp_ref[...].astype(BF16), w_ref[...], preferred_element_type=F32)

    @pl.when(i >= npb)
    def _():
        acc_ref[...] += jnp.dot(ys_ref[...].astype(BF16), w_ref[...], preferred_element_type=F32)

    @pl.when(k == nk - 1)
    def _():
        o_ref[...] = r_ref[...] + _rms(acc_ref[...], g_ref[...])


def _out_proj(y_p, y_s, w, resid, g, *, tm):
    kdim, d = w.shape
    tk = _col_tile(kdim)
    npb, nsb, nk = y_p.shape[0] // tm, y_s.shape[0] // tm, kdim // tk
    return pl.pallas_call(
        functools.partial(_out_proj_kernel, npb=npb, nk=nk),
        grid=(npb + nsb, nk),
        in_specs=[
            pl.BlockSpec((tm, tk), lambda i, k: (jnp.minimum(i, npb - 1), jnp.where(i < npb, k, nk - 1))),
            pl.BlockSpec((tm, tk), lambda i, k: (jnp.maximum(i - npb, 0), jnp.where(i >= npb, k, 0))),
            pl.BlockSpec((tk, d), lambda i, k: (k, 0)),
            pl.BlockSpec((tm, d), lambda i, k: (i, 0)),
            pl.BlockSpec((1, d), lambda i, k: (0, 0)),
        ],
        out_specs=pl.BlockSpec((tm, d), lambda i, k: (i, 0)),
        out_shape=jax.ShapeDtypeStruct(resid.shape, F32),
        scratch_shapes=[pltpu.VMEM((tm, d), F32)],
        compiler_params=_cparams(("parallel", "arbitrary")),
        name="out_proj_resnorm",
    )(y_p, y_s, w, resid, g.reshape(1, d))


def _mlp_kernel(x_ref, g2_ref, wu_ref, wd_ref, g3_ref, o_ref, xs_ref, *, nf):
    j = pl.program_id(1)

    @pl.when(j == 0)
    def _():
        xs_ref[...] = _rms(x_ref[...], g2_ref[...]).astype(BF16)
        o_ref[...] = jnp.zeros_like(o_ref)

    h = jnp.dot(xs_ref[...], wu_ref[...].astype(BF16), preferred_element_type=F32)
    h = jnp.square(jnp.maximum(h, 0.0))
    o_ref[...] += jnp.dot(h.astype(BF16), wd_ref[...].astype(BF16), preferred_element_type=F32)

    @pl.when(j == nf - 1)
    def _():
        o_ref[...] = x_ref[...] + _rms(o_ref[...], g3_ref[...])


def _mlp(x, g2, w_up, w_down, layer, g3, *, tm):
    t, d = x.shape
    f = w_up.shape[-1]
    tf = _col_tile(f)
    vec = pl.BlockSpec((1, d), lambda i, j: (0, 0))
    return pl.pallas_call(
        functools.partial(_mlp_kernel, nf=f // tf),
        grid=(t // tm, f // tf),
        in_specs=[pl.BlockSpec((tm, d), lambda i, j: (i, 0)), vec,
                  pl.BlockSpec((None, d, tf), lambda i, j: (layer, 0, j)),
                  pl.BlockSpec((None, tf, d), lambda i, j: (layer, j, 0)), vec],
        out_specs=pl.BlockSpec((tm, d), lambda i, j: (i, 0)),
        out_shape=jax.ShapeDtypeStruct((t, d), F32),
        scratch_shapes=[pltpu.VMEM((tm, d), BF16)],
        compiler_params=_cparams(("parallel", "arbitrary")),
        name="sq_relu_mlp",
    )(x, g2.reshape(1, d), w_up, w_down, g3.reshape(1, d))


def _head_linear_kernel(x_ref, w_ref, o_ref):
    x = x_ref[...]
    y = jnp.dot(x.reshape(-1, x.shape[-1]).astype(BF16), w_ref[...], preferred_element_type=F32)
    o_ref[...] = y.reshape(o_ref.shape)


def _head_linear(x, w, *, seq_len, row0=0, rows=None, x_by_head=False, out_by_head=False, tm):
    nh, k, n = w.shape
    rows = x.shape[0] * x.shape[2] if x_by_head else rows
    rb0 = row0 // tm
    by_head = lambda width: pl.BlockSpec((tm // seq_len, None, seq_len, width), lambda i, h: (i, h, 0, 0))
    x_spec = by_head(k) if x_by_head else pl.BlockSpec((tm, k), lambda i, h: (i + rb0, h))
    if out_by_head:
        out_spec, out_shape = by_head(n), (rows // seq_len, nh, seq_len, n)
    else:
        out_spec, out_shape = pl.BlockSpec((tm, n), lambda i, h: (i, h)), (rows, nh * n)
    return pl.pallas_call(
        _head_linear_kernel,
        grid=(rows // tm, nh),
        in_specs=[x_spec, pl.BlockSpec((None, k, n), lambda i, h: (h, 0, 0))],
        out_specs=out_spec,
        out_shape=jax.ShapeDtypeStruct(out_shape, F32),
        compiler_params=_cparams(("parallel", "parallel")),
        name="head_linear",
    )(x, w)


def _gdn_gate_kernel(x_ref, alog_ref, dtb_ref, e_ref, o_ref):
    x = x_ref[...]
    lane = lax.broadcasted_iota(jnp.int32, x.shape, 1)
    g = -jnp.exp(alog_ref[...]) * _softplus(x + dtb_ref[...])
    gb = jnp.where(lane < GDN_V_HEADS, g, _sigmoid(x))
    o_ref[...] = _dot_exact_rhs(gb, e_ref[...])


def _gdn_gates(ab, a_log, dt_bias, *, tm):
    t, n = ab.shape
    tn = 16 * LANES
    pad = jnp.zeros((GDN_V_HEADS,), F32)
    vec = pl.BlockSpec((1, n), lambda i, j: (0, 0))
    expand = jnp.repeat(jnp.eye(n, dtype=BF16), LANES, axis=1)
    return pl.pallas_call(
        _gdn_gate_kernel,
        grid=(t // tm, n * LANES // tn),
        in_specs=[pl.BlockSpec((tm, n), lambda i, j: (i, 0)), vec, vec,
                  pl.BlockSpec((n, tn), lambda i, j: (0, j))],
        out_specs=pl.BlockSpec((tm, tn), lambda i, j: (i, j)),
        out_shape=jax.ShapeDtypeStruct((t, n * LANES), F32),
        compiler_params=_cparams(("parallel", "parallel")),
        name="gdn_gates",
    )(ab, jnp.concatenate([a_log, pad]).reshape(1, n), jnp.concatenate([dt_bias, pad]).reshape(1, n), expand)


def _gdn_conv_kernel(x_ref, prev_ref, w_ref, o_ref, *, tm, tc, seq_len, nq, nqk, sample):
    i = pl.program_id(0)
    j = pl.program_id(1)
    x = x_ref[...]
    w = w_ref[...]
    tap = lambda s: w[GDN_CONV - 1 - s:GDN_CONV - s]
    is_qk = j < nqk
    qscale = jnp.where(j < nq, GDN_HEAD ** -0.5, 1.0)

    def finish(acc, rows):
        y = acc * _sigmoid(acc)
        for hh in range(tc // GDN_HEAD):
            sl = slice(hh * GDN_HEAD, (hh + 1) * GDN_HEAD)
            yh = y[:, sl]
            nrm = lax.rsqrt(jnp.sum(yh * yh, axis=-1, keepdims=True) + L2_EPS) * qscale
            o_ref[rows, sl] = (yh * jnp.where(is_qk, nrm, 1.0))[rows]

    acc = x * tap(0)
    if sample:
        hist = prev_ref[...]
        tok = jnp.bitwise_and(lax.broadcasted_iota(jnp.int32, x.shape, 0), SUBLANES - 1)
        for s in range(1, GDN_CONV):
            xs = jnp.where(tok >= s, pltpu.roll(x, s, 0), pltpu.roll(hist, tm - SUBLANES + s, 0))
            acc = acc + xs * tap(s)
        finish(acc, slice(None))
    else:
        first = lax.rem(i * tm, seq_len) == 0
        prev = jnp.where(first, 0.0, prev_ref[...])
        row = lax.broadcasted_iota(jnp.int32, prev.shape, 0)
        head_rows = slice(0, SUBLANES)
        acc8 = x[head_rows] * tap(0)
        for s in range(1, GDN_CONV):
            xs = pltpu.roll(x, s, 0)
            acc = acc + xs * tap(s)
            acc8 = acc8 + jnp.where(row < s, pltpu.roll(prev, s, 0), xs[head_rows]) * tap(s)
        if tm > SUBLANES:
            finish(acc, slice(SUBLANES, None))
        finish(acc8, head_rows)


def _gdn_conv(proj, hist, conv_w, *, row0, rows, seq_len, sample, tm):
    tc = 512
    rb0 = row0 // tm
    if sample:
        prev_spec = pl.BlockSpec((tm, tc), lambda i, j: (i, j))
        prev = hist
    else:
        per = tm // SUBLANES
        prev_spec = pl.BlockSpec((SUBLANES, tc), lambda i, j: (jnp.maximum((i + rb0) * per - 1, 0), j))
        prev = proj
    return pl.pallas_call(
        functools.partial(_gdn_conv_kernel, tm=tm, tc=tc, seq_len=seq_len, nq=GDN_KEY_DIM // tc,
                          nqk=2 * GDN_KEY_DIM // tc, sample=sample),
        grid=(rows // tm, GDN_CONV_DIM // tc),
        in_specs=[pl.BlockSpec((tm, tc), lambda i, j: (i + rb0, j)), prev_spec,
                  pl.BlockSpec((GDN_CONV, tc), lambda i, j: (0, j))],
        out_specs=pl.BlockSpec((tm, tc), lambda i, j: (i, j)),
        out_shape=jax.ShapeDtypeStruct((rows, GDN_CONV_DIM), F32),
        compiler_params=_cparams(("parallel", "parallel")),
        name="gdn_conv_sample" if sample else "gdn_conv_prompt",
    )(proj, prev, conv_w)


def _stack_lanes(x, n):
    return jnp.stack([x[:, i * LANES:(i + 1) * LANES] for i in range(n)])


def _stack_cols(x, lo, n):
    return jnp.stack([x[:, lo + i:lo + i + 1] for i in range(n)])


def _over_seqs(x, nseq):
    return jnp.broadcast_to(x[None], (nseq,) + x.shape).reshape((nseq * x.shape[0],) + x.shape[1:])


def _chunk_masks(rows, seq_rows):
    ii = lax.broadcasted_iota(jnp.int32, (rows, rows), 0)
    jj = lax.broadcasted_iota(jnp.int32, (rows, rows), 1)
    shift = int(math.log2(seq_rows))
    same = jnp.right_shift(ii, shift) == jnp.right_shift(jj, shift)
    incl = jnp.logical_and(same, ii >= jj)
    strict = jnp.logical_and(same, ii > jj)
    last = jnp.logical_and(same, jnp.bitwise_and(jj, seq_rows - 1) == seq_rows - 1)
    return ii, jj, incl, strict, last


def _seq_mask(nseq, rows, seq_rows):
    shape = (nseq, 1, rows, 1)
    n = lax.broadcasted_iota(jnp.int32, shape, 0)
    r = lax.broadcasted_iota(jnp.int32, shape, 2)
    return jnp.right_shift(r, int(math.log2(seq_rows))) == n


def _gdn_scan_kernel(*refs, rows, seq_rows, nseq, nchunks, has_s0, hb, nprev):
    q_ref, k_ref, v_ref, z_ref, g_ref, b_ref, on_ref = refs[:7]
    s0_ref = refs[7] if has_s0 else None
    prev_refs = refs[7 + has_s0:7 + has_s0 + nprev]
    o_ref, sout_ref, s_scr = refs[7 + has_s0 + nprev:]
    c = pl.program_id(2)
    nb = nseq * hb

    @pl.when(c == 0)
    def _():
        if has_s0:
            s_scr[...] = s0_ref[...].reshape(nb, GDN_HEAD, GDN_HEAD)
        else:
            s_scr[...] = jnp.zeros_like(s_scr)

    ii, jj, incl, strict, last = _chunk_masks(rows, seq_rows)
    cum_all = _dot_exact_lhs(jnp.where(incl, 1.0, 0.0).astype(BF16), g_ref[...])
    if nseq == 1:
        tot_all = cum_all[rows - 1:rows]
    else:
        tot_all = _dot_exact_lhs(jnp.where(last, 1.0, 0.0).astype(BF16), cum_all)
    cum = _stack_lanes(cum_all, hb)
    e_cum = _stack_lanes(jnp.exp(cum_all), hb)
    e_tail = _stack_lanes(jnp.exp(tot_all - cum_all), hb)
    g_tot = _stack_lanes(jnp.exp(tot_all), hb)
    beta = _stack_lanes(b_ref[...], hb)
    cum_c = cum[:, :, :rows]
    cum_r = jnp.sum(jnp.where(ii == jj, cum_c, 0.0), axis=1, keepdims=True)
    decay = jnp.where(incl, jnp.exp(jnp.where(incl, cum_c - cum_r, 0.0)), 0.0)

    hq = hb // 2
    q2 = _stack_lanes(q_ref[...], hq)
    k2 = _stack_lanes(k_ref[...], hq)
    per_v_head = lambda x: jnp.stack([x[h // 2] for h in range(hb)])
    kk = per_v_head(_mm_nt(k2, k2))
    qk = per_v_head(_mm_nt(q2, k2))
    q = per_v_head(q2)
    k = per_v_head(k2)
    v = _stack_lanes(v_ref[...], hb)
    z = _stack_lanes(z_ref[...], hb)

    a_mat = jnp.where(strict, beta[:, :, :rows] * decay * kk, 0.0)
    t_inv = _unit_lower_inverse(a_mat, ii, jj, seq_rows)
    u_v = _mm(t_inv, beta * v)
    w_k = _mm(t_inv, (beta * e_cum) * k)
    p_qk = decay * qk
    q_g = q * e_cum
    k_d_t = jnp.swapaxes(k * e_tail, 1, 2)
    s = s_scr[...]
    sb = s.astype(BF16)
    if nseq == 1:
        u = u_v - _mm(w_k, sb)
        o = _mm(q_g, sb) + _mm(p_qk, u)
        s_scr[...] = g_tot * s + _mm(k_d_t, u)
    else:
        mine = _seq_mask(nseq, rows, seq_rows)
        pick = lambda x: jnp.sum(jnp.where(mine, x.reshape((nseq, hb) + x.shape[1:]), 0.0), axis=0)
        u = u_v - pick(_mm(_over_seqs(w_k, nseq), sb))
        o = pick(_mm(_over_seqs(q_g, nseq), sb)) + _mm(p_qk, u)
        u_n = jnp.where(mine, u[None], 0.0).reshape(nb, rows, GDN_HEAD)
        gt = jnp.stack([g_tot[:, (n + 1) * seq_rows - 1:(n + 1) * seq_rows] for n in range(nseq)])
        s_scr[...] = gt.reshape(nb, 1, GDN_HEAD) * s + _mm(_over_seqs(k_d_t, nseq), u_n)
    o = _rms(o, on_ref[...]) * (z * _sigmoid(z))
    for h in range(hb):
        o_ref[:, h * GDN_HEAD:(h + 1) * GDN_HEAD] = o[h]

    @pl.when(c == nchunks - 1)
    def _():
        new = s_scr[...].reshape(nseq, hb, GDN_HEAD, GDN_HEAD)
        if nprev:
            for l, p_ref in enumerate(prev_refs):
                sout_ref[:, l] = p_ref[...]
            sout_ref[:, nprev] = new
        else:
            sout_ref[...] = new


def _gdn_scan(act, proj, gates, o_norm, s0, layer, *, batch, seq_len, row0, sample, prev_states=()):
    hb = GDN_HB_SAMPLE if sample else GDN_HB_PROMPT
    ng = GDN_V_HEADS // hb
    hgw = hb * GDN_HEAD
    qkw = hgw // 2
    if sample:
        nseq, seq_rows, nchunks = SEQ_GROUP, seq_len, 1
        rows = nseq * seq_rows
        grid = (batch // nseq, ng, 1)
    else:
        nseq, seq_rows, rows = 1, GDN_CHUNK, GDN_CHUNK
        nchunks = seq_len // rows
        grid = (batch, ng, nchunks)
    rb0 = row0 // rows
    rmap = lambda b, g, c: b * nchunks + c
    in_specs = [
        pl.BlockSpec((rows, qkw), lambda b, g, c: (rmap(b, g, c), g)),
        pl.BlockSpec((rows, qkw), lambda b, g, c: (rmap(b, g, c), GDN_KEY_DIM // qkw + g)),
        pl.BlockSpec((rows, hgw), lambda b, g, c: (rmap(b, g, c), 2 * GDN_KEY_DIM // hgw + g)),
        pl.BlockSpec((rows, hgw), lambda b, g, c: (rmap(b, g, c) + rb0, GDN_CONV_DIM // hgw + g)),
        pl.BlockSpec((rows, hgw), lambda b, g, c: (rmap(b, g, c) + rb0, g)),
        pl.BlockSpec((rows, hgw), lambda b, g, c: (rmap(b, g, c) + rb0, ng + g)),
        pl.BlockSpec((1, GDN_HEAD), lambda b, g, c: (0, 0)),
    ]
    args = [act, act, act, proj, gates, gates, o_norm.reshape(1, GDN_HEAD)]
    if sample:
        in_specs.append(pl.BlockSpec((nseq, None, hb, GDN_HEAD, GDN_HEAD),
                                     lambda b, g, c: (b, layer, g, 0, 0)))
        args.append(s0)
    state_spec = pl.BlockSpec((nseq, hb, GDN_HEAD, GDN_HEAD), lambda b, g, c: (b, g, 0, 0))
    state_shape = jax.ShapeDtypeStruct((batch, GDN_V_HEADS, GDN_HEAD, GDN_HEAD), F32)
    nprev = len(prev_states)
    if nprev:
        in_specs += [state_spec] * nprev
        args += list(prev_states)
        state_spec = pl.BlockSpec((nseq, nprev + 1, hb, GDN_HEAD, GDN_HEAD), lambda b, g, c: (b, 0, g, 0, 0))
        state_shape = jax.ShapeDtypeStruct((batch, nprev + 1, GDN_V_HEADS, GDN_HEAD, GDN_HEAD), F32)
    return pl.pallas_call(
        functools.partial(_gdn_scan_kernel, rows=rows, seq_rows=seq_rows, nseq=nseq, nchunks=nchunks,
                          has_s0=sample, hb=hb, nprev=nprev),
        grid=grid,
        in_specs=in_specs,
        out_specs=[pl.BlockSpec((rows, hgw), lambda b, g, c: (rmap(b, g, c), g)), state_spec],
        out_shape=[jax.ShapeDtypeStruct((batch * seq_len, GDN_VAL_DIM), F32), state_shape],
        scratch_shapes=[pltpu.VMEM((nseq * hb, GDN_HEAD, GDN_HEAD), F32)],
        compiler_params=_cparams(("parallel", "parallel", "arbitrary")),
        name="gdn_scan_sample" if sample else "gdn_scan_prompt",
    )(*args)


def _mla_post_kernel(ckv_in_ref, kpe_in_ref, kpe_sw_ref, qpe_ref, qsw_ref, cos_ref, sin_ref, g_ref,
                     ckv_ref, kpe_ref, qp_ref):
    ckv_ref[...] = _rms(ckv_in_ref[...], g_ref[...])
    cos = cos_ref[...]
    sin = sin_ref[...]
    kpe_ref[...] = kpe_in_ref[...] * cos + kpe_sw_ref[...] * sin
    for h in range(MLA_HEADS):
        sl = slice(h * LANES, (h + 1) * LANES)
        qp_ref[:, sl] = qpe_ref[:, sl] * cos + qsw_ref[:, sl] * sin


def _mla_post(c, q_raw, cos2, sin2, kv_norm, *, tm):
    t = c.shape[0]
    hw = MLA_HEADS * LANES
    c0 = MLA_Q_RANK // MLA_KV_RANK
    k0 = (MLA_Q_RANK + MLA_KV_RANK) // LANES
    row = lambda w, col: pl.BlockSpec((tm, w), lambda i: (i, col))
    return pl.pallas_call(
        _mla_post_kernel,
        grid=(t // tm,),
        in_specs=[row(MLA_KV_RANK, c0), row(LANES, k0), row(LANES, k0 + 1),
                  row(hw, 1), row(hw, 2), row(LANES, 0), row(LANES, 0),
                  pl.BlockSpec((1, MLA_KV_RANK), lambda i: (0, 0))],
        out_specs=[row(MLA_KV_RANK, 0), row(LANES, 0), row(hw, 0)],
        out_shape=[jax.ShapeDtypeStruct((t, MLA_KV_RANK), F32), jax.ShapeDtypeStruct((t, LANES), F32),
                   jax.ShapeDtypeStruct((t, hw), F32)],
        compiler_params=_cparams(("parallel",)),
        name="mla_post",
    )(c, c, c, q_raw, q_raw, cos2, sin2, kv_norm.reshape(1, MLA_KV_RANK))


def _online_softmax_chunks(scores, values, nchunks, m, l, acc, mask=None):
    s_next = scores(0)
    for c in range(nchunks):
        s = s_next
        if c + 1 < nchunks:
            s_next = scores(c + 1)
        if mask is not None:
            s = mask(c, s)
        m_new = jnp.maximum(m, jnp.max(s, axis=-1, keepdims=True))
        alpha = jnp.exp(m - m_new)
        p = jnp.exp(s - m_new)
        l = alpha * l + jnp.sum(p, axis=-1, keepdims=True)
        acc = alpha * acc + jnp.dot(p.astype(BF16), values(c), preferred_element_type=F32)
        m = m_new
    return m, l, acc


def _flash_kernel(qi_ref, ki_ref, qn_ref, qp_ref, kn_ref, kp_ref, v_ref, o_ref, qs_ref, kc_ref,
                  m_ref, l_ref, acc_ref, *, scale, nh):
    t = pl.program_id(2)
    qi = qi_ref[t]
    ki = ki_ref[t]
    tq = qn_ref.shape[0]
    head = lambda e: slice(e * LANES, (e + 1) * LANES)

    @pl.when(ki == 0)
    def _():
        for e in range(nh):
            qs_ref[e, :, :LANES] = (qn_ref[:, head(e)] * scale).astype(BF16)
            qs_ref[e, :, LANES:] = (qp_ref[:, head(e)] * scale).astype(BF16)
        m_ref[...] = jnp.full_like(m_ref, -jnp.inf)
        l_ref[...] = jnp.zeros_like(l_ref)
        acc_ref[...] = jnp.zeros_like(acc_ref)

    kpb = kp_ref[...].astype(BF16)
    for e in range(nh):
        kc_ref[e, :, :LANES] = kn_ref[:, head(e)].astype(BF16)
        kc_ref[e, :, LANES:] = kpb

    def block(diagonal):
        nr = 2 if tq >= 2 * LANES else 1
        tr = tq // nr
        streams = [(e, r) for e in range(nh) for r in range(nr)]
        rows = lambda r: slice(r * tr, (r + 1) * tr)
        score = lambda e, r: _dot_nt(qs_ref[e, rows(r), :], kc_ref[e])
        ahead = 2
        pending = [score(*st) for st in streams[:ahead]]
        for i, (e, r) in enumerate(streams):
            s = pending.pop(0)
            if i + ahead < len(streams):
                pending.append(score(*streams[i + ahead]))
            if diagonal:
                row = lax.broadcasted_iota(jnp.int32, s.shape, 0) + r * tr
                col = lax.broadcasted_iota(jnp.int32, s.shape, 1)
                s = jnp.where(row >= col, s, -jnp.inf)
            m_old = m_ref[e, rows(r), :]
            m_new = jnp.maximum(m_old, jnp.max(s, axis=-1, keepdims=True))
            alpha = jnp.exp(m_old - m_new)
            p = jnp.exp(s - m_new)
            l_ref[e, rows(r), :] = alpha * l_ref[e, rows(r), :] + jnp.sum(p, axis=-1, keepdims=True)
            acc_ref[e, rows(r), :] = alpha * acc_ref[e, rows(r), :] + _dot(p, v_ref[:, head(e)])
            m_ref[e, rows(r), :] = m_new

    @pl.when(ki < qi)
    def _():
        block(False)

    @pl.when(ki == qi)
    def _():
        block(True)
        for e in range(nh):
            o_ref[:, head(e)] = acc_ref[e] / l_ref[e]


def _mla_flash(q_raw, qp, kv, kpe, *, batch, seq_len):
    tq = next(t for t in (1024, 512, 256, 128, 64) if t <= MLA_FLASH_TILE and seq_len % t == 0)
    nq = seq_len // tq
    pairs = [(qi, ki) for qi in range(nq) for ki in range(qi + 1)]
    qi_of = jnp.asarray([p[0] for p in pairs], jnp.int32)
    ki_of = jnp.asarray([p[1] for p in pairs], jnp.int32)
    nh = MLA_FLASH_HEADS
    hw = nh * LANES
    qmap = lambda b, h, t, qi, ki: (b * nq + qi[t], h)
    kmap = lambda off: (lambda b, h, t, qi, ki: (b * nq + ki[t], off + h))
    grid_spec = pltpu.PrefetchScalarGridSpec(
        num_scalar_prefetch=2,
        grid=(batch, MLA_HEADS // nh, len(pairs)),
        in_specs=[pl.BlockSpec((tq, hw), qmap), pl.BlockSpec((tq, hw), qmap),
                  pl.BlockSpec((tq, hw), kmap(0)),
                  pl.BlockSpec((tq, LANES), lambda b, h, t, qi, ki: (b * nq + ki[t], 0)),
                  pl.BlockSpec((tq, hw), kmap(MLA_HEADS // nh))],
        out_specs=pl.BlockSpec((tq, hw), qmap),
        scratch_shapes=[pltpu.VMEM((nh, tq, 2 * LANES), BF16), pltpu.VMEM((nh, tq, 2 * LANES), BF16),
                        pltpu.VMEM((nh, tq, 1), F32), pltpu.VMEM((nh, tq, 1), F32),
                        pltpu.VMEM((nh, tq, MLA_V), F32)],
    )
    return pl.pallas_call(
        functools.partial(_flash_kernel, scale=MLA_QK ** -0.5, nh=nh),
        grid_spec=grid_spec,
        out_shape=jax.ShapeDtypeStruct((batch * seq_len, MLA_HEADS * MLA_V), F32),
        compiler_params=_cparams(("parallel", "parallel", "arbitrary")),
        name="mla_flash_prompt",
    )(qi_of, ki_of, q_raw, qp, kv, kpe, kv)


def _decode_kernel(*refs, npg, ngroups, new_len, scale):
    pt_ref = refs[0]
    ql_ref, qp_ref = refs[1], refs[2]
    ckv_refs = refs[3:3 + npg]
    kpe_refs = refs[3 + npg:3 + 2 * npg]
    cnew_ref, knew_ref = refs[3 + 2 * npg], refs[4 + 2 * npg]
    o_ref = refs[5 + 2 * npg]
    qls_ref, qps_ref, m_ref, l_ref, acc_ref, kc_ref, pc_ref = refs[6 + 2 * npg:]
    del pt_ref
    g = pl.program_id(1)

    @pl.when(g == 0)
    def _():
        qls_ref[...] = (ql_ref[...] * scale).astype(BF16)
        qpe = _stack_lanes(qp_ref[...], MLA_HEADS).reshape(qps_ref.shape[0], LANES)
        qps_ref[...] = (qpe[:, :MLA_ROPE] * scale).astype(BF16)
        m_ref[...] = jnp.full_like(m_ref, -jnp.inf)
        l_ref[...] = jnp.zeros_like(l_ref)
        acc_ref[...] = jnp.zeros_like(acc_ref)

    for i in range(npg):
        sl = slice(i * PAGE_SIZE, (i + 1) * PAGE_SIZE)
        kc_ref[sl, :] = ckv_refs[i][...].astype(BF16)
        pc_ref[:, sl] = kpe_refs[i][...].astype(BF16)
    chunk_pages = math.gcd(npg, DEC_PAGES_PER_CHUNK)
    width = chunk_pages * PAGE_SIZE
    chunk = lambda c: slice(c * width, (c + 1) * width)
    scores = lambda c: _dot_nt(qls_ref[...], kc_ref[chunk(c), :]) + _dot(qps_ref[...], pc_ref[:, chunk(c)])
    values = lambda c: kc_ref[chunk(c), :]
    m, l, acc = _online_softmax_chunks(scores, values, npg // chunk_pages, m_ref[...], l_ref[...], acc_ref[...])
    m_ref[...] = m
    l_ref[...] = l
    acc_ref[...] = acc

    @pl.when(g == ngroups - 1)
    def _():
        kb = cnew_ref[...].astype(BF16)
        tq = jnp.bitwise_and(lax.broadcasted_iota(jnp.int32, (qls_ref.shape[0], kb.shape[0]), 0), new_len - 1)
        tk = lax.broadcasted_iota(jnp.int32, (qls_ref.shape[0], kb.shape[0]), 1)
        visible = jnp.logical_and(tk < new_len, tk <= tq)
        m2, l2, acc2 = _online_softmax_chunks(
            lambda c: _dot_nt(qls_ref[...], kb) + _dot(qps_ref[...], knew_ref[...]), lambda c: kb, 1,
            m, l, acc, mask=lambda c, s: jnp.where(visible, s, -jnp.inf))
        o_ref[...] = acc2 / l2


def _mla_decode(q_lat, qp, cache_ckv, cache_kpe, page_table, ckv_new, kpe_new, layer, *, new_len):
    bsz, n_pages = page_table.shape
    nrow = q_lat.shape[1]
    npg = min(DEC_PAGES_PER_STEP, n_pages)
    ngroups = n_pages // npg
    qp_rb0 = qp.shape[0] // new_len - bsz
    page = lambda i, shape: pl.BlockSpec((None, None) + shape,
                                         lambda b, g, pt: (pt[b, g * npg + i], layer, 0, 0))
    in_specs = [pl.BlockSpec((None, nrow, MLA_KV_RANK), lambda b, g, pt: (b, 0, 0)),
                pl.BlockSpec((new_len, MLA_HEADS * LANES), lambda b, g, pt: (qp_rb0 + b, 0))]
    in_specs += [page(i, (PAGE_SIZE, MLA_KV_RANK)) for i in range(npg)]
    in_specs += [page(i, (MLA_ROPE, PAGE_SIZE)) for i in range(npg)]
    in_specs += [pl.BlockSpec((None, PAGE_SIZE, MLA_KV_RANK), lambda b, g, pt: (b, 0, 0)),
                 pl.BlockSpec((None, MLA_ROPE, PAGE_SIZE), lambda b, g, pt: (b, 0, 0))]
    grid_spec = pltpu.PrefetchScalarGridSpec(
        num_scalar_prefetch=1,
        grid=(bsz, ngroups),
        in_specs=in_specs,
        out_specs=pl.BlockSpec((None, nrow, MLA_KV_RANK), lambda b, g, pt: (b, 0, 0)),
        scratch_shapes=[pltpu.VMEM((nrow, MLA_KV_RANK), BF16), pltpu.VMEM((nrow, MLA_ROPE), BF16),
                        pltpu.VMEM((nrow, 1), F32), pltpu.VMEM((nrow, 1), F32),
                        pltpu.VMEM((nrow, MLA_KV_RANK), F32),
                        pltpu.VMEM((npg * PAGE_SIZE, MLA_KV_RANK), BF16),
                        pltpu.VMEM((MLA_ROPE, npg * PAGE_SIZE), BF16)],
    )
    return pl.pallas_call(
        functools.partial(_decode_kernel, npg=npg, ngroups=ngroups, new_len=new_len, scale=MLA_QK ** -0.5),
        grid_spec=grid_spec,
        out_shape=jax.ShapeDtypeStruct(q_lat.shape, F32),
        compiler_params=_cparams(("parallel", "arbitrary")),
        name="mla_decode",
    )(page_table, q_lat, qp, *([cache_ckv] * npg), *([cache_kpe] * npg), ckv_new, kpe_new)


def _lora_kernel(x_ref, p_ref, mu_ref, w1_ref, w2_ref, b_ref, o_ref, *, kind):
    x = x_ref[...]
    xm = x + (p_ref[...] - x) * mu_ref[...]
    t = jnp.dot(xm.astype(BF16), w1_ref[...], preferred_element_type=F32)
    if kind == "decay":
        t = jnp.tanh(t)
    elif kind == "gate":
        t = _sigmoid(t)
    y = jnp.dot(t.astype(BF16), w2_ref[...], preferred_element_type=F32)
    if kind == "decay":
        o_ref[...] = -_softplus(-(b_ref[...] + y)) - 0.5
    elif kind == "aaa":
        o_ref[...] = _sigmoid(b_ref[...] + y)
    else:
        o_ref[...] = y


def _lora(h, prev, mu, w1, w2, bias, *, kind, tm):
    t, d = h.shape
    r = w1.shape[1]
    row = pl.BlockSpec((tm, d), lambda i: (i, 0))
    vec = pl.BlockSpec((1, d), lambda i: (0, 0))
    return pl.pallas_call(
        functools.partial(_lora_kernel, kind=kind),
        grid=(t // tm,),
        in_specs=[row, row, vec, pl.BlockSpec((d, r), lambda i: (0, 0)),
                  pl.BlockSpec((r, d), lambda i: (0, 0)), vec],
        out_specs=row,
        out_shape=jax.ShapeDtypeStruct((t, d), F32),
        compiler_params=_cparams(("parallel",)),
        name=f"rwkv_lora_{kind}",
    )(h, prev, mu.reshape(1, d), w1, w2, bias.reshape(1, d))


def _rwkv_scan_kernel(*refs, rows, seq_rows, nseq, nchunks, has_h0, pb):
    r_ref, k_ref, v_ref, w_ref, a_ref, g_ref, kk_ref, ka_ref, rk_ref, lw_ref, lb_ref = refs[:11]
    if has_h0:
        h0_ref, y_ref, hout_ref, h_scr = refs[11:]
    else:
        y_ref, hout_ref, h_scr = refs[11:]
    c = pl.program_id(2)
    nb = nseq * pb

    @pl.when(c == 0)
    def _():
        if has_h0:
            h_scr[...] = h0_ref[...].reshape(nb, LANES, LANES)
        else:
            h_scr[...] = jnp.zeros_like(h_scr)

    ii, jj, incl, strict, last = _chunk_masks(rows, seq_rows)
    head0 = lax.broadcasted_iota(jnp.int32, (1, 1, LANES), 2) < RWKV_HEAD

    def head_sum(x):
        s0 = jnp.sum(jnp.where(head0, x, 0.0), axis=-1, keepdims=True)
        s1 = jnp.sum(jnp.where(head0, 0.0, x), axis=-1, keepdims=True)
        return jnp.where(head0, s0, s1)

    log_d2 = -jnp.exp(w_ref[...])
    cum2 = _dot_exact_lhs(jnp.where(incl, 1.0, 0.0).astype(BF16), log_d2)
    if nseq == 1:
        tot2 = cum2[rows - 1:rows]
    else:
        tot2 = _dot_exact_lhs(jnp.where(last, 1.0, 0.0).astype(BF16), cum2)
    stack = lambda x: _stack_lanes(x, pb)
    e_pos = stack(jnp.exp(cum2))
    e_neg = stack(jnp.exp(-cum2))
    e_prev = stack(jnp.exp(cum2 - log_d2))
    e_end = stack(jnp.exp(tot2 - cum2))
    gamma = stack(jnp.exp(tot2))

    r = stack(r_ref[...])
    k = stack(k_ref[...])
    v = stack(v_ref[...])
    a = stack(a_ref[...])
    kx = k * stack(kk_ref[...])
    kk = kx * lax.rsqrt(head_sum(kx * kx) + L2_EPS)
    k = k * (1.0 + (a - 1.0) * stack(ka_ref[...]))
    b = kk * a
    a_t = -kk * e_prev
    b_t = b * e_neg
    k_t = k * e_neg
    r_t = r * e_pos
    bd_t = jnp.swapaxes(b * e_end, 1, 2)
    kd_t = jnp.swapaxes(k * e_end, 1, 2)

    i2 = lax.broadcasted_iota(jnp.int32, (LANES, LANES), 0)
    j2 = lax.broadcasted_iota(jnp.int32, (LANES, LANES), 1)
    eye2 = i2 == j2
    block = (i2 < RWKV_HEAD) == (j2 < RWKV_HEAD)

    halves = lambda x: jnp.concatenate([jnp.where(head0, x, 0.0), jnp.where(head0, 0.0, x)], axis=0)
    twice = lambda x: jnp.concatenate([x, x], axis=0)
    join = lambda x2: jnp.where(head0, x2[:pb], x2[pb:])
    a_m = halves(a_t)
    r_m = halves(r_t)
    b2, k2, v2 = twice(b_t), twice(k_t), twice(v)
    a_ab = jnp.where(strict, _mm_nt(a_m, b2), 0.0)
    a_ak = jnp.where(strict, _mm_nt(a_m, k2), 0.0)
    p_rb = jnp.where(incl, _mm_nt(r_m, b2), 0.0)
    p_rk = jnp.where(incl, _mm_nt(r_m, k2), 0.0)
    t_inv = _unit_lower_inverse(-a_ab, ii, jj, seq_rows)
    akv = _mm(a_ak, v2)
    prkv = _mm(p_rk, v2)

    hs = h_scr[...]
    hbf = hs.astype(BF16)
    if nseq == 1:
        h2 = twice(hbf)
        u2 = _mm(t_inv, _mm(a_m, h2) + akv)
        y2 = _mm(r_m, h2) + _mm(p_rb, u2) + prkv
        u = join(u2)
        y = join(y2)
        g_col = jnp.sum(jnp.where(eye2, gamma, 0.0), axis=2, keepdims=True)
        h_scr[...] = g_col * hs + jnp.where(block, _mm(bd_t, u) + _mm(kd_t, v), 0.0)
    else:
        mine = _seq_mask(nseq, rows, seq_rows)
        h4 = hbf.reshape(nseq, pb, LANES, LANES)
        h2 = jnp.concatenate([h4, h4], axis=1).reshape(nseq * 2 * pb, LANES, LANES)
        pick = lambda x: jnp.sum(jnp.where(mine, x.reshape((nseq, 2 * pb) + x.shape[1:]), 0.0), axis=0)
        u2 = _mm(t_inv, pick(_mm(_over_seqs(a_m, nseq), h2)) + akv)
        y2 = pick(_mm(_over_seqs(r_m, nseq), h2)) + _mm(p_rb, u2) + prkv
        u = join(u2)
        y = join(y2)
        u_n = jnp.where(mine, u[None], 0.0).reshape(nb, rows, LANES)
        v_n = jnp.where(mine, v[None], 0.0).reshape(nb, rows, LANES)
        g_rows = jnp.stack([gamma[:, (n + 1) * seq_rows - 1:(n + 1) * seq_rows] for n in range(nseq)])
        g_col = jnp.sum(jnp.where(eye2, g_rows.reshape(nb, 1, LANES), 0.0), axis=2, keepdims=True)
        h_scr[...] = g_col * hs + jnp.where(
            block, _mm(_over_seqs(bd_t, nseq), u_n) + _mm(_over_seqs(kd_t, nseq), v_n), 0.0)

    mean = head_sum(y) * (1.0 / RWKV_HEAD)
    d = y - mean
    var = head_sum(d * d) * (1.0 / RWKV_HEAD)
    yn = d * lax.rsqrt(var + RWKV_GN_EPS) * stack(lw_ref[...]) + stack(lb_ref[...])
    bonus = head_sum(r * k * stack(rk_ref[...])) * v
    out = (yn + bonus) * stack(g_ref[...])
    for p in range(pb):
        y_ref[:, p * LANES:(p + 1) * LANES] = out[p]

    @pl.when(c == nchunks - 1)
    def _():
        hout_ref[...] = h_scr[...].reshape(hout_ref.shape)


def _rwkv_scan(r, k, v, w, a, g, k_k, k_a, r_k, ln_w, ln_b, h0, *, batch, seq_len, row0, sample):
    d = r.shape[1]
    npairs = d // LANES
    pb = min(RWKV_PB_SAMPLE if sample else RWKV_PB_PROMPT, npairs)
    if sample:
        nseq, seq_rows, nchunks = SEQ_GROUP, seq_len, 1
        rows = nseq * seq_rows
        grid = (batch // nseq, npairs // pb, 1)
    else:
        nseq, seq_rows, rows = 1, RWKV_CHUNK, RWKV_CHUNK
        nchunks = seq_len // rows
        grid = (batch, npairs // pb, nchunks)
    rb0 = row0 // rows
    tile = pl.BlockSpec((rows, pb * LANES), lambda b, p, c: (b * nchunks + c + rb0, p))
    vec = pl.BlockSpec((1, pb * LANES), lambda b, p, c: (0, p))
    state = pl.BlockSpec((nseq, pb, LANES, LANES), lambda b, p, c: (b, p, 0, 0))
    in_specs = [tile] * 6 + [vec] * 5
    args = [r, k, v, w, a, g] + [x.reshape(1, d) for x in (k_k, k_a, r_k, ln_w, ln_b)]
    if sample:
        in_specs.append(state)
        args.append(h0)
    return pl.pallas_call(
        functools.partial(_rwkv_scan_kernel, rows=rows, seq_rows=seq_rows, nseq=nseq, nchunks=nchunks,
                          has_h0=sample, pb=pb),
        grid=grid,
        in_specs=in_specs,
        out_specs=[pl.BlockSpec((rows, pb * LANES), lambda b, p, c: (b * nchunks + c, p)), state],
        out_shape=[jax.ShapeDtypeStruct((batch * seq_len, d), F32),
                   jax.ShapeDtypeStruct((batch, npairs, LANES, LANES), F32)],
        scratch_shapes=[pltpu.VMEM((nseq * pb, LANES, LANES), F32)],
        compiler_params=_cparams(("parallel", "parallel", "arbitrary")),
        name="rwkv_scan_sample" if sample else "rwkv_scan_prompt",
    )(*args)


def _pairs_from_heads(s):
    b, h, n, _ = s.shape
    st = jnp.swapaxes(s, -1, -2).reshape(b, h // 2, 2, n, n)
    bd = jnp.einsum("bpeij,ef->bpeifj", st, jnp.eye(2, dtype=s.dtype))
    return bd.reshape(b, h // 2, 2 * n, 2 * n)


def _heads_from_pairs(hp):
    b, p, n2, _ = hp.shape
    n = n2 // 2
    hr = hp.reshape(b, p, 2, n, 2, n)
    st = jnp.stack([hr[:, :, 0, :, 0, :], hr[:, :, 1, :, 1, :]], axis=2).reshape(b, 2 * p, n, n)
    return jnp.swapaxes(st, -1, -2)


def _pad_cols(w, n):
    return jnp.pad(w, ((0, 0), (0, n - w.shape[1])))


def kernel(x_prompt, x_sample, cache_mla_ckv, cache_mla_kpe, page_table, state_gdn_s, state_gdn_conv, state_rwkv_wkv, state_rwkv_shift, norm_w, gdn_w_in, gdn_conv_w, gdn_a_log, gdn_dt_bias, gdn_o_norm, gdn_w_out, mla_w_in, mla_q_norm, mla_w_uq, mla_kv_norm, mla_w_uk, mla_w_uv, mla_w_o, rw_mu, rw_w_rkv, rw_w0, rw_w1, rw_w2, rw_a0, rw_a1, rw_a2, rw_g1, rw_g2, rw_k_k, rw_k_a, rw_r_k, rw_ln_w, rw_ln_b, rw_w_o, mlp_w_up, mlp_w_down):
    bp, lp, d = x_prompt.shape
    bs, ls, _ = x_sample.shape
    tp, ts = bp * lp, bs * ls
    depth = norm_w.shape[0]
    n_pages = page_table.shape[1]
    past_len = n_pages * PAGE_SIZE
    tm = _row_tile(tp, ts)
    assert ls == SUBLANES and bs % SEQ_GROUP == 0 and lp % GDN_CHUNK == 0 and lp % RWKV_CHUNK == 0

    x = jnp.concatenate([x_prompt.reshape(tp, d), x_sample.reshape(ts, d)], axis=0)
    mlp_up, mlp_down = mlp_w_up.astype(BF16), mlp_w_down.astype(BF16)

    half = MLA_ROPE // 2
    inv_freq = 1.0 / (ROPE_THETA ** (jnp.arange(half, dtype=F32) / half))
    pos = jnp.concatenate([jnp.tile(jnp.arange(lp), bp), jnp.tile(past_len + jnp.arange(ls), bs)])
    ang = pos.astype(F32)[:, None] * inv_freq[None, :]
    cos, sin = jnp.cos(ang), jnp.sin(ang)
    zpad = jnp.zeros((tp + ts, LANES - MLA_ROPE), F32)
    cos2 = jnp.concatenate([cos, cos, zpad], axis=1)
    sin2 = jnp.concatenate([-sin, sin, zpad], axis=1)

    outs = {k: [] for k in ("ckv_p", "kpe_p", "ckv_s", "kpe_s", "gs_p", "gc_p", "gs_s", "gc_s",
                            "rs_p", "rx_p", "rs_s", "rx_s")}
    for i in range(depth):
        kind, j = i % 3, i // 3
        if kind == 0:
            w_in = gdn_w_in[j]
            nb = GDN_CONV_DIM + GDN_VAL_DIM
            w_ab = jnp.concatenate([w_in[:, nb + GDN_V_HEADS:], w_in[:, nb:nb + GDN_V_HEADS]], axis=1).astype(BF16)
            proj = _linear(x, gdn_w_in, wsel=j, ncols=nb, mode="norm", g=norm_w[i, 0], tm=tm)
            ab = _linear(x, w_ab, mode="norm", g=norm_w[i, 0], tm=tm)
            gates = _gdn_gates(ab, gdn_a_log[j], gdn_dt_bias[j], tm=tm)
            hist = jnp.pad(state_gdn_conv[:, j], ((0, 0), (SUBLANES - GDN_CONV + 1, 0), (0, 0)))
            hist = hist.reshape(ts, GDN_CONV_DIM)
            act_p = _gdn_conv(proj, None, gdn_conv_w[j], row0=0, rows=tp, seq_len=lp, sample=False, tm=tm)
            act_s = _gdn_conv(proj, hist, gdn_conv_w[j], row0=tp, rows=ts, seq_len=ls, sample=True, tm=tm)
            y_p, s_p = _gdn_scan(act_p, proj, gates, gdn_o_norm[j], None, j, batch=bp, seq_len=lp,
                                 row0=0, sample=False)
            earlier = tuple(outs["gs_s"]) if j == gdn_w_in.shape[0] - 1 else ()
            y_s, s_s = _gdn_scan(act_s, proj, gates, gdn_o_norm[j], state_gdn_s, j, batch=bs, seq_len=ls,
                                 row0=tp, sample=True, prev_states=earlier)
            if earlier:
                outs["gs_s"] = []
            w_out = gdn_w_out[j].astype(BF16)
            tail = GDN_CONV - 1
            conv_p = jnp.stack([lax.slice(proj, ((b + 1) * lp - tail, 0), ((b + 1) * lp, GDN_CONV_DIM))
                                for b in range(bp)])
            conv_s = lax.slice(proj, (tp, 0), (tp + ts, GDN_CONV_DIM)).reshape(bs, ls, GDN_CONV_DIM)[:, ls - tail:]
            outs["gs_p"].append(s_p)
            outs["gc_p"].append(conv_p)
            outs["gs_s"].append(s_s)
            outs["gc_s"].append(conv_s)
        elif kind == 1:
            w_in = mla_w_in[j]
            w_kpe = w_in[:, MLA_Q_RANK + MLA_KV_RANK:]
            w_c = jnp.concatenate([w_in[:, :MLA_Q_RANK + MLA_KV_RANK], _pad_cols(w_kpe, LANES),
                                   _pad_cols(jnp.roll(w_kpe, half, axis=1), LANES)], axis=1).astype(BF16)
            uq = mla_w_uq[j].reshape(MLA_Q_RANK, MLA_HEADS, MLA_QK)
            uq_pe = uq[:, :, MLA_NOPE:]
            padh = ((0, 0), (0, 0), (0, LANES - MLA_ROPE))
            w_q = jnp.concatenate([
                uq[:, :, :MLA_NOPE].reshape(MLA_Q_RANK, -1),
                jnp.pad(uq_pe, padh).reshape(MLA_Q_RANK, -1),
                jnp.pad(jnp.roll(uq_pe, half, axis=2), padh).reshape(MLA_Q_RANK, -1)], axis=1).astype(BF16)
            c = _linear(x, w_c, mode="norm", g=norm_w[i, 0], tm=tm)
            q_raw = _linear(c, w_q, mode="norm", g=mla_q_norm[j], tm=tm)
            ckv, kpe, qp = _mla_post(c, q_raw, cos2, sin2, mla_kv_norm[j], tm=min(tm, 256))
            w_kv = jnp.concatenate([mla_w_uk[j].reshape(MLA_KV_RANK, -1),
                                    mla_w_uv[j].reshape(MLA_KV_RANK, -1)], axis=1).astype(BF16)
            kv = _linear(ckv, w_kv, rows=tp, tm=tm)
            y_p = _mla_flash(q_raw, qp, kv, kpe, batch=bp, seq_len=lp)
            w_ukt = jnp.transpose(mla_w_uk[j], (1, 2, 0)).astype(BF16)
            w_uvh = jnp.transpose(mla_w_uv[j], (1, 0, 2)).astype(BF16)
            q_lat = _head_linear(q_raw, w_ukt, seq_len=ls, row0=tp, rows=ts, out_by_head=True, tm=tm)
            pad_keys = ((0, 0), (0, PAGE_SIZE - ls), (0, 0))
            ckv_new = jnp.pad(ckv[tp:].reshape(bs, ls, MLA_KV_RANK), pad_keys)
            kpe_new = jnp.pad(kpe[tp:, :MLA_ROPE].reshape(bs, ls, MLA_ROPE), pad_keys)
            o_lat = _mla_decode(q_lat.reshape(bs, MLA_HEADS * ls, MLA_KV_RANK), qp, cache_mla_ckv,
                                jnp.swapaxes(cache_mla_kpe, 2, 3), page_table, ckv_new,
                                jnp.swapaxes(kpe_new, 1, 2), j, new_len=ls)
            y_s = _head_linear(o_lat.reshape(bs, MLA_HEADS, ls, MLA_KV_RANK), w_uvh, seq_len=ls,
                               x_by_head=True, tm=tm)
            w_out = mla_w_o[j].astype(BF16)
            outs["ckv_p"].append(ckv[:tp].reshape(bp, lp, MLA_KV_RANK))
            outs["kpe_p"].append(kpe[:tp, :MLA_ROPE].reshape(bp, lp, MLA_ROPE))
            outs["ckv_s"].append(ckv[tp:].reshape(bs, ls, MLA_KV_RANK))
            outs["kpe_s"].append(kpe[tp:, :MLA_ROPE].reshape(bs, ls, MLA_ROPE))
        else:
            shift_rows = jnp.pad(state_rwkv_shift[:, j][:, None, :], ((0, 0), (0, ls - 1), (0, 0)))
            h, prev = _norm_shift(x, norm_w[i, 0], shift_rows.reshape(ts, d), tp=tp, seq_len=lp, tm=tm)
            mu = rw_mu[j]
            wr, wk, wv = (rw_w_rkv[j, s].astype(BF16) for s in range(3))
            r = _linear(h, wr, mode="mix", prev=prev, mu=mu[0], tm=tm)
            k = _linear(h, wk, mode="mix", prev=prev, mu=mu[1], tm=tm)
            v = _linear(h, wv, mode="mix", prev=prev, mu=mu[2], tm=tm)
            rank = lambda n: -(-n // LANES) * LANES
            lora_w = lambda w1, w2: (_pad_cols(w1, rank(w1.shape[1])).astype(BF16),
                                     jnp.pad(w2, ((0, rank(w2.shape[0]) - w2.shape[0]), (0, 0))).astype(BF16))
            w = _lora(h, prev, mu[3], *lora_w(rw_w1[j], rw_w2[j]), rw_w0[j], kind="decay", tm=tm)
            a = _lora(h, prev, mu[4], *lora_w(rw_a1[j], rw_a2[j]), rw_a0[j], kind="aaa", tm=tm)
            g = _lora(h, prev, mu[5], *lora_w(rw_g1[j], rw_g2[j]), jnp.zeros((d,), F32), kind="gate", tm=tm)
            params = (rw_k_k[j], rw_k_a[j], rw_r_k[j].reshape(d), rw_ln_w[j], rw_ln_b[j])
            y_p, hp_p = _rwkv_scan(r, k, v, w, a, g, *params, None, batch=bp, seq_len=lp, row0=0, sample=False)
            y_s, hp_s = _rwkv_scan(r, k, v, w, a, g, *params, _pairs_from_heads(state_rwkv_wkv[:, j]),
                                   batch=bs, seq_len=ls, row0=tp, sample=True)
            w_out = rw_w_o[j].astype(BF16)
            outs["rs_p"].append(_heads_from_pairs(hp_p))
            outs["rx_p"].append(jnp.concatenate([lax.slice(h, ((b + 1) * lp - 1, 0), ((b + 1) * lp, d))
                                                 for b in range(bp)]))
            outs["rs_s"].append(_heads_from_pairs(hp_s))
            outs["rx_s"].append(lax.slice(h, (tp, 0), (tp + ts, d)).reshape(bs, ls, d)[:, -1])
        x = _out_proj(y_p, y_s, w_out, x, norm_w[i, 1], tm=tm)
        x = _mlp(x, norm_w[i, 2], mlp_up, mlp_down, i, norm_w[i, 3], tm=tm)

    def stack(key):
        rows = outs[key]
        if key == "gs_s" and rows[0].ndim == state_gdn_s.ndim:
            return rows[0]
        return jnp.stack(rows, axis=1)

    return (x[:tp].reshape(bp, lp, d), x[tp:].reshape(bs, ls, d),
            stack("ckv_p"), stack("kpe_p"), stack("gs_p"), stack("gc_p"), stack("rs_p"), stack("rx_p"),
            stack("ckv_s"), stack("kpe_s"), stack("gs_s"), stack("gc_s"), stack("rs_s"), stack("rx_s"))
```

```python
import functools
import math

import jax
import jax.numpy as jnp
from jax import lax
from jax.experimental import pallas as pl
from jax.experimental.pallas import tpu as pltpu

F32 = jnp.float32
BF16 = jnp.bfloat16

NORM_EPS = 1e-6
L2_EPS = 1e-6
RWKV_GN_EPS = 64e-5
ROPE_THETA = 10000.0

LANES = 128
SUBLANES = 8

GDN_QK_HEADS = 16
GDN_V_HEADS = 32
GDN_HEAD = 128
GDN_CONV = 4
GDN_CHUNK = 64
GDN_KEY_DIM = GDN_QK_HEADS * GDN_HEAD
GDN_VAL_DIM = GDN_V_HEADS * GDN_HEAD
GDN_CONV_DIM = 2 * GDN_KEY_DIM + GDN_VAL_DIM
GDN_HB_PROMPT = 32
GDN_HB_SAMPLE = 4

MLA_HEADS = 16
MLA_Q_RANK = 512
MLA_KV_RANK = 512
MLA_NOPE = 128
MLA_ROPE = 64
MLA_V = 128
MLA_QK = MLA_NOPE + MLA_ROPE
PAGE_SIZE = 128
DEC_PAGES_PER_STEP = 16
DEC_PAGES_PER_CHUNK = 16
MLA_FLASH_HEADS = 2
MLA_FLASH_TILE = 1024

RWKV_HEAD = 64
RWKV_CHUNK = 64
RWKV_PB_PROMPT = 16
RWKV_PB_SAMPLE = 4
SEQ_GROUP = 8

VMEM_LIMIT = 48 * 1024 * 1024
WIDE_ROW_TILE = 1024


def _cparams(sem):
    return pltpu.CompilerParams(dimension_semantics=sem, vmem_limit_bytes=VMEM_LIMIT)


def _row_tile(*row_counts):
    for t in (512, 256, 128, 64, 32, 16, 8):
        if all(r % t == 0 for r in row_counts):
            return t
    raise ValueError(f"row counts {row_counts} are not multiples of {SUBLANES}")


def _col_tile(n):
    for t in (1024, 640, 512, 384, 256, 128):
        if n % t == 0:
            return t
    return n


def _dot(a, b):
    return jnp.dot(a.astype(BF16), b.astype(BF16), preferred_element_type=F32)


def _dot_nt(a, b):
    return lax.dot_general(a.astype(BF16), b.astype(BF16), (((1,), (1,)), ((), ())),
                           preferred_element_type=F32)


def _mm(a, b):
    if a.ndim == 2:
        return _dot(a, b)
    return jnp.einsum("bmk,bkn->bmn", a.astype(BF16), b.astype(BF16), preferred_element_type=F32)


def _mm_nt(a, b):
    if a.ndim == 2:
        return _dot_nt(a, b)
    return jnp.einsum("bmk,bnk->bmn", a.astype(BF16), b.astype(BF16), preferred_element_type=F32)


def _split3(x):
    hi = x.astype(BF16)
    r1 = x - hi.astype(F32)
    mid = r1.astype(BF16)
    lo = (r1 - mid.astype(F32)).astype(BF16)
    return hi, mid, lo


def _dot_exact_lhs(a_bf16, x):
    hi, mid, lo = _split3(x)
    return (jnp.dot(a_bf16, hi, preferred_element_type=F32)
            + jnp.dot(a_bf16, mid, preferred_element_type=F32)
            + jnp.dot(a_bf16, lo, preferred_element_type=F32))


def _dot_exact_rhs(x, b_bf16):
    hi, mid, lo = _split3(x)
    return (jnp.dot(hi, b_bf16, preferred_element_type=F32)
            + jnp.dot(mid, b_bf16, preferred_element_type=F32)
            + jnp.dot(lo, b_bf16, preferred_element_type=F32))


def _rms(x, g, eps=NORM_EPS):
    return x * lax.rsqrt(jnp.mean(x * x, axis=-1, keepdims=True) + eps) * g


def _sigmoid(x):
    return 1.0 / (1.0 + jnp.exp(-x))


def _softplus(x):
    return jnp.maximum(x, 0.0) + jnp.log(1.0 + jnp.exp(-jnp.abs(x)))


def _unit_lower_inverse(a, ii, jj, seq_rows):
    base = min(SUBLANES, seq_rows)
    sh = int(math.log2(base))
    a0 = jnp.where(jnp.right_shift(ii, sh) == jnp.right_shift(jj, sh), a, 0.0)
    x = jnp.where(ii == jj, 1.0, 0.0) - a0
    p = a0
    k = 1
    while 2 * k < base:
        p = _mm(p, p)
        x = x + _mm(x, p)
        k *= 2
    s = base
    while s < seq_rows:
        sh = int(math.log2(s))
        lower_left = jnp.logical_and(
            jnp.right_shift(ii, sh + 1) == jnp.right_shift(jj, sh + 1),
            jnp.logical_and(jnp.bitwise_and(jnp.right_shift(ii, sh), 1) == 1,
                            jnp.bitwise_and(jnp.right_shift(jj, sh), 1) == 0))
        e = jnp.where(lower_left, a, 0.0)
        x = x - _mm(_mm(x, e), x)
        s *= 2
    return x


def _linear_kernel(*refs, mode, w_is_nk):
    if mode == "norm":
        x_ref, g_ref, w_ref, o_ref, xs_ref = refs
    elif mode == "mix":
        x_ref, p_ref, mu_ref, w_ref, o_ref, xs_ref = refs
    else:
        x_ref, w_ref, o_ref, xs_ref = refs

    @pl.when(pl.program_id(1) == 0)
    def _():
        x = x_ref[...]
        if mode == "norm":
            x = _rms(x, g_ref[...])
        elif mode == "mix":
            x = x + (p_ref[...] - x) * mu_ref[...]
        xs_ref[...] = x.astype(BF16)

    if w_is_nk:
        o_ref[...] = _dot_nt(xs_ref[...], w_ref[...])
    else:
        o_ref[...] = jnp.dot(xs_ref[...], w_ref[...].astype(BF16), preferred_element_type=F32)


def _linear(x, w, *, mode="none", g=None, prev=None, mu=None, xcol=0, row0=0, rows=None, wsel=None, ncols=None, tm):
    stacked = w.ndim == 3
    k = w.shape[-1] if stacked else w.shape[0]
    n = ncols if stacked else w.shape[1]
    rows = x.shape[0] - row0 if rows is None else rows
    if stacked and rows % WIDE_ROW_TILE == 0 and row0 % WIDE_ROW_TILE == 0:
        tm = WIDE_ROW_TILE
    tn = _col_tile(n) if not stacked else next(t for t in (512, 256, 128) if n % t == 0)
    rb0 = row0 // tm
    xmap = lambda i, j: (i + rb0, xcol)
    vec = pl.BlockSpec((1, k), lambda i, j: (0, 0))
    in_specs, args = [pl.BlockSpec((tm, k), xmap)], [x]
    if mode == "norm":
        in_specs.append(vec)
        args.append(g.reshape(1, k))
    elif mode == "mix":
        in_specs += [pl.BlockSpec((tm, k), xmap), vec]
        args += [prev, mu.reshape(1, k)]
    if stacked:
        in_specs.append(pl.BlockSpec((None, tn, k), lambda i, j: (wsel, j, 0)))
    else:
        in_specs.append(pl.BlockSpec((k, tn), lambda i, j: (0, j)))
    args.append(w)
    return pl.pallas_call(
        functools.partial(_linear_kernel, mode=mode, w_is_nk=stacked),
        grid=(rows // tm, n // tn),
        in_specs=in_specs,
        out_specs=pl.BlockSpec((tm, tn), lambda i, j: (i, j)),
        out_shape=jax.ShapeDtypeStruct((rows, n), F32),
        scratch_shapes=[pltpu.VMEM((tm, k), BF16)],
        compiler_params=_cparams(("parallel", "arbitrary")),
        name=f"linear_{mode}",
    )(*args)


def _norm_shift_kernel(x_ref, x8_ref, st_ref, g_ref, h_ref, p_ref, *, tm, npb, seq_len):
    i = pl.program_id(0)
    g = g_ref[...]
    h = _rms(x_ref[...], g)
    h_ref[...] = h
    rolled = pltpu.roll(h, 1, 0)
    row = lax.broadcasted_iota(jnp.int32, h.shape, 0)
    first = lax.rem(i * tm, seq_len) == 0
    carry = jnp.where(first, 0.0, _rms(x8_ref[...], g)[SUBLANES - 1:SUBLANES])
    prev_p = jnp.where(row == 0, carry, rolled)
    prev_s = jnp.where(jnp.bitwise_and(row, SUBLANES - 1) == 0, st_ref[...], rolled)
    p_ref[...] = jnp.where(i < npb, prev_p, prev_s)


def _norm_shift(x, g, shift_rows, *, tp, seq_len, tm):
    t, d = x.shape
    npb = tp // tm
    per = tm // SUBLANES
    row = pl.BlockSpec((tm, d), lambda i: (i, 0))
    return pl.pallas_call(
        functools.partial(_norm_shift_kernel, tm=tm, npb=npb, seq_len=seq_len),
        grid=(t // tm,),
        in_specs=[row, pl.BlockSpec((SUBLANES, d), lambda i: (jnp.maximum(i * per - 1, 0), 0)),
                  pl.BlockSpec((tm, d), lambda i: (jnp.maximum(i - npb, 0), 0)),
                  pl.BlockSpec((1, d), lambda i: (0, 0))],
        out_specs=[row, row],
        out_shape=[jax.ShapeDtypeStruct((t, d), F32), jax.ShapeDtypeStruct((t, d), F32)],
        compiler_params=_cparams(("parallel",)),
        name="rms_norm_shift",
    )(x, x, shift_rows, g.reshape(1, d))


def _out_proj_kernel(yp_ref, ys_ref, w_ref, r_ref, g_ref, o_ref, acc_ref, *, npb, nk):
    i = pl.program_id(0)
    k = pl.program_id(1)

    @pl.when(k == 0)
    def _():
        acc_ref[...] = jnp.zeros_like(acc_ref)

    @pl.when(i < npb)
    def _():
        acc_ref[...] += jnp.dot(yp_ref[...].astype(BF16), w_ref[...], preferred_element_type=F32)

    @pl.when(i >= npb)
    def _():
        acc_ref[...] += jnp.dot(ys_ref[...].astype(BF16), w_ref[...], preferred_element_type=F32)

    @pl.when(k == nk - 1)
    def _():
        o_ref[...] = r_ref[...] + _rms(acc_ref[...], g_ref[...])


def _out_proj(y_p, y_s, w, resid, g, *, tm):
    kdim, d = w.shape
    tk = _col_tile(kdim)
    npb, nsb, nk = y_p.shape[0] // tm, y_s.shape[0] // tm, kdim // tk
    return pl.pallas_call(
        functools.partial(_out_proj_kernel, npb=npb, nk=nk),
        grid=(npb + nsb, nk),
        in_specs=[
            pl.BlockSpec((tm, tk), lambda i, k: (jnp.minimum(i, npb - 1), jnp.where(i < npb, k, nk - 1))),
            pl.BlockSpec((tm, tk), lambda i, k: (jnp.maximum(i - npb, 0), jnp.where(i >= npb, k, 0))),
            pl.BlockSpec((tk, d), lambda i, k: (k, 0)),
            pl.BlockSpec((tm, d), lambda i, k: (i, 0)),
            pl.BlockSpec((1, d), lambda i, k: (0, 0)),
        ],
        out_specs=pl.BlockSpec((tm, d), lambda i, k: (i, 0)),
        out_shape=jax.ShapeDtypeStruct(resid.shape, F32),
        scratch_shapes=[pltpu.VMEM((tm, d), F32)],
        compiler_params=_cparams(("parallel", "arbitrary")),
        name="out_proj_resnorm",
    )(y_p, y_s, w, resid, g.reshape(1, d))


def _mlp_kernel(x_ref, g2_ref, wu_ref, wd_ref, g3_ref, o_ref, xs_ref, *, nf):
    j = pl.program_id(1)

    @pl.when(j == 0)
    def _():
        xs_ref[...] = _rms(x_ref[...], g2_ref[...]).astype(BF16)
        o_ref[...] = jnp.zeros_like(o_ref)

    h = jnp.dot(xs_ref[...], wu_ref[...].astype(BF16), preferred_element_type=F32)
    h = jnp.square(jnp.maximum(h, 0.0))
    o_ref[...] += jnp.dot(h.astype(BF16), wd_ref[...].astype(BF16), preferred_element_type=F32)

    @pl.when(j == nf - 1)
    def _():
        o_ref[...] = x_ref[...] + _rms(o_ref[...], g3_ref[...])


def _mlp(x, g2, w_up, w_down, layer, g3, *, tm):
    t, d = x.shape
    f = w_up.shape[-1]
    tf = _col_tile(f)
    vec = pl.BlockSpec((1, d), lambda i, j: (0, 0))
    return pl.pallas_call(
        functools.partial(_mlp_kernel, nf=f // tf),
        grid=(t // tm, f // tf),
        in_specs=[pl.BlockSpec((tm, d), lambda i, j: (i, 0)), vec,
                  pl.BlockSpec((None, d, tf), lambda i, j: (layer, 0, j)),
                  pl.BlockSpec((None, tf, d), lambda i, j: (layer, j, 0)), vec],
        out_specs=pl.BlockSpec((tm, d), lambda i, j: (i, 0)),
        out_shape=jax.ShapeDtypeStruct((t, d), F32),
        scratch_shapes=[pltpu.VMEM((tm, d), BF16)],
        compiler_params=_cparams(("parallel", "arbitrary")),
        name="sq_relu_mlp",
    )(x, g2.reshape(1, d), w_up, w_down, g3.reshape(1, d))


def _head_linear_kernel(x_ref, w_ref, o_ref):
    x = x_ref[...]
    y = jnp.dot(x.reshape(-1, x.shape[-1]).astype(BF16), w_ref[...], preferred_element_type=F32)
    o_ref[...] = y.reshape(o_ref.shape)


def _head_linear(x, w, *, seq_len, row0=0, rows=None, x_by_head=False, out_by_head=False, tm):
    nh, k, n = w.shape
    rows = x.shape[0] * x.shape[2] if x_by_head else rows
    rb0 = row0 // tm
    by_head = lambda width: pl.BlockSpec((tm // seq_len, None, seq_len, width), lambda i, h: (i, h, 0, 0))
    x_spec = by_head(k) if x_by_head else pl.BlockSpec((tm, k), lambda i, h: (i + rb0, h))
    if out_by_head:
        out_spec, out_shape = by_head(n), (rows // seq_len, nh, seq_len, n)
    else:
        out_spec, out_shape = pl.BlockSpec((tm, n), lambda i, h: (i, h)), (rows, nh * n)
    return pl.pallas_call(
        _head_linear_kernel,
        grid=(rows // tm, nh),
        in_specs=[x_spec, pl.BlockSpec((None, k, n), lambda i, h: (h, 0, 0))],
        out_specs=out_spec,
        out_shape=jax.ShapeDtypeStruct(out_shape, F32),
        compiler_params=_cparams(("parallel", "parallel")),
        name="head_linear",
    )(x, w)


def _gdn_gate_kernel(x_ref, alog_ref, dtb_ref, e_ref, o_ref):
    x = x_ref[...]
    lane = lax.broadcasted_iota(jnp.int32, x.shape, 1)
    g = -jnp.exp(alog_ref[...]) * _softplus(x + dtb_ref[...])
    gb = jnp.where(lane < GDN_V_HEADS, g, _sigmoid(x))
    o_ref[...] = _dot_exact_rhs(gb, e_ref[...])


def _gdn_gates(ab, a_log, dt_bias, *, tm):
    t, n = ab.shape
    tn = 16 * LANES
    pad = jnp.zeros((GDN_V_HEADS,), F32)
    vec = pl.BlockSpec((1, n), lambda i, j: (0, 0))
    expand = jnp.repeat(jnp.eye(n, dtype=BF16), LANES, axis=1)
    return pl.pallas_call(
        _gdn_gate_kernel,
        grid=(t // tm, n * LANES // tn),
        in_specs=[pl.BlockSpec((tm, n), lambda i, j: (i, 0)), vec, vec,
                  pl.BlockSpec((n, tn), lambda i, j: (0, j))],
        out_specs=pl.BlockSpec((tm, tn), lambda i, j: (i, j)),
        out_shape=jax.ShapeDtypeStruct((t, n * LANES), F32),
        compiler_params=_cparams(("parallel", "parallel")),
        name="gdn_gates",
    )(ab, jnp.concatenate([a_log, pad]).reshape(1, n), jnp.concatenate([dt_bias, pad]).reshape(1, n), expand)


def _gdn_conv_kernel(x_ref, prev_ref, w_ref, o_ref, *, tm, tc, seq_len, nq, nqk, sample):
    i = pl.program_id(0)
    j = pl.program_id(1)
    x = x_ref[...]
    w = w_ref[...]
    tap = lambda s: w[GDN_CONV - 1 - s:GDN_CONV - s]
    is_qk = j < nqk
    qscale = jnp.where(j < nq, GDN_HEAD ** -0.5, 1.0)

    def finish(acc, rows):
        y = acc * _sigmoid(acc)
        for hh in range(tc // GDN_HEAD):
            sl = slice(hh * GDN_HEAD, (hh + 1) * GDN_HEAD)
            yh = y[:, sl]
            nrm = lax.rsqrt(jnp.sum(yh * yh, axis=-1, keepdims=True) + L2_EPS) * qscale
            o_ref[rows, sl] = (yh * jnp.where(is_qk, nrm, 1.0))[rows]

    acc = x * tap(0)
    if sample:
        hist = prev_ref[...]
        tok = jnp.bitwise_and(lax.broadcasted_iota(jnp.int32, x.shape, 0), SUBLANES - 1)
        for s in range(1, GDN_CONV):
            xs = jnp.where(tok >= s, pltpu.roll(x, s, 0), pltpu.roll(hist, tm - SUBLANES + s, 0))
            acc = acc + xs * tap(s)
        finish(acc, slice(None))
    else:
        first = lax.rem(i * tm, seq_len) == 0
        prev = jnp.where(first, 0.0, prev_ref[...])
        row = lax.broadcasted_iota(jnp.int32, prev.shape, 0)
        head_rows = slice(0, SUBLANES)
        acc8 = x[head_rows] * tap(0)
        for s in range(1, GDN_CONV):
            xs = pltpu.roll(x, s, 0)
            acc = acc + xs * tap(s)
            acc8 = acc8 + jnp.where(row < s, pltpu.roll(prev, s, 0), xs[head_rows]) * tap(s)
        if tm > SUBLANES:
            finish(acc, slice(SUBLANES, None))
        finish(acc8, head_rows)


def _gdn_conv(proj, hist, conv_w, *, row0, rows, seq_len, sample, tm):
    tc = GDN_KEY_DIM
    rb0 = row0 // tm
    if sample:
        prev_spec = pl.BlockSpec((tm, tc), lambda i, j: (i, j))
        prev = hist
    else:
        per = tm // SUBLANES
        prev_spec = pl.BlockSpec((SUBLANES, tc), lambda i, j: (jnp.maximum((i + rb0) * per - 1, 0), j))
        prev = proj
    return pl.pallas_call(
        functools.partial(_gdn_conv_kernel, tm=tm, tc=tc, seq_len=seq_len, nq=GDN_KEY_DIM // tc,
                          nqk=2 * GDN_KEY_DIM // tc, sample=sample),
        grid=(rows // tm, GDN_CONV_DIM // tc),
        in_specs=[pl.BlockSpec((tm, tc), lambda i, j: (i + rb0, j)), prev_spec,
                  pl.BlockSpec((GDN_CONV, tc), lambda i, j: (0, j))],
        out_specs=pl.BlockSpec((tm, tc), lambda i, j: (i, j)),
        out_shape=jax.ShapeDtypeStruct((rows, GDN_CONV_DIM), F32),
        compiler_params=_cparams(("parallel", "parallel")),
        name="gdn_conv_sample" if sample else "gdn_conv_prompt",
    )(proj, prev, conv_w)


def _stack_lanes(x, n):
    return jnp.stack([x[:, i * LANES:(i + 1) * LANES] for i in range(n)])


def _stack_cols(x, lo, n):
    return jnp.stack([x[:, lo + i:lo + i + 1] for i in range(n)])


def _over_seqs(x, nseq):
    return jnp.broadcast_to(x[None], (nseq,) + x.shape).reshape((nseq * x.shape[0],) + x.shape[1:])


def _chunk_masks(rows, seq_rows):
    ii = lax.broadcasted_iota(jnp.int32, (rows, rows), 0)
    jj = lax.broadcasted_iota(jnp.int32, (rows, rows), 1)
    shift = int(math.log2(seq_rows))
    same = jnp.right_shift(ii, shift) == jnp.right_shift(jj, shift)
    incl = jnp.logical_and(same, ii >= jj)
    strict = jnp.logical_and(same, ii > jj)
    last = jnp.logical_and(same, jnp.bitwise_and(jj, seq_rows - 1) == seq_rows - 1)
    return ii, jj, incl, strict, last


def _seq_mask(nseq, rows, seq_rows):
    shape = (nseq, 1, rows, 1)
    n = lax.broadcasted_iota(jnp.int32, shape, 0)
    r = lax.broadcasted_iota(jnp.int32, shape, 2)
    return jnp.right_shift(r, int(math.log2(seq_rows))) == n


def _gdn_scan_kernel(*refs, rows, seq_rows, nseq, nchunks, has_s0, hb, nprev):
    q_ref, k_ref, v_ref, z_ref, g_ref, b_ref, on_ref = refs[:7]
    s0_ref = refs[7] if has_s0 else None
    prev_refs = refs[7 + has_s0:7 + has_s0 + nprev]
    o_ref, sout_ref, s_scr = refs[7 + has_s0 + nprev:]
    c = pl.program_id(2)
    nb = nseq * hb

    @pl.when(c == 0)
    def _():
        if has_s0:
            s_scr[...] = s0_ref[...].reshape(nb, GDN_HEAD, GDN_HEAD)
        else:
            s_scr[...] = jnp.zeros_like(s_scr)

    ii, jj, incl, strict, last = _chunk_masks(rows, seq_rows)
    cum_all = _dot_exact_lhs(jnp.where(incl, 1.0, 0.0).astype(BF16), g_ref[...])
    if nseq == 1:
        tot_all = cum_all[rows - 1:rows]
    else:
        tot_all = _dot_exact_lhs(jnp.where(last, 1.0, 0.0).astype(BF16), cum_all)
    cum = _stack_lanes(cum_all, hb)
    e_cum = _stack_lanes(jnp.exp(cum_all), hb)
    e_tail = _stack_lanes(jnp.exp(tot_all - cum_all), hb)
    g_tot = _stack_lanes(jnp.exp(tot_all), hb)
    beta = _stack_lanes(b_ref[...], hb)
    cum_c = cum[:, :, :rows]
    cum_r = jnp.sum(jnp.where(ii == jj, cum_c, 0.0), axis=1, keepdims=True)
    decay = jnp.where(incl, jnp.exp(jnp.where(incl, cum_c - cum_r, 0.0)), 0.0)

    hq = hb // 2
    q2 = _stack_lanes(q_ref[...], hq)
    k2 = _stack_lanes(k_ref[...], hq)
    per_v_head = lambda x: jnp.stack([x[h // 2] for h in range(hb)])
    kk = per_v_head(_mm_nt(k2, k2))
    qk = per_v_head(_mm_nt(q2, k2))
    q = per_v_head(q2)
    k = per_v_head(k2)
    v = _stack_lanes(v_ref[...], hb)
    z = _stack_lanes(z_ref[...], hb)

    a_mat = jnp.where(strict, beta[:, :, :rows] * decay * kk, 0.0)
    t_inv = _unit_lower_inverse(a_mat, ii, jj, seq_rows)
    u_v = _mm(t_inv, beta * v)
    w_k = _mm(t_inv, (beta * e_cum) * k)
    p_qk = decay * qk
    q_g = q * e_cum
    k_d_t = jnp.swapaxes(k * e_tail, 1, 2)
    s = s_scr[...]
    sb = s.astype(BF16)
    if nseq == 1:
        u = u_v - _mm(w_k, sb)
        o = _mm(q_g, sb) + _mm(p_qk, u)
        s_scr[...] = g_tot * s + _mm(k_d_t, u)
    else:
        mine = _seq_mask(nseq, rows, seq_rows)
        pick = lambda x: jnp.sum(jnp.where(mine, x.reshape((nseq, hb) + x.shape[1:]), 0.0), axis=0)
        u = u_v - pick(_mm(_over_seqs(w_k, nseq), sb))
        o = pick(_mm(_over_seqs(q_g, nseq), sb)) + _mm(p_qk, u)
        u_n = jnp.where(mine, u[None], 0.0).reshape(nb, rows, GDN_HEAD)
        gt = jnp.stack([g_tot[:, (n + 1) * seq_rows - 1:(n + 1) * seq_rows] for n in range(nseq)])
        s_scr[...] = gt.reshape(nb, 1, GDN_HEAD) * s + _mm(_over_seqs(k_d_t, nseq), u_n)
    o = _rms(o, on_ref[...]) * (z * _sigmoid(z))
    for h in range(hb):
        o_ref[:, h * GDN_HEAD:(h + 1) * GDN_HEAD] = o[h]

    @pl.when(c == nchunks - 1)
    def _():
        new = s_scr[...].reshape(nseq, hb, GDN_HEAD, GDN_HEAD)
        if nprev:
            for l, p_ref in enumerate(prev_refs):
                sout_ref[:, l] = p_ref[...]
            sout_ref[:, nprev] = new
        else:
            sout_ref[...] = new


def _gdn_scan(act, proj, gates, o_norm, s0, layer, *, batch, seq_len, row0, sample, prev_states=()):
    hb = GDN_HB_SAMPLE if sample else GDN_HB_PROMPT
    ng = GDN_V_HEADS // hb
    hgw = hb * GDN_HEAD
    qkw = hgw // 2
    if sample:
        nseq, seq_rows, nchunks = SEQ_GROUP, seq_len, 1
        rows = nseq * seq_rows
        grid = (batch // nseq, ng, 1)
    else:
        nseq, seq_rows, rows = 1, GDN_CHUNK, GDN_CHUNK
        nchunks = seq_len // rows
        grid = (batch, ng, nchunks)
    rb0 = row0 // rows
    rmap = lambda b, g, c: b * nchunks + c
    in_specs = [
        pl.BlockSpec((rows, qkw), lambda b, g, c: (rmap(b, g, c), g)),
        pl.BlockSpec((rows, qkw), lambda b, g, c: (rmap(b, g, c), GDN_KEY_DIM // qkw + g)),
        pl.BlockSpec((rows, hgw), lambda b, g, c: (rmap(b, g, c), 2 * GDN_KEY_DIM // hgw + g)),
        pl.BlockSpec((rows, hgw), lambda b, g, c: (rmap(b, g, c) + rb0, GDN_CONV_DIM // hgw + g)),
        pl.BlockSpec((rows, hgw), lambda b, g, c: (rmap(b, g, c) + rb0, g)),
        pl.BlockSpec((rows, hgw), lambda b, g, c: (rmap(b, g, c) + rb0, ng + g)),
        pl.BlockSpec((1, GDN_HEAD), lambda b, g, c: (0, 0)),
    ]
    args = [act, act, act, proj, gates, gates, o_norm.reshape(1, GDN_HEAD)]
    if sample:
        in_specs.append(pl.BlockSpec((nseq, None, hb, GDN_HEAD, GDN_HEAD),
                                     lambda b, g, c: (b, layer, g, 0, 0)))
        args.append(s0)
    state_spec = pl.BlockSpec((nseq, hb, GDN_HEAD, GDN_HEAD), lambda b, g, c: (b, g, 0, 0))
    state_shape = jax.ShapeDtypeStruct((batch, GDN_V_HEADS, GDN_HEAD, GDN_HEAD), F32)
    nprev = len(prev_states)
    if nprev:
        in_specs += [state_spec] * nprev
        args += list(prev_states)
        state_spec = pl.BlockSpec((nseq, nprev + 1, hb, GDN_HEAD, GDN_HEAD), lambda b, g, c: (b, 0, g, 0, 0))
        state_shape = jax.ShapeDtypeStruct((batch, nprev + 1, GDN_V_HEADS, GDN_HEAD, GDN_HEAD), F32)
    return pl.pallas_call(
        functools.partial(_gdn_scan_kernel, rows=rows, seq_rows=seq_rows, nseq=nseq, nchunks=nchunks,
                          has_s0=sample, hb=hb, nprev=nprev),
        grid=grid,
        in_specs=in_specs,
        out_specs=[pl.BlockSpec((rows, hgw), lambda b, g, c: (rmap(b, g, c), g)), state_spec],
        out_shape=[jax.ShapeDtypeStruct((batch * seq_len, GDN_VAL_DIM), F32), state_shape],
        scratch_shapes=[pltpu.VMEM((nseq * hb, GDN_HEAD, GDN_HEAD), F32)],
        compiler_params=_cparams(("parallel", "parallel", "arbitrary")),
        name="gdn_scan_sample" if sample else "gdn_scan_prompt",
    )(*args)


def _mla_post_kernel(ckv_in_ref, kpe_in_ref, kpe_sw_ref, qpe_ref, qsw_ref, cos_ref, sin_ref, g_ref,
                     ckv_ref, kpe_ref, qp_ref):
    ckv_ref[...] = _rms(ckv_in_ref[...], g_ref[...])
    cos = cos_ref[...]
    sin = sin_ref[...]
    kpe_ref[...] = kpe_in_ref[...] * cos + kpe_sw_ref[...] * sin
    for h in range(MLA_HEADS):
        sl = slice(h * LANES, (h + 1) * LANES)
        qp_ref[:, sl] = qpe_ref[:, sl] * cos + qsw_ref[:, sl] * sin


def _mla_post(c, q_raw, cos2, sin2, kv_norm, *, tm):
    t = c.shape[0]
    hw = MLA_HEADS * LANES
    c0 = MLA_Q_RANK // MLA_KV_RANK
    k0 = (MLA_Q_RANK + MLA_KV_RANK) // LANES
    row = lambda w, col: pl.BlockSpec((tm, w), lambda i: (i, col))
    return pl.pallas_call(
        _mla_post_kernel,
        grid=(t // tm,),
        in_specs=[row(MLA_KV_RANK, c0), row(LANES, k0), row(LANES, k0 + 1),
                  row(hw, 1), row(hw, 2), row(LANES, 0), row(LANES, 0),
                  pl.BlockSpec((1, MLA_KV_RANK), lambda i: (0, 0))],
        out_specs=[row(MLA_KV_RANK, 0), row(LANES, 0), row(hw, 0)],
        out_shape=[jax.ShapeDtypeStruct((t, MLA_KV_RANK), F32), jax.ShapeDtypeStruct((t, LANES), F32),
                   jax.ShapeDtypeStruct((t, hw), F32)],
        compiler_params=_cparams(("parallel",)),
        name="mla_post",
    )(c, c, c, q_raw, q_raw, cos2, sin2, kv_norm.reshape(1, MLA_KV_RANK))


def _online_softmax_chunks(scores, values, nchunks, m, l, acc, mask=None):
    s_next = scores(0)
    for c in range(nchunks):
        s = s_next
        if c + 1 < nchunks:
            s_next = scores(c + 1)
        if mask is not None:
            s = mask(c, s)
        m_new = jnp.maximum(m, jnp.max(s, axis=-1, keepdims=True))
        alpha = jnp.exp(m - m_new)
        p = jnp.exp(s - m_new)
        l = alpha * l + jnp.sum(p, axis=-1, keepdims=True)
        acc = alpha * acc + jnp.dot(p.astype(BF16), values(c), preferred_element_type=F32)
        m = m_new
    return m, l, acc


def _flash_kernel(qi_ref, ki_ref, qn_ref, qp_ref, kn_ref, kp_ref, v_ref, o_ref, qs_ref, kc_ref,
                  m_ref, l_ref, acc_ref, *, scale, nh):
    t = pl.program_id(2)
    qi = qi_ref[t]
    ki = ki_ref[t]
    tq = qn_ref.shape[0]
    head = lambda e: slice(e * LANES, (e + 1) * LANES)

    @pl.when(ki == 0)
    def _():
        for e in range(nh):
            qs_ref[e, :, :LANES] = (qn_ref[:, head(e)] * scale).astype(BF16)
            qs_ref[e, :, LANES:] = (qp_ref[:, head(e)] * scale).astype(BF16)
        m_ref[...] = jnp.full_like(m_ref, -jnp.inf)
        l_ref[...] = jnp.zeros_like(l_ref)
        acc_ref[...] = jnp.zeros_like(acc_ref)

    kpb = kp_ref[...].astype(BF16)
    for e in range(nh):
        kc_ref[e, :, :LANES] = kn_ref[:, head(e)].astype(BF16)
        kc_ref[e, :, LANES:] = kpb

    def block(diagonal):
        nr = 2 if tq >= 2 * LANES else 1
        tr = tq // nr
        streams = [(e, r) for e in range(nh) for r in range(nr)]
        rows = lambda r: slice(r * tr, (r + 1) * tr)
        score = lambda e, r: _dot_nt(qs_ref[e, rows(r), :], kc_ref[e])
        ahead = 2
        pending = [score(*st) for st in streams[:ahead]]
        for i, (e, r) in enumerate(streams):
            s = pending.pop(0)
            if i + ahead < len(streams):
                pending.append(score(*streams[i + ahead]))
            if diagonal:
                row = lax.broadcasted_iota(jnp.int32, s.shape, 0) + r * tr
                col = lax.broadcasted_iota(jnp.int32, s.shape, 1)
                s = jnp.where(row >= col, s, -jnp.inf)
            m_old = m_ref[e, rows(r), :]
            m_new = jnp.maximum(m_old, jnp.max(s, axis=-1, keepdims=True))
            alpha = jnp.exp(m_old - m_new)
            p = jnp.exp(s - m_new)
            l_ref[e, rows(r), :] = alpha * l_ref[e, rows(r), :] + jnp.sum(p, axis=-1, keepdims=True)
            acc_ref[e, rows(r), :] = alpha * acc_ref[e, rows(r), :] + _dot(p, v_ref[:, head(e)])
            m_ref[e, rows(r), :] = m_new

    @pl.when(ki < qi)
    def _():
        block(False)

    @pl.when(ki == qi)
    def _():
        block(True)
        for e in range(nh):
            o_ref[:, head(e)] = acc_ref[e] / l_ref[e]


def _mla_flash(q_raw, qp, kv, kpe, *, batch, seq_len):
    tq = next(t for t in (1024, 512, 256, 128, 64) if t <= MLA_FLASH_TILE and seq_len % t == 0)
    nq = seq_len // tq
    pairs = [(qi, ki) for qi in range(nq) for ki in range(qi + 1)]
    qi_of = jnp.asarray([p[0] for p in pairs], jnp.int32)
    ki_of = jnp.asarray([p[1] for p in pairs], jnp.int32)
    nh = MLA_FLASH_HEADS
    hw = nh * LANES
    qmap = lambda b, h, t, qi, ki: (b * nq + qi[t], h)
    kmap = lambda off: (lambda b, h, t, qi, ki: (b * nq + ki[t], off + h))
    grid_spec = pltpu.PrefetchScalarGridSpec(
        num_scalar_prefetch=2,
        grid=(batch, MLA_HEADS // nh, len(pairs)),
        in_specs=[pl.BlockSpec((tq, hw), qmap), pl.BlockSpec((tq, hw), qmap),
                  pl.BlockSpec((tq, hw), kmap(0)),
                  pl.BlockSpec((tq, LANES), lambda b, h, t, qi, ki: (b * nq + ki[t], 0)),
                  pl.BlockSpec((tq, hw), kmap(MLA_HEADS // nh))],
        out_specs=pl.BlockSpec((tq, hw), qmap),
        scratch_shapes=[pltpu.VMEM((nh, tq, 2 * LANES), BF16), pltpu.VMEM((nh, tq, 2 * LANES), BF16),
                        pltpu.VMEM((nh, tq, 1), F32), pltpu.VMEM((nh, tq, 1), F32),
                        pltpu.VMEM((nh, tq, MLA_V), F32)],
    )
    return pl.pallas_call(
        functools.partial(_flash_kernel, scale=MLA_QK ** -0.5, nh=nh),
        grid_spec=grid_spec,
        out_shape=jax.ShapeDtypeStruct((batch * seq_len, MLA_HEADS * MLA_V), F32),
        compiler_params=_cparams(("parallel", "parallel", "arbitrary")),
        name="mla_flash_prompt",
    )(qi_of, ki_of, q_raw, qp, kv, kpe, kv)


def _decode_kernel(pt_ref, ql_ref, qp_ref, ckv_hbm, kpe_hbm, cnew_ref, knew_ref, o_ref,
                   qls_ref, qps_ref, m_ref, l_ref, acc_ref, kc_ref, pc_ref, kbuf, pbuf, sems,
                   *, npg, ngroups, nsteps, layer, new_len, scale):
    b = pl.program_id(0)
    g = pl.program_id(1)
    step = b * ngroups + g
    slot = jnp.bitwise_and(step, 1)

    def page_copies(bb, gg, sl):
        out = []
        for i in range(npg):
            page = pt_ref[bb, gg * npg + i]
            out.append(pltpu.make_async_copy(ckv_hbm.at[page, layer], kbuf.at[sl, i], sems.at[sl]))
            out.append(pltpu.make_async_copy(kpe_hbm.at[page, layer], pbuf.at[sl, i], sems.at[sl]))
        return out

    @pl.when(step == 0)
    def _():
        for cp in page_copies(b, g, slot):
            cp.start()

    @pl.when(step + 1 < nsteps)
    def _():
        wrap = g == ngroups - 1
        for cp in page_copies(jnp.where(wrap, b + 1, b), jnp.where(wrap, 0, g + 1), 1 - slot):
            cp.start()

    for cp in page_copies(b, g, slot):
        cp.wait()
    ckv_refs = [kbuf.at[slot, i] for i in range(npg)]
    kpe_refs = [pbuf.at[slot, i] for i in range(npg)]

    @pl.when(g == 0)
    def _():
        qls_ref[...] = (ql_ref[...] * scale).astype(BF16)
        qpe = _stack_lanes(qp_ref[...], MLA_HEADS).reshape(qps_ref.shape[0], LANES)
        qps_ref[...] = (qpe[:, :MLA_ROPE] * scale).astype(BF16)
        m_ref[...] = jnp.full_like(m_ref, -jnp.inf)
        l_ref[...] = jnp.zeros_like(l_ref)
        acc_ref[...] = jnp.zeros_like(acc_ref)

    for i in range(npg):
        sl = slice(i * PAGE_SIZE, (i + 1) * PAGE_SIZE)
        kc_ref[sl, :] = ckv_refs[i][...].astype(BF16)
        pc_ref[:, sl] = kpe_refs[i][...].astype(BF16)
    chunk_pages = math.gcd(npg, DEC_PAGES_PER_CHUNK)
    width = chunk_pages * PAGE_SIZE
    chunk = lambda c: slice(c * width, (c + 1) * width)
    scores = lambda c: _dot_nt(qls_ref[...], kc_ref[chunk(c), :]) + _dot(qps_ref[...], pc_ref[:, chunk(c)])
    values = lambda c: kc_ref[chunk(c), :]
    m, l, acc = _online_softmax_chunks(scores, values, npg // chunk_pages, m_ref[...], l_ref[...], acc_ref[...])
    m_ref[...] = m
    l_ref[...] = l
    acc_ref[...] = acc

    @pl.when(g == ngroups - 1)
    def _():
        kb = cnew_ref[...].astype(BF16)
        tq = jnp.bitwise_and(lax.broadcasted_iota(jnp.int32, (qls_ref.shape[0], kb.shape[0]), 0), new_len - 1)
        tk = lax.broadcasted_iota(jnp.int32, (qls_ref.shape[0], kb.shape[0]), 1)
        visible = jnp.logical_and(tk < new_len, tk <= tq)
        m2, l2, acc2 = _online_softmax_chunks(
            lambda c: _dot_nt(qls_ref[...], kb) + _dot(qps_ref[...], knew_ref[...]), lambda c: kb, 1,
            m, l, acc, mask=lambda c, s: jnp.where(visible, s, -jnp.inf))
        o_ref[...] = acc2 / l2


def _mla_decode(q_lat, qp, cache_ckv, cache_kpe, page_table, ckv_new, kpe_new, layer, *, new_len):
    bsz, n_pages = page_table.shape
    nrow = q_lat.shape[1]
    npg = min(DEC_PAGES_PER_STEP, n_pages)
    ngroups = n_pages // npg
    qp_rb0 = qp.shape[0] // new_len - bsz
    in_specs = [pl.BlockSpec((None, nrow, MLA_KV_RANK), lambda b, g, pt: (b, 0, 0)),
                pl.BlockSpec((new_len, MLA_HEADS * LANES), lambda b, g, pt: (qp_rb0 + b, 0)),
                pl.BlockSpec(memory_space=pl.ANY), pl.BlockSpec(memory_space=pl.ANY),
                pl.BlockSpec((None, PAGE_SIZE, MLA_KV_RANK), lambda b, g, pt: (b, 0, 0)),
                pl.BlockSpec((None, MLA_ROPE, PAGE_SIZE), lambda b, g, pt: (b, 0, 0))]
    grid_spec = pltpu.PrefetchScalarGridSpec(
        num_scalar_prefetch=1,
        grid=(bsz, ngroups),
        in_specs=in_specs,
        out_specs=pl.BlockSpec((None, nrow, MLA_KV_RANK), lambda b, g, pt: (b, 0, 0)),
        scratch_shapes=[pltpu.VMEM((nrow, MLA_KV_RANK), BF16), pltpu.VMEM((nrow, MLA_ROPE), BF16),
                        pltpu.VMEM((nrow, 1), F32), pltpu.VMEM((nrow, 1), F32),
                        pltpu.VMEM((nrow, MLA_KV_RANK), F32),
                        pltpu.VMEM((npg * PAGE_SIZE, MLA_KV_RANK), BF16),
                        pltpu.VMEM((MLA_ROPE, npg * PAGE_SIZE), BF16),
                        pltpu.VMEM((2, npg, PAGE_SIZE, MLA_KV_RANK), F32),
                        pltpu.VMEM((2, npg, MLA_ROPE, PAGE_SIZE), F32),
                        pltpu.SemaphoreType.DMA((2,))],
    )
    return pl.pallas_call(
        functools.partial(_decode_kernel, npg=npg, ngroups=ngroups, nsteps=bsz * ngroups, layer=layer,
                          new_len=new_len, scale=MLA_QK ** -0.5),
        grid_spec=grid_spec,
        out_shape=jax.ShapeDtypeStruct(q_lat.shape, F32),
        compiler_params=_cparams(("arbitrary", "arbitrary")),
        name="mla_decode",
    )(page_table, q_lat, qp, cache_ckv, cache_kpe, ckv_new, kpe_new)


def _lora_kernel(x_ref, p_ref, mu_ref, w1_ref, w2_ref, b_ref, o_ref, *, kind):
    x = x_ref[...]
    xm = x + (p_ref[...] - x) * mu_ref[...]
    t = jnp.dot(xm.astype(BF16), w1_ref[...], preferred_element_type=F32)
    if kind == "decay":
        t = jnp.tanh(t)
    elif kind == "gate":
        t = _sigmoid(t)
    y = jnp.dot(t.astype(BF16), w2_ref[...], preferred_element_type=F32)
    if kind == "decay":
        o_ref[...] = -_softplus(-(b_ref[...] + y)) - 0.5
    elif kind == "aaa":
        o_ref[...] = _sigmoid(b_ref[...] + y)
    else:
        o_ref[...] = y


def _lora(h, prev, mu, w1, w2, bias, *, kind, tm):
    t, d = h.shape
    r = w1.shape[1]
    row = pl.BlockSpec((tm, d), lambda i: (i, 0))
    vec = pl.BlockSpec((1, d), lambda i: (0, 0))
    return pl.pallas_call(
        functools.partial(_lora_kernel, kind=kind),
        grid=(t // tm,),
        in_specs=[row, row, vec, pl.BlockSpec((d, r), lambda i: (0, 0)),
                  pl.BlockSpec((r, d), lambda i: (0, 0)), vec],
        out_specs=row,
        out_shape=jax.ShapeDtypeStruct((t, d), F32),
        compiler_params=_cparams(("parallel",)),
        name=f"rwkv_lora_{kind}",
    )(h, prev, mu.reshape(1, d), w1, w2, bias.reshape(1, d))


def _rwkv_scan_kernel(*refs, rows, seq_rows, nseq, nchunks, has_h0, pb):
    r_ref, k_ref, v_ref, w_ref, a_ref, g_ref, kk_ref, ka_ref, rk_ref, lw_ref, lb_ref = refs[:11]
    if has_h0:
        h0_ref, y_ref, hout_ref, h_scr = refs[11:]
    else:
        y_ref, hout_ref, h_scr = refs[11:]
    c = pl.program_id(2)
    nb = nseq * pb

    @pl.when(c == 0)
    def _():
        if has_h0:
            h_scr[...] = h0_ref[...].reshape(nb, LANES, LANES)
        else:
            h_scr[...] = jnp.zeros_like(h_scr)

    ii, jj, incl, strict, last = _chunk_masks(rows, seq_rows)
    head0 = lax.broadcasted_iota(jnp.int32, (1, 1, LANES), 2) < RWKV_HEAD

    def head_sum(x):
        s0 = jnp.sum(jnp.where(head0, x, 0.0), axis=-1, keepdims=True)
        s1 = jnp.sum(jnp.where(head0, 0.0, x), axis=-1, keepdims=True)
        return jnp.where(head0, s0, s1)

    log_d2 = -jnp.exp(w_ref[...])
    cum2 = _dot_exact_lhs(jnp.where(incl, 1.0, 0.0).astype(BF16), log_d2)
    if nseq == 1:
        tot2 = cum2[rows - 1:rows]
    else:
        tot2 = _dot_exact_lhs(jnp.where(last, 1.0, 0.0).astype(BF16), cum2)
    stack = lambda x: _stack_lanes(x, pb)
    e_pos = stack(jnp.exp(cum2))
    e_neg = stack(jnp.exp(-cum2))
    e_prev = stack(jnp.exp(cum2 - log_d2))
    e_end = stack(jnp.exp(tot2 - cum2))
    gamma = stack(jnp.exp(tot2))

    r = stack(r_ref[...])
    k = stack(k_ref[...])
    v = stack(v_ref[...])
    a = stack(a_ref[...])
    kx = k * stack(kk_ref[...])
    kk = kx * lax.rsqrt(head_sum(kx * kx) + L2_EPS)
    k = k * (1.0 + (a - 1.0) * stack(ka_ref[...]))
    b = kk * a
    a_t = -kk * e_prev
    b_t = b * e_neg
    k_t = k * e_neg
    r_t = r * e_pos
    bd_t = jnp.swapaxes(b * e_end, 1, 2)
    kd_t = jnp.swapaxes(k * e_end, 1, 2)

    i2 = lax.broadcasted_iota(jnp.int32, (LANES, LANES), 0)
    j2 = lax.broadcasted_iota(jnp.int32, (LANES, LANES), 1)
    eye2 = i2 == j2
    block = (i2 < RWKV_HEAD) == (j2 < RWKV_HEAD)

    halves = lambda x: jnp.concatenate([jnp.where(head0, x, 0.0), jnp.where(head0, 0.0, x)], axis=0)
    twice = lambda x: jnp.concatenate([x, x], axis=0)
    join = lambda x2: jnp.where(head0, x2[:pb], x2[pb:])
    a_m = halves(a_t)
    r_m = halves(r_t)
    b2, k2, v2 = twice(b_t), twice(k_t), twice(v)
    a_ab = jnp.where(strict, _mm_nt(a_m, b2), 0.0)
    a_ak = jnp.where(strict, _mm_nt(a_m, k2), 0.0)
    p_rb = jnp.where(incl, _mm_nt(r_m, b2), 0.0)
    p_rk = jnp.where(incl, _mm_nt(r_m, k2), 0.0)
    t_inv = _unit_lower_inverse(-a_ab, ii, jj, seq_rows)
    akv = _mm(a_ak, v2)
    prkv = _mm(p_rk, v2)

    hs = h_scr[...]
    hbf = hs.astype(BF16)
    if nseq == 1:
        h2 = twice(hbf)
        u2 = _mm(t_inv, _mm(a_m, h2) + akv)
        y2 = _mm(r_m, h2) + _mm(p_rb, u2) + prkv
        u = join(u2)
        y = join(y2)
        g_col = jnp.sum(jnp.where(eye2, gamma, 0.0), axis=2, keepdims=True)
        h_scr[...] = g_col * hs + jnp.where(block, _mm(bd_t, u) + _mm(kd_t, v), 0.0)
    else:
        mine = _seq_mask(nseq, rows, seq_rows)
        h4 = hbf.reshape(nseq, pb, LANES, LANES)
        h2 = jnp.concatenate([h4, h4], axis=1).reshape(nseq * 2 * pb, LANES, LANES)
        pick = lambda x: jnp.sum(jnp.where(mine, x.reshape((nseq, 2 * pb) + x.shape[1:]), 0.0), axis=0)
        u2 = _mm(t_inv, pick(_mm(_over_seqs(a_m, nseq), h2)) + akv)
        y2 = pick(_mm(_over_seqs(r_m, nseq), h2)) + _mm(p_rb, u2) + prkv
        u = join(u2)
        y = join(y2)
        u_n = jnp.where(mine, u[None], 0.0).reshape(nb, rows, LANES)
        v_n = jnp.where(mine, v[None], 0.0).reshape(nb, rows, LANES)
        g_rows = jnp.stack([gamma[:, (n + 1) * seq_rows - 1:(n + 1) * seq_rows] for n in range(nseq)])
        g_col = jnp.sum(jnp.where(eye2, g_rows.reshape(nb, 1, LANES), 0.0), axis=2, keepdims=True)
        h_scr[...] = g_col * hs + jnp.where(
            block, _mm(_over_seqs(bd_t, nseq), u_n) + _mm(_over_seqs(kd_t, nseq), v_n), 0.0)

    mean = head_sum(y) * (1.0 / RWKV_HEAD)
    d = y - mean
    var = head_sum(d * d) * (1.0 / RWKV_HEAD)
    yn = d * lax.rsqrt(var + RWKV_GN_EPS) * stack(lw_ref[...]) + stack(lb_ref[...])
    bonus = head_sum(r * k * stack(rk_ref[...])) * v
    out = (yn + bonus) * stack(g_ref[...])
    for p in range(pb):
        y_ref[:, p * LANES:(p + 1) * LANES] = out[p]

    @pl.when(c == nchunks - 1)
    def _():
        hout_ref[...] = h_scr[...].reshape(hout_ref.shape)


def _rwkv_scan(r, k, v, w, a, g, k_k, k_a, r_k, ln_w, ln_b, h0, *, batch, seq_len, row0, sample):
    d = r.shape[1]
    npairs = d // LANES
    pb = min(RWKV_PB_SAMPLE if sample else RWKV_PB_PROMPT, npairs)
    if sample:
        nseq, seq_rows, nchunks = SEQ_GROUP, seq_len, 1
        rows = nseq * seq_rows
        grid = (batch // nseq, npairs // pb, 1)
    else:
        nseq, seq_rows, rows = 1, RWKV_CHUNK, RWKV_CHUNK
        nchunks = seq_len // rows
        grid = (batch, npairs // pb, nchunks)
    rb0 = row0 // rows
    tile = pl.BlockSpec((rows, pb * LANES), lambda b, p, c: (b * nchunks + c + rb0, p))
    vec = pl.BlockSpec((1, pb * LANES), lambda b, p, c: (0, p))
    state = pl.BlockSpec((nseq, pb, LANES, LANES), lambda b, p, c: (b, p, 0, 0))
    in_specs = [tile] * 6 + [vec] * 5
    args = [r, k, v, w, a, g] + [x.reshape(1, d) for x in (k_k, k_a, r_k, ln_w, ln_b)]
    if sample:
        in_specs.append(state)
        args.append(h0)
    return pl.pallas_call(
        functools.partial(_rwkv_scan_kernel, rows=rows, seq_rows=seq_rows, nseq=nseq, nchunks=nchunks,
                          has_h0=sample, pb=pb),
        grid=grid,
        in_specs=in_specs,
        out_specs=[pl.BlockSpec((rows, pb * LANES), lambda b, p, c: (b * nchunks + c, p)), state],
        out_shape=[jax.ShapeDtypeStruct((batch * seq_len, d), F32),
                   jax.ShapeDtypeStruct((batch, npairs, LANES, LANES), F32)],
        scratch_shapes=[pltpu.VMEM((nseq * pb, LANES, LANES), F32)],
        compiler_params=_cparams(("parallel", "parallel", "arbitrary")),
        name="rwkv_scan_sample" if sample else "rwkv_scan_prompt",
    )(*args)


def _pairs_from_heads(s):
    b, h, n, _ = s.shape
    st = jnp.swapaxes(s, -1, -2).reshape(b, h // 2, 2, n, n)
    bd = jnp.einsum("bpeij,ef->bpeifj", st, jnp.eye(2, dtype=s.dtype))
    return bd.reshape(b, h // 2, 2 * n, 2 * n)


def _heads_from_pairs(hp):
    b, p, n2, _ = hp.shape
    n = n2 // 2
    hr = hp.reshape(b, p, 2, n, 2, n)
    st = jnp.stack([hr[:, :, 0, :, 0, :], hr[:, :, 1, :, 1, :]], axis=2).reshape(b, 2 * p, n, n)
    return jnp.swapaxes(st, -1, -2)


def _pad_cols(w, n):
    return jnp.pad(w, ((0, 0), (0, n - w.shape[1])))


def kernel(x_prompt, x_sample, cache_mla_ckv, cache_mla_kpe, page_table, state_gdn_s, state_gdn_conv, state_rwkv_wkv, state_rwkv_shift, norm_w, gdn_w_in, gdn_conv_w, gdn_a_log, gdn_dt_bias, gdn_o_norm, gdn_w_out, mla_w_in, mla_q_norm, mla_w_uq, mla_kv_norm, mla_w_uk, mla_w_uv, mla_w_o, rw_mu, rw_w_rkv, rw_w0, rw_w1, rw_w2, rw_a0, rw_a1, rw_a2, rw_g1, rw_g2, rw_k_k, rw_k_a, rw_r_k, rw_ln_w, rw_ln_b, rw_w_o, mlp_w_up, mlp_w_down):
    bp, lp, d = x_prompt.shape
    bs, ls, _ = x_sample.shape
    tp, ts = bp * lp, bs * ls
    depth = norm_w.shape[0]
    n_pages = page_table.shape[1]
    past_len = n_pages * PAGE_SIZE
    tm = _row_tile(tp, ts)
    assert ls == SUBLANES and bs % SEQ_GROUP == 0 and lp % GDN_CHUNK == 0 and lp % RWKV_CHUNK == 0

    x = jnp.concatenate([x_prompt.reshape(tp, d), x_sample.reshape(ts, d)], axis=0)
    mlp_up, mlp_down = mlp_w_up.astype(BF16), mlp_w_down.astype(BF16)
    gdn_w_nk = jnp.swapaxes(gdn_w_in, 1, 2)

    half = MLA_ROPE // 2
    inv_freq = 1.0 / (ROPE_THETA ** (jnp.arange(half, dtype=F32) / half))
    pos = jnp.concatenate([jnp.tile(jnp.arange(lp), bp), jnp.tile(past_len + jnp.arange(ls), bs)])
    ang = pos.astype(F32)[:, None] * inv_freq[None, :]
    cos, sin = jnp.cos(ang), jnp.sin(ang)
    zpad = jnp.zeros((tp + ts, LANES - MLA_ROPE), F32)
    cos2 = jnp.concatenate([cos, cos, zpad], axis=1)
    sin2 = jnp.concatenate([-sin, sin, zpad], axis=1)

    outs = {k: [] for k in ("ckv_p", "kpe_p", "ckv_s", "kpe_s", "gs_p", "gc_p", "gs_s", "gc_s",
                            "rs_p", "rx_p", "rs_s", "rx_s")}
    for i in range(depth):
        kind, j = i % 3, i // 3
        if kind == 0:
            nb = GDN_CONV_DIM + GDN_VAL_DIM
            w_ab = jnp.concatenate([lax.slice(gdn_w_nk, (j, nb + GDN_V_HEADS, 0), (j + 1, nb + 2 * GDN_V_HEADS, d)),
                                    lax.slice(gdn_w_nk, (j, nb, 0), (j + 1, nb + GDN_V_HEADS, d))], axis=1)
            w_ab = w_ab[0].T.astype(BF16)
            proj = _linear(x, gdn_w_nk, wsel=j, ncols=nb, mode="norm", g=norm_w[i, 0], tm=tm)
            ab = _linear(x, w_ab, mode="norm", g=norm_w[i, 0], tm=tm)
            gates = _gdn_gates(ab, gdn_a_log[j], gdn_dt_bias[j], tm=tm)
            hist = jnp.pad(state_gdn_conv[:, j], ((0, 0), (SUBLANES - GDN_CONV + 1, 0), (0, 0)))
            hist = hist.reshape(ts, GDN_CONV_DIM)
            act_p = _gdn_conv(proj, None, gdn_conv_w[j], row0=0, rows=tp, seq_len=lp, sample=False, tm=tm)
            act_s = _gdn_conv(proj, hist, gdn_conv_w[j], row0=tp, rows=ts, seq_len=ls, sample=True, tm=tm)
            y_p, s_p = _gdn_scan(act_p, proj, gates, gdn_o_norm[j], None, j, batch=bp, seq_len=lp,
                                 row0=0, sample=False)
            earlier = tuple(outs["gs_s"]) if j == gdn_w_in.shape[0] - 1 else ()
            y_s, s_s = _gdn_scan(act_s, proj, gates, gdn_o_norm[j], state_gdn_s, j, batch=bs, seq_len=ls,
                                 row0=tp, sample=True, prev_states=earlier)
            if earlier:
                outs["gs_s"] = []
            w_out = gdn_w_out[j].astype(BF16)
            tail = GDN_CONV - 1
            conv_p = jnp.stack([lax.slice(proj, ((b + 1) * lp - tail, 0), ((b + 1) * lp, GDN_CONV_DIM))
                                for b in range(bp)])
            conv_s = lax.slice(proj, (tp, 0), (tp + ts, GDN_CONV_DIM)).reshape(bs, ls, GDN_CONV_DIM)[:, ls - tail:]
            outs["gs_p"].append(s_p)
            outs["gc_p"].append(conv_p)
            outs["gs_s"].append(s_s)
            outs["gc_s"].append(conv_s)
        elif kind == 1:
            w_in = mla_w_in[j]
            w_kpe = w_in[:, MLA_Q_RANK + MLA_KV_RANK:]
            w_c = jnp.concatenate([w_in[:, :MLA_Q_RANK + MLA_KV_RANK], _pad_cols(w_kpe, LANES),
                                   _pad_cols(jnp.roll(w_kpe, half, axis=1), LANES)], axis=1).astype(BF16)
            uq = mla_w_uq[j].reshape(MLA_Q_RANK, MLA_HEADS, MLA_QK)
            uq_pe = uq[:, :, MLA_NOPE:]
            padh = ((0, 0), (0, 0), (0, LANES - MLA_ROPE))
            w_q = jnp.concatenate([
                uq[:, :, :MLA_NOPE].reshape(MLA_Q_RANK, -1),
                jnp.pad(uq_pe, padh).reshape(MLA_Q_RANK, -1),
                jnp.pad(jnp.roll(uq_pe, half, axis=2), padh).reshape(MLA_Q_RANK, -1)], axis=1).astype(BF16)
            c = _linear(x, w_c, mode="norm", g=norm_w[i, 0], tm=tm)
            q_raw = _linear(c, w_q, mode="norm", g=mla_q_norm[j], tm=tm)
            ckv, kpe, qp = _mla_post(c, q_raw, cos2, sin2, mla_kv_norm[j], tm=min(tm, 256))
            w_kv = jnp.concatenate([mla_w_uk[j].reshape(MLA_KV_RANK, -1),
                                    mla_w_uv[j].reshape(MLA_KV_RANK, -1)], axis=1).astype(BF16)
            kv = _linear(ckv, w_kv, rows=tp, tm=tm)
            y_p = _mla_flash(q_raw, qp, kv, kpe, batch=bp, seq_len=lp)
            w_ukt = jnp.transpose(mla_w_uk[j], (1, 2, 0)).astype(BF16)
            w_uvh = jnp.transpose(mla_w_uv[j], (1, 0, 2)).astype(BF16)
            q_lat = _head_linear(q_raw, w_ukt, seq_len=ls, row0=tp, rows=ts, out_by_head=True, tm=tm)
            pad_keys = ((0, 0), (0, PAGE_SIZE - ls), (0, 0))
            ckv_new = jnp.pad(ckv[tp:].reshape(bs, ls, MLA_KV_RANK), pad_keys)
            kpe_new = jnp.pad(kpe[tp:, :MLA_ROPE].reshape(bs, ls, MLA_ROPE), pad_keys)
            o_lat = _mla_decode(q_lat.reshape(bs, MLA_HEADS * ls, MLA_KV_RANK), qp, cache_mla_ckv,
                                jnp.swapaxes(cache_mla_kpe, 2, 3), page_table, ckv_new,
                                jnp.swapaxes(kpe_new, 1, 2), j, new_len=ls)
            y_s = _head_linear(o_lat.reshape(bs, MLA_HEADS, ls, MLA_KV_RANK), w_uvh, seq_len=ls,
                               x_by_head=True, tm=tm)
            w_out = mla_w_o[j].astype(BF16)
            outs["ckv_p"].append(ckv[:tp].reshape(bp, lp, MLA_KV_RANK))
            outs["kpe_p"].append(kpe[:tp, :MLA_ROPE].reshape(bp, lp, MLA_ROPE))
            outs["ckv_s"].append(ckv[tp:].reshape(bs, ls, MLA_KV_RANK))
            outs["kpe_s"].append(kpe[tp:, :MLA_ROPE].reshape(bs, ls, MLA_ROPE))
        else:
            shift_rows = jnp.pad(state_rwkv_shift[:, j][:, None, :], ((0, 0), (0, ls - 1), (0, 0)))
            h, prev = _norm_shift(x, norm_w[i, 0], shift_rows.reshape(ts, d), tp=tp, seq_len=lp, tm=tm)
            mu = rw_mu[j]
            wr, wk, wv = (rw_w_rkv[j, s].astype(BF16) for s in range(3))
            r = _linear(h, wr, mode="mix", prev=prev, mu=mu[0], tm=tm)
            k = _linear(h, wk, mode="mix", prev=prev, mu=mu[1], tm=tm)
            v = _linear(h, wv, mode="mix", prev=prev, mu=mu[2], tm=tm)
            rank = lambda n: -(-n // LANES) * LANES
            lora_w = lambda w1, w2: (_pad_cols(w1, rank(w1.shape[1])).astype(BF16),
                                     jnp.pad(w2, ((0, rank(w2.shape[0]) - w2.shape[0]), (0, 0))).astype(BF16))
            w = _lora(h, prev, mu[3], *lora_w(rw_w1[j], rw_w2[j]), rw_w0[j], kind="decay", tm=tm)
            a = _lora(h, prev, mu[4], *lora_w(rw_a1[j], rw_a2[j]), rw_a0[j], kind="aaa", tm=tm)
            g = _lora(h, prev, mu[5], *lora_w(rw_g1[j], rw_g2[j]), jnp.zeros((d,), F32), kind="gate", tm=tm)
            params = (rw_k_k[j], rw_k_a[j], rw_r_k[j].reshape(d), rw_ln_w[j], rw_ln_b[j])
            y_p, hp_p = _rwkv_scan(r, k, v, w, a, g, *params, None, batch=bp, seq_len=lp, row0=0, sample=False)
            y_s, hp_s = _rwkv_scan(r, k, v, w, a, g, *params, _pairs_from_heads(state_rwkv_wkv[:, j]),
                                   batch=bs, seq_len=ls, row0=tp, sample=True)
            w_out = rw_w_o[j].astype(BF16)
            outs["rs_p"].append(_heads_from_pairs(hp_p))
            outs["rx_p"].append(jnp.concatenate([lax.slice(h, ((b + 1) * lp - 1, 0), ((b + 1) * lp, d))
                                                 for b in range(bp)]))
            outs["rs_s"].append(_heads_from_pairs(hp_s))
            outs["rx_s"].append(lax.slice(h, (tp, 0), (tp + ts, d)).reshape(bs, ls, d)[:, -1])
        x = _out_proj(y_p, y_s, w_out, x, norm_w[i, 1], tm=tm)
        x = _mlp(x, norm_w[i, 2], mlp_up, mlp_down, i, norm_w[i, 3], tm=tm)

    def stack(key):
        rows = outs[key]
        if key == "gs_s" and rows[0].ndim == state_gdn_s.ndim:
            return rows[0]
        return jnp.stack(rows, axis=1)

    return (x[:tp].reshape(bp, lp, d), x[tp:].reshape(bs, ls, d),
            stack("ckv_p"), stack("kpe_p"), stack("gs_p"), stack("gc_p"), stack("rs_p"), stack("rx_p"),
            stack("ckv_s"), stack("kpe_s"), stack("gs_s"), stack("gc_s"), stack("rs_s"), stack("rx_s"))
```

```python
import functools
import math

import jax
import jax.numpy as jnp
from jax import lax
from jax.experimental import pallas as pl
from jax.experimental.pallas import tpu as pltpu

F32 = jnp.float32
BF16 = jnp.bfloat16

NORM_EPS = 1e-6
L2_EPS = 1e-6
RWKV_GN_EPS = 64e-5
ROPE_THETA = 10000.0

LANES = 128
SUBLANES = 8

GDN_QK_HEADS = 16
GDN_V_HEADS = 32
GDN_HEAD = 128
GDN_CONV = 4
GDN_CHUNK = 64
GDN_KEY_DIM = GDN_QK_HEADS * GDN_HEAD
GDN_VAL_DIM = GDN_V_HEADS * GDN_HEAD
GDN_CONV_DIM = 2 * GDN_KEY_DIM + GDN_VAL_DIM
GDN_HB_PROMPT = 32
GDN_HB_SAMPLE = 4

MLA_HEADS = 16
MLA_Q_RANK = 512
MLA_KV_RANK = 512
MLA_NOPE = 128
MLA_ROPE = 64
MLA_V = 128
MLA_QK = MLA_NOPE + MLA_ROPE
PAGE_SIZE = 128
DEC_PAGES_PER_STEP = 16
DEC_PAGES_PER_CHUNK = 16
MLA_FLASH_HEADS = 2
MLA_FLASH_TILE = 1024

RWKV_HEAD = 64
RWKV_CHUNK = 64
RWKV_PB_PROMPT = 16
RWKV_PB_SAMPLE = 4
SEQ_GROUP = 8

VMEM_LIMIT = 48 * 1024 * 1024
WIDE_ROW_TILE = 1024


def _cparams(sem):
    return pltpu.CompilerParams(dimension_semantics=sem, vmem_limit_bytes=VMEM_LIMIT)


def _row_tile(*row_counts):
    for t in (512, 256, 128, 64, 32, 16, 8):
        if all(r % t == 0 for r in row_counts):
            return t
    raise ValueError(f"row counts {row_counts} are not multiples of {SUBLANES}")


def _col_tile(n):
    for t in (1024, 640, 512, 384, 256, 128):
        if n % t == 0:
            return t
    return n


def _dot(a, b):
    return jnp.dot(a.astype(BF16), b.astype(BF16), preferred_element_type=F32)


def _dot_nt(a, b):
    return lax.dot_general(a.astype(BF16), b.astype(BF16), (((1,), (1,)), ((), ())),
                           preferred_element_type=F32)


def _mm(a, b):
    if a.ndim == 2:
        return _dot(a, b)
    return jnp.einsum("bmk,bkn->bmn", a.astype(BF16), b.astype(BF16), preferred_element_type=F32)


def _mm_nt(a, b):
    if a.ndim == 2:
        return _dot_nt(a, b)
    return jnp.einsum("bmk,bnk->bmn", a.astype(BF16), b.astype(BF16), preferred_element_type=F32)


def _split3(x):
    hi = x.astype(BF16)
    r1 = x - hi.astype(F32)
    mid = r1.astype(BF16)
    lo = (r1 - mid.astype(F32)).astype(BF16)
    return hi, mid, lo


def _dot_exact_lhs(a_bf16, x):
    hi, mid, lo = _split3(x)
    return (jnp.dot(a_bf16, hi, preferred_element_type=F32)
            + jnp.dot(a_bf16, mid, preferred_element_type=F32)
            + jnp.dot(a_bf16, lo, preferred_element_type=F32))


def _dot_exact_rhs(x, b_bf16):
    hi, mid, lo = _split3(x)
    return (jnp.dot(hi, b_bf16, preferred_element_type=F32)
            + jnp.dot(mid, b_bf16, preferred_element_type=F32)
            + jnp.dot(lo, b_bf16, preferred_element_type=F32))


def _rms(x, g, eps=NORM_EPS):
    return x * lax.rsqrt(jnp.mean(x * x, axis=-1, keepdims=True) + eps) * g


def _sigmoid(x):
    return 1.0 / (1.0 + jnp.exp(-x))


def _softplus(x):
    return jnp.maximum(x, 0.0) + jnp.log(1.0 + jnp.exp(-jnp.abs(x)))


def _unit_lower_inverse(a, ii, jj, seq_rows):
    mm = _mm
    base = min(SUBLANES, seq_rows)
    sh = int(math.log2(base))
    a0 = jnp.where(jnp.right_shift(ii, sh) == jnp.right_shift(jj, sh), a, 0.0)
    x = jnp.where(ii == jj, 1.0, 0.0) - a0
    p = a0
    k = 1
    while 2 * k < base:
        p = mm(p, p)
        x = x + mm(x, p)
        k *= 2
    s = base
    while s < seq_rows:
        sh = int(math.log2(s))
        lower_left = jnp.logical_and(
            jnp.right_shift(ii, sh + 1) == jnp.right_shift(jj, sh + 1),
            jnp.logical_and(jnp.bitwise_and(jnp.right_shift(ii, sh), 1) == 1,
                            jnp.bitwise_and(jnp.right_shift(jj, sh), 1) == 0))
        e = jnp.where(lower_left, a, 0.0)
        x = x - mm(mm(x, e), x)
        s *= 2
    return x


def _linear_kernel(*refs, mode, w_is_nk):
    if mode == "norm":
        x_ref, g_ref, w_ref, o_ref, xs_ref = refs
    elif mode == "mix":
        x_ref, p_ref, mu_ref, w_ref, o_ref, xs_ref = refs
    else:
        x_ref, w_ref, o_ref, xs_ref = refs

    @pl.when(pl.program_id(1) == 0)
    def _():
        x = x_ref[...]
        if mode == "norm":
            x = _rms(x, g_ref[...])
        elif mode == "mix":
            x = x + (p_ref[...] - x) * mu_ref[...]
        xs_ref[...] = x.astype(BF16)

    if w_is_nk:
        o_ref[...] = _dot_nt(xs_ref[...], w_ref[...])
    else:
        o_ref[...] = jnp.dot(xs_ref[...], w_ref[...].astype(BF16), preferred_element_type=F32)


def _linear(x, w, *, mode="none", g=None, prev=None, mu=None, xcol=0, row0=0, rows=None, wsel=None, ncols=None,
            col0=0, tm):
    stacked = w.ndim == 3
    k = w.shape[-1] if stacked else w.shape[0]
    n = ncols if stacked else w.shape[1]
    rows = x.shape[0] - row0 if rows is None else rows
    if stacked and rows % WIDE_ROW_TILE == 0 and row0 % WIDE_ROW_TILE == 0:
        tm = WIDE_ROW_TILE
    tn = _col_tile(n) if not stacked else next(t for t in (512, 256, 128, n) if n % t == 0)
    cb0 = col0 // tn
    rb0 = row0 // tm
    xmap = lambda i, j: (i + rb0, xcol)
    vec = pl.BlockSpec((1, k), lambda i, j: (0, 0))
    in_specs, args = [pl.BlockSpec((tm, k), xmap)], [x]
    if mode == "norm":
        in_specs.append(vec)
        args.append(g.reshape(1, k))
    elif mode == "mix":
        in_specs += [pl.BlockSpec((tm, k), xmap), vec]
        args += [prev, mu.reshape(1, k)]
    if stacked:
        in_specs.append(pl.BlockSpec((None, tn, k), lambda i, j: (wsel, cb0 + j, 0)))
    else:
        in_specs.append(pl.BlockSpec((k, tn), lambda i, j: (0, j)))
    args.append(w)
    return pl.pallas_call(
        functools.partial(_linear_kernel, mode=mode, w_is_nk=stacked),
        grid=(rows // tm, n // tn),
        in_specs=in_specs,
        out_specs=pl.BlockSpec((tm, tn), lambda i, j: (i, j)),
        out_shape=jax.ShapeDtypeStruct((rows, n), F32),
        scratch_shapes=[pltpu.VMEM((tm, k), BF16)],
        compiler_params=_cparams(("parallel", "arbitrary")),
        name=f"linear_{mode}",
    )(*args)


def _norm_shift_kernel(x_ref, x8_ref, st_ref, g_ref, h_ref, p_ref, *, tm, npb, seq_len):
    i = pl.program_id(0)
    g = g_ref[...]
    h = _rms(x_ref[...], g)
    h_ref[...] = h
    rolled = pltpu.roll(h, 1, 0)
    row = lax.broadcasted_iota(jnp.int32, h.shape, 0)
    first = lax.rem(i * tm, seq_len) == 0
    carry = jnp.where(first, 0.0, _rms(x8_ref[...], g)[SUBLANES - 1:SUBLANES])
    prev_p = jnp.where(row == 0, carry, rolled)
    prev_s = jnp.where(jnp.bitwise_and(row, SUBLANES - 1) == 0, st_ref[...], rolled)
    p_ref[...] = jnp.where(i < npb, prev_p, prev_s)


def _norm_shift(x, g, shift_rows, *, tp, seq_len, tm):
    t, d = x.shape
    npb = tp // tm
    per = tm // SUBLANES
    row = pl.BlockSpec((tm, d), lambda i: (i, 0))
    return pl.pallas_call(
        functools.partial(_norm_shift_kernel, tm=tm, npb=npb, seq_len=seq_len),
        grid=(t // tm,),
        in_specs=[row, pl.BlockSpec((SUBLANES, d), lambda i: (jnp.maximum(i * per - 1, 0), 0)),
                  pl.BlockSpec((tm, d), lambda i: (jnp.maximum(i - npb, 0), 0)),
                  pl.BlockSpec((1, d), lambda i: (0, 0))],
        out_specs=[row, row],
        out_shape=[jax.ShapeDtypeStruct((t, d), F32), jax.ShapeDtypeStruct((t, d), F32)],
        compiler_params=_cparams(("parallel",)),
        name="rms_norm_shift",
    )(x, x, shift_rows, g.reshape(1, d))


def _out_proj_kernel(yp_ref, ys_ref, w_ref, r_ref, g_ref, o_ref, acc_ref, *, npb, nk):
    i = pl.program_id(0)
    k = pl.program_id(1)

    @pl.when(k == 0)
    def _():
        acc_ref[...] = jnp.zeros_like(acc_ref)

    @pl.when(i < npb)
    def _():
        acc_ref[...] += jnp.dot(yp_ref[...].astype(BF16), w_ref[...], preferred_element_type=F32)

    @pl.when(i >= npb)
    def _():
        acc_ref[...] += jnp.dot(ys_ref[...].astype(BF16), w_ref[...], preferred_element_type=F32)

    @pl.when(k == nk - 1)
    def _():
        o_ref[...] = r_ref[...] + _rms(acc_ref[...], g_ref[...])


def _out_proj(y_p, y_s, w, resid, g, *, tm):
    kdim, d = w.shape
    tk = _col_tile(kdim)
    npb, nsb, nk = y_p.shape[0] // tm, y_s.shape[0] // tm, kdim // tk
    return pl.pallas_call(
        functools.partial(_out_proj_kernel, npb=npb, nk=nk),
        grid=(npb + nsb, nk),
        in_specs=[
            pl.BlockSpec((tm, tk), lambda i, k: (jnp.minimum(i, npb - 1), jnp.where(i < npb, k, nk - 1))),
            pl.BlockSpec((tm, tk), lambda i, k: (jnp.maximum(i - npb, 0), jnp.where(i >= npb, k, 0))),
            pl.BlockSpec((tk, d), lambda i, k: (k, 0)),
            pl.BlockSpec((tm, d), lambda i, k: (i, 0)),
            pl.BlockSpec((1, d), lambda i, k: (0, 0)),
        ],
        out_specs=pl.BlockSpec((tm, d), lambda i, k: (i, 0)),
        out_shape=jax.ShapeDtypeStruct(resid.shape, F32),
        scratch_shapes=[pltpu.VMEM((tm, d), F32)],
        compiler_params=_cparams(("parallel", "arbitrary")),
        name="out_proj_resnorm",
    )(y_p, y_s, w, resid, g.reshape(1, d))


def _mlp_kernel(x_ref, g2_ref, wu_ref, wd_ref, g3_ref, o_ref, xs_ref, *, nf):
    j = pl.program_id(1)

    @pl.when(j == 0)
    def _():
        xs_ref[...] = _rms(x_ref[...], g2_ref[...]).astype(BF16)
        o_ref[...] = jnp.zeros_like(o_ref)

    h = jnp.dot(xs_ref[...], wu_ref[...].astype(BF16), preferred_element_type=F32)
    h = jnp.square(jnp.maximum(h, 0.0))
    o_ref[...] += jnp.dot(h.astype(BF16), wd_ref[...].astype(BF16), preferred_element_type=F32)

    @pl.when(j == nf - 1)
    def _():
        o_ref[...] = x_ref[...] + _rms(o_ref[...], g3_ref[...])


def _mlp(x, g2, w_up, w_down, layer, g3, *, tm):
    t, d = x.shape
    f = w_up.shape[-1]
    tf = _col_tile(f)
    vec = pl.BlockSpec((1, d), lambda i, j: (0, 0))
    return pl.pallas_call(
        functools.partial(_mlp_kernel, nf=f // tf),
        grid=(t // tm, f // tf),
        in_specs=[pl.BlockSpec((tm, d), lambda i, j: (i, 0)), vec,
                  pl.BlockSpec((None, d, tf), lambda i, j: (layer, 0, j)),
                  pl.BlockSpec((None, tf, d), lambda i, j: (layer, j, 0)), vec],
        out_specs=pl.BlockSpec((tm, d), lambda i, j: (i, 0)),
        out_shape=jax.ShapeDtypeStruct((t, d), F32),
        scratch_shapes=[pltpu.VMEM((tm, d), BF16)],
        compiler_params=_cparams(("parallel", "arbitrary")),
        name="sq_relu_mlp",
    )(x, g2.reshape(1, d), w_up, w_down, g3.reshape(1, d))


def _head_linear_kernel(x_ref, w_ref, o_ref):
    x = x_ref[...]
    y = jnp.dot(x.reshape(-1, x.shape[-1]).astype(BF16), w_ref[...], preferred_element_type=F32)
    o_ref[...] = y.reshape(o_ref.shape)


def _head_linear(x, w, *, seq_len, row0=0, rows=None, x_by_head=False, out_by_head=False, tm):
    nh, k, n = w.shape
    rows = x.shape[0] * x.shape[2] if x_by_head else rows
    rb0 = row0 // tm
    by_head = lambda width: pl.BlockSpec((tm // seq_len, None, seq_len, width), lambda i, h: (i, h, 0, 0))
    x_spec = by_head(k) if x_by_head else pl.BlockSpec((tm, k), lambda i, h: (i + rb0, h))
    if out_by_head:
        out_spec, out_shape = by_head(n), (rows // seq_len, nh, seq_len, n)
    else:
        out_spec, out_shape = pl.BlockSpec((tm, n), lambda i, h: (i, h)), (rows, nh * n)
    return pl.pallas_call(
        _head_linear_kernel,
        grid=(rows // tm, nh),
        in_specs=[x_spec, pl.BlockSpec((None, k, n), lambda i, h: (h, 0, 0))],
        out_specs=out_spec,
        out_shape=jax.ShapeDtypeStruct(out_shape, F32),
        compiler_params=_cparams(("parallel", "parallel")),
        name="head_linear",
    )(x, w)


def _gdn_gate_kernel(x_ref, alog_ref, dtb_ref, e_ref, o_ref):
    x = x_ref[...]
    lane = lax.broadcasted_iota(jnp.int32, x.shape, 1)
    g = -jnp.exp(alog_ref[...]) * _softplus(x + dtb_ref[...])
    bg = jnp.where(lane < GDN_V_HEADS, _sigmoid(x), g)
    o_ref[...] = _dot_exact_rhs(bg, e_ref[...])


def _gdn_gates(ba, a_log, dt_bias, *, tm):
    t, n = ba.shape
    tn = 16 * LANES
    pad = jnp.zeros((GDN_V_HEADS,), F32)
    vec = pl.BlockSpec((1, n), lambda i, j: (0, 0))
    expand = jnp.repeat(jnp.roll(jnp.eye(n, dtype=BF16), GDN_V_HEADS, axis=1), LANES, axis=1)
    return pl.pallas_call(
        _gdn_gate_kernel,
        grid=(t // tm, n * LANES // tn),
        in_specs=[pl.BlockSpec((tm, n), lambda i, j: (i, 0)), vec, vec,
                  pl.BlockSpec((n, tn), lambda i, j: (0, j))],
        out_specs=pl.BlockSpec((tm, tn), lambda i, j: (i, j)),
        out_shape=jax.ShapeDtypeStruct((t, n * LANES), F32),
        compiler_params=_cparams(("parallel", "parallel")),
        name="gdn_gates",
    )(ba, jnp.concatenate([pad, a_log]).reshape(1, n), jnp.concatenate([pad, dt_bias]).reshape(1, n), expand)


def _gdn_conv_kernel(x_ref, prev_ref, w_ref, o_ref, *, tm, tc, seq_len, nq, nqk, sample):
    i = pl.program_id(0)
    j = pl.program_id(1)
    x = x_ref[...]
    w = w_ref[...]
    tap = lambda s: w[GDN_CONV - 1 - s:GDN_CONV - s]
    is_qk = j < nqk
    qscale = jnp.where(j < nq, GDN_HEAD ** -0.5, 1.0)

    def finish(acc, rows):
        y = acc * _sigmoid(acc)
        for hh in range(tc // GDN_HEAD):
            sl = slice(hh * GDN_HEAD, (hh + 1) * GDN_HEAD)
            yh = y[:, sl]
            nrm = lax.rsqrt(jnp.sum(yh * yh, axis=-1, keepdims=True) + L2_EPS) * qscale
            o_ref[rows, sl] = (yh * jnp.where(is_qk, nrm, 1.0))[rows]

    acc = x * tap(0)
    if sample:
        hist = prev_ref[...]
        tok = jnp.bitwise_and(lax.broadcasted_iota(jnp.int32, x.shape, 0), SUBLANES - 1)
        for s in range(1, GDN_CONV):
            xs = jnp.where(tok >= s, pltpu.roll(x, s, 0), pltpu.roll(hist, tm - SUBLANES + s, 0))
            acc = acc + xs * tap(s)
        finish(acc, slice(None))
    else:
        first = lax.rem(i * tm, seq_len) == 0
        prev = jnp.where(first, 0.0, prev_ref[...])
        row = lax.broadcasted_iota(jnp.int32, prev.shape, 0)
        head_rows = slice(0, SUBLANES)
        acc8 = x[head_rows] * tap(0)
        for s in range(1, GDN_CONV):
            xs = pltpu.roll(x, s, 0)
            acc = acc + xs * tap(s)
            acc8 = acc8 + jnp.where(row < s, pltpu.roll(prev, s, 0), xs[head_rows]) * tap(s)
        if tm > SUBLANES:
            finish(acc, slice(SUBLANES, None))
        finish(acc8, head_rows)


def _gdn_conv(proj, hist, conv_w, *, row0, rows, seq_len, sample, tm):
    tc = GDN_KEY_DIM
    rb0 = row0 // tm
    if sample:
        prev_spec = pl.BlockSpec((tm, tc), lambda i, j: (i, j))
        prev = hist
    else:
        per = tm // SUBLANES
        prev_spec = pl.BlockSpec((SUBLANES, tc), lambda i, j: (jnp.maximum((i + rb0) * per - 1, 0), j))
        prev = proj
    return pl.pallas_call(
        functools.partial(_gdn_conv_kernel, tm=tm, tc=tc, seq_len=seq_len, nq=GDN_KEY_DIM // tc,
                          nqk=2 * GDN_KEY_DIM // tc, sample=sample),
        grid=(rows // tm, GDN_CONV_DIM // tc),
        in_specs=[pl.BlockSpec((tm, tc), lambda i, j: (i + rb0, j)), prev_spec,
                  pl.BlockSpec((GDN_CONV, tc), lambda i, j: (0, j))],
        out_specs=pl.BlockSpec((tm, tc), lambda i, j: (i, j)),
        out_shape=jax.ShapeDtypeStruct((rows, GDN_CONV_DIM), F32),
        compiler_params=_cparams(("parallel", "parallel")),
        name="gdn_conv_sample" if sample else "gdn_conv_prompt",
    )(proj, prev, conv_w)


def _stack_lanes(x, n):
    return jnp.stack([x[:, i * LANES:(i + 1) * LANES] for i in range(n)])


def _stack_cols(x, lo, n):
    return jnp.stack([x[:, lo + i:lo + i + 1] for i in range(n)])


def _over_seqs(x, nseq):
    return jnp.broadcast_to(x[None], (nseq,) + x.shape).reshape((nseq * x.shape[0],) + x.shape[1:])


def _chunk_masks(rows, seq_rows):
    ii = lax.broadcasted_iota(jnp.int32, (rows, rows), 0)
    jj = lax.broadcasted_iota(jnp.int32, (rows, rows), 1)
    shift = int(math.log2(seq_rows))
    same = jnp.right_shift(ii, shift) == jnp.right_shift(jj, shift)
    incl = jnp.logical_and(same, ii >= jj)
    strict = jnp.logical_and(same, ii > jj)
    last = jnp.logical_and(same, jnp.bitwise_and(jj, seq_rows - 1) == seq_rows - 1)
    return ii, jj, incl, strict, last


def _seq_mask(nseq, rows, seq_rows):
    shape = (nseq, 1, rows, 1)
    n = lax.broadcasted_iota(jnp.int32, shape, 0)
    r = lax.broadcasted_iota(jnp.int32, shape, 2)
    return jnp.right_shift(r, int(math.log2(seq_rows))) == n


def _gdn_scan_kernel(*refs, rows, seq_rows, nseq, nchunks, has_s0, hb, nprev):
    q_ref, k_ref, v_ref, z_ref, g_ref, b_ref, on_ref = refs[:7]
    s0_ref = refs[7] if has_s0 else None
    prev_refs = refs[7 + has_s0:7 + has_s0 + nprev]
    o_ref, sout_ref, s_scr = refs[7 + has_s0 + nprev:]
    c = pl.program_id(2)
    nb = nseq * hb

    @pl.when(c == 0)
    def _():
        if has_s0:
            s_scr[...] = s0_ref[...].reshape(nb, GDN_HEAD, GDN_HEAD)
        else:
            s_scr[...] = jnp.zeros_like(s_scr)

    ii, jj, incl, strict, last = _chunk_masks(rows, seq_rows)
    cum_all = _dot_exact_lhs(jnp.where(incl, 1.0, 0.0).astype(BF16), g_ref[...])
    if nseq == 1:
        tot_all = cum_all[rows - 1:rows]
    else:
        tot_all = _dot_exact_lhs(jnp.where(last, 1.0, 0.0).astype(BF16), cum_all)
    cum = _stack_lanes(cum_all, hb)
    e_cum = _stack_lanes(jnp.exp(cum_all), hb)
    e_tail = _stack_lanes(jnp.exp(tot_all - cum_all), hb)
    g_tot = _stack_lanes(jnp.exp(tot_all), hb)
    beta = _stack_lanes(b_ref[...], hb)
    cum_c = cum[:, :, :rows]
    cum_r = jnp.sum(jnp.where(ii == jj, cum_c, 0.0), axis=1, keepdims=True)
    decay = jnp.where(incl, jnp.exp(jnp.where(incl, cum_c - cum_r, 0.0)), 0.0)

    hq = hb // 2
    q2 = _stack_lanes(q_ref[...], hq)
    k2 = _stack_lanes(k_ref[...], hq)
    per_v_head = lambda x: jnp.stack([x[h // 2] for h in range(hb)])
    kk = per_v_head(_mm_nt(k2, k2))
    qk = per_v_head(_mm_nt(q2, k2))
    q = per_v_head(q2)
    k = per_v_head(k2)
    v = _stack_lanes(v_ref[...], hb)
    z = _stack_lanes(z_ref[...], hb)

    a_mat = jnp.where(strict, beta[:, :, :rows] * decay * kk, 0.0)
    t_inv = _unit_lower_inverse(a_mat, ii, jj, seq_rows)
    u_v = _mm(t_inv, beta * v)
    w_k = _mm(t_inv, (beta * e_cum) * k)
    p_qk = decay * qk
    q_g = q * e_cum
    k_d_t = jnp.swapaxes(k * e_tail, 1, 2)
    s = s_scr[...]
    sb = s.astype(BF16)
    if nseq == 1:
        u = u_v - _mm(w_k, sb)
        o = _mm(q_g, sb) + _mm(p_qk, u)
        s_scr[...] = g_tot * s + _mm(k_d_t, u)
    else:
        mine = _seq_mask(nseq, rows, seq_rows)
        pick = lambda x: jnp.sum(jnp.where(mine, x.reshape((nseq, hb) + x.shape[1:]), 0.0), axis=0)
        u = u_v - pick(_mm(_over_seqs(w_k, nseq), sb))
        o = pick(_mm(_over_seqs(q_g, nseq), sb)) + _mm(p_qk, u)
        u_n = jnp.where(mine, u[None], 0.0).reshape(nb, rows, GDN_HEAD)
        gt = jnp.stack([g_tot[:, (n + 1) * seq_rows - 1:(n + 1) * seq_rows] for n in range(nseq)])
        s_scr[...] = gt.reshape(nb, 1, GDN_HEAD) * s + _mm(_over_seqs(k_d_t, nseq), u_n)
    o = _rms(o, on_ref[...]) * (z * _sigmoid(z))
    for h in range(hb):
        o_ref[:, h * GDN_HEAD:(h + 1) * GDN_HEAD] = o[h]

    @pl.when(c == nchunks - 1)
    def _():
        new = s_scr[...].reshape(nseq, hb, GDN_HEAD, GDN_HEAD)
        if nprev:
            for l, p_ref in enumerate(prev_refs):
                sout_ref[:, l] = p_ref[...]
            sout_ref[:, nprev] = new
        else:
            sout_ref[...] = new


def _gdn_scan(act, proj, gates, o_norm, s0, layer, *, batch, seq_len, row0, sample, prev_states=()):
    hb = GDN_HB_SAMPLE if sample else GDN_HB_PROMPT
    ng = GDN_V_HEADS // hb
    hgw = hb * GDN_HEAD
    qkw = hgw // 2
    if sample:
        nseq, seq_rows, nchunks = SEQ_GROUP, seq_len, 1
        rows = nseq * seq_rows
        grid = (batch // nseq, ng, 1)
    else:
        nseq, seq_rows, rows = 1, GDN_CHUNK, GDN_CHUNK
        nchunks = seq_len // rows
        grid = (batch, ng, nchunks)
    rb0 = row0 // rows
    rmap = lambda b, g, c: b * nchunks + c
    in_specs = [
        pl.BlockSpec((rows, qkw), lambda b, g, c: (rmap(b, g, c), g)),
        pl.BlockSpec((rows, qkw), lambda b, g, c: (rmap(b, g, c), GDN_KEY_DIM // qkw + g)),
        pl.BlockSpec((rows, hgw), lambda b, g, c: (rmap(b, g, c), 2 * GDN_KEY_DIM // hgw + g)),
        pl.BlockSpec((rows, hgw), lambda b, g, c: (rmap(b, g, c) + rb0, GDN_CONV_DIM // hgw + g)),
        pl.BlockSpec((rows, hgw), lambda b, g, c: (rmap(b, g, c) + rb0, g)),
        pl.BlockSpec((rows, hgw), lambda b, g, c: (rmap(b, g, c) + rb0, ng + g)),
        pl.BlockSpec((1, GDN_HEAD), lambda b, g, c: (0, 0)),
    ]
    args = [act, act, act, proj, gates, gates, o_norm.reshape(1, GDN_HEAD)]
    if sample:
        in_specs.append(pl.BlockSpec((nseq, None, hb, GDN_HEAD, GDN_HEAD),
                                     lambda b, g, c: (b, layer, g, 0, 0)))
        args.append(s0)
    state_spec = pl.BlockSpec((nseq, hb, GDN_HEAD, GDN_HEAD), lambda b, g, c: (b, g, 0, 0))
    state_shape = jax.ShapeDtypeStruct((batch, GDN_V_HEADS, GDN_HEAD, GDN_HEAD), F32)
    nprev = len(prev_states)
    if nprev:
        in_specs += [state_spec] * nprev
        args += list(prev_states)
        state_spec = pl.BlockSpec((nseq, nprev + 1, hb, GDN_HEAD, GDN_HEAD), lambda b, g, c: (b, 0, g, 0, 0))
        state_shape = jax.ShapeDtypeStruct((batch, nprev + 1, GDN_V_HEADS, GDN_HEAD, GDN_HEAD), F32)
    return pl.pallas_call(
        functools.partial(_gdn_scan_kernel, rows=rows, seq_rows=seq_rows, nseq=nseq, nchunks=nchunks,
                          has_s0=sample, hb=hb, nprev=nprev),
        grid=grid,
        in_specs=in_specs,
        out_specs=[pl.BlockSpec((rows, hgw), lambda b, g, c: (rmap(b, g, c), g)), state_spec],
        out_shape=[jax.ShapeDtypeStruct((batch * seq_len, GDN_VAL_DIM), F32), state_shape],
        scratch_shapes=[pltpu.VMEM((nseq * hb, GDN_HEAD, GDN_HEAD), F32)],
        compiler_params=_cparams(("parallel", "parallel", "arbitrary")),
        name="gdn_scan_sample" if sample else "gdn_scan_prompt",
    )(*args)


def _mla_post_kernel(ckv_in_ref, kpe_in_ref, kpe_sw_ref, qpe_ref, qsw_ref, cos_ref, sin_ref, g_ref,
                     ckv_ref, kpe_ref, qp_ref):
    ckv_ref[...] = _rms(ckv_in_ref[...], g_ref[...])
    cos = cos_ref[...]
    sin = sin_ref[...]
    kpe_ref[...] = kpe_in_ref[...] * cos + kpe_sw_ref[...] * sin
    for h in range(MLA_HEADS):
        sl = slice(h * LANES, (h + 1) * LANES)
        qp_ref[:, sl] = qpe_ref[:, sl] * cos + qsw_ref[:, sl] * sin


def _mla_post(c, q_raw, cos2, sin2, kv_norm, *, tm):
    t = c.shape[0]
    hw = MLA_HEADS * LANES
    c0 = MLA_Q_RANK // MLA_KV_RANK
    k0 = (MLA_Q_RANK + MLA_KV_RANK) // LANES
    row = lambda w, col: pl.BlockSpec((tm, w), lambda i: (i, col))
    return pl.pallas_call(
        _mla_post_kernel,
        grid=(t // tm,),
        in_specs=[row(MLA_KV_RANK, c0), row(LANES, k0), row(LANES, k0 + 1),
                  row(hw, 1), row(hw, 2), row(LANES, 0), row(LANES, 0),
                  pl.BlockSpec((1, MLA_KV_RANK), lambda i: (0, 0))],
        out_specs=[row(MLA_KV_RANK, 0), row(LANES, 0), row(hw, 0)],
        out_shape=[jax.ShapeDtypeStruct((t, MLA_KV_RANK), F32), jax.ShapeDtypeStruct((t, LANES), F32),
                   jax.ShapeDtypeStruct((t, hw), F32)],
        compiler_params=_cparams(("parallel",)),
        name="mla_post",
    )(c, c, c, q_raw, q_raw, cos2, sin2, kv_norm.reshape(1, MLA_KV_RANK))


def _online_softmax_chunks(scores, values, nchunks, m, l, acc, mask=None):
    s_next = scores(0)
    for c in range(nchunks):
        s = s_next
        if c + 1 < nchunks:
            s_next = scores(c + 1)
        if mask is not None:
            s = mask(c, s)
        m_new = jnp.maximum(m, jnp.max(s, axis=-1, keepdims=True))
        alpha = jnp.exp(m - m_new)
        p = jnp.exp(s - m_new)
        l = alpha * l + jnp.sum(p, axis=-1, keepdims=True)
        acc = alpha * acc + jnp.dot(p.astype(BF16), values(c), preferred_element_type=F32)
        m = m_new
    return m, l, acc


def _flash_kernel(qi_ref, ki_ref, qn_ref, qp_ref, kn_ref, kp_ref, v_ref, o_ref, qs_ref, kc_ref,
                  m_ref, l_ref, acc_ref, *, scale, nh):
    t = pl.program_id(2)
    qi = qi_ref[t]
    ki = ki_ref[t]
    tq = qn_ref.shape[0]
    head = lambda e: slice(e * LANES, (e + 1) * LANES)

    @pl.when(ki == 0)
    def _():
        for e in range(nh):
            qs_ref[e, :, :LANES] = (qn_ref[:, head(e)] * scale).astype(BF16)
            qs_ref[e, :, LANES:] = (qp_ref[:, head(e)] * scale).astype(BF16)
        m_ref[...] = jnp.full_like(m_ref, -jnp.inf)
        l_ref[...] = jnp.zeros_like(l_ref)
        acc_ref[...] = jnp.zeros_like(acc_ref)

    kpb = kp_ref[...].astype(BF16)
    for e in range(nh):
        kc_ref[e, :, :LANES] = kn_ref[:, head(e)].astype(BF16)
        kc_ref[e, :, LANES:] = kpb

    def block(diagonal):
        nr = 2 if tq >= 2 * LANES else 1
        tr = tq // nr
        streams = [(e, r) for e in range(nh) for r in range(nr)]
        rows = lambda r: slice(r * tr, (r + 1) * tr)
        score = lambda e, r: _dot_nt(qs_ref[e, rows(r), :], kc_ref[e])
        ahead = 2
        pending = [score(*st) for st in streams[:ahead]]
        for i, (e, r) in enumerate(streams):
            s = pending.pop(0)
            if i + ahead < len(streams):
                pending.append(score(*streams[i + ahead]))
            if diagonal:
                row = lax.broadcasted_iota(jnp.int32, s.shape, 0) + r * tr
                col = lax.broadcasted_iota(jnp.int32, s.shape, 1)
                s = jnp.where(row >= col, s, -jnp.inf)
            m_old = m_ref[e, rows(r), :]
            m_new = jnp.maximum(m_old, jnp.max(s, axis=-1, keepdims=True))
            alpha = jnp.exp(m_old - m_new)
            p = jnp.exp(s - m_new)
            l_ref[e, rows(r), :] = alpha * l_ref[e, rows(r), :] + jnp.sum(p, axis=-1, keepdims=True)
            acc_ref[e, rows(r), :] = alpha * acc_ref[e, rows(r), :] + _dot(p, v_ref[:, head(e)])
            m_ref[e, rows(r), :] = m_new

    @pl.when(ki < qi)
    def _():
        block(False)

    @pl.when(ki == qi)
    def _():
        block(True)
        for e in range(nh):
            o_ref[:, head(e)] = acc_ref[e] / l_ref[e]


def _mla_flash(q_raw, qp, kv, kpe, *, batch, seq_len):
    tq = next(t for t in (1024, 512, 256, 128, 64) if t <= MLA_FLASH_TILE and seq_len % t == 0)
    nq = seq_len // tq
    pairs = [(qi, ki) for qi in range(nq) for ki in range(qi + 1)]
    qi_of = jnp.asarray([p[0] for p in pairs], jnp.int32)
    ki_of = jnp.asarray([p[1] for p in pairs], jnp.int32)
    nh = MLA_FLASH_HEADS
    hw = nh * LANES
    qmap = lambda b, h, t, qi, ki: (b * nq + qi[t], h)
    kmap = lambda off: (lambda b, h, t, qi, ki: (b * nq + ki[t], off + h))
    grid_spec = pltpu.PrefetchScalarGridSpec(
        num_scalar_prefetch=2,
        grid=(batch, MLA_HEADS // nh, len(pairs)),
        in_specs=[pl.BlockSpec((tq, hw), qmap), pl.BlockSpec((tq, hw), qmap),
                  pl.BlockSpec((tq, hw), kmap(0)),
                  pl.BlockSpec((tq, LANES), lambda b, h, t, qi, ki: (b * nq + ki[t], 0)),
                  pl.BlockSpec((tq, hw), kmap(MLA_HEADS // nh))],
        out_specs=pl.BlockSpec((tq, hw), qmap),
        scratch_shapes=[pltpu.VMEM((nh, tq, 2 * LANES), BF16), pltpu.VMEM((nh, tq, 2 * LANES), BF16),
                        pltpu.VMEM((nh, tq, 1), F32), pltpu.VMEM((nh, tq, 1), F32),
                        pltpu.VMEM((nh, tq, MLA_V), F32)],
    )
    return pl.pallas_call(
        functools.partial(_flash_kernel, scale=MLA_QK ** -0.5, nh=nh),
        grid_spec=grid_spec,
        out_shape=jax.ShapeDtypeStruct((batch * seq_len, MLA_HEADS * MLA_V), F32),
        compiler_params=_cparams(("parallel", "parallel", "arbitrary")),
        name="mla_flash_prompt",
    )(qi_of, ki_of, q_raw, qp, kv, kpe, kv)


def _decode_kernel(pt_ref, ql_ref, qp_ref, ckv_hbm, kpe_hbm, cnew_ref, knew_ref, o_ref,
                   qls_ref, qps_ref, m_ref, l_ref, acc_ref, kc_ref, pc_ref, kbuf, pbuf, sems,
                   *, npg, ngroups, nsteps, layer, new_len, scale):
    b = pl.program_id(0)
    g = pl.program_id(1)
    step = b * ngroups + g
    slot = jnp.bitwise_and(step, 1)

    def page_copies(bb, gg, sl):
        out = []
        for i in range(npg):
            page = pt_ref[bb, gg * npg + i]
            out.append(pltpu.make_async_copy(ckv_hbm.at[page, layer], kbuf.at[sl, i], sems.at[sl]))
            out.append(pltpu.make_async_copy(kpe_hbm.at[page, layer], pbuf.at[sl, i], sems.at[sl]))
        return out

    def start_all(copies):
        for n, cp in enumerate(copies):
            cp.start(priority=(n // 2) % 2)

    @pl.when(step == 0)
    def _():
        start_all(page_copies(b, g, slot))

    @pl.when(step + 1 < nsteps)
    def _():
        wrap = g == ngroups - 1
        start_all(page_copies(jnp.where(wrap, b + 1, b), jnp.where(wrap, 0, g + 1), 1 - slot))

    for cp in page_copies(b, g, slot):
        cp.wait()
    ckv_refs = [kbuf.at[slot, i] for i in range(npg)]
    kpe_refs = [pbuf.at[slot, i] for i in range(npg)]

    @pl.when(g == 0)
    def _():
        qls_ref[...] = (ql_ref[...] * scale).astype(BF16)
        qpe = _stack_lanes(qp_ref[...], MLA_HEADS).reshape(qps_ref.shape[0], LANES)
        qps_ref[...] = (qpe[:, :MLA_ROPE] * scale).astype(BF16)
        m_ref[...] = jnp.full_like(m_ref, -jnp.inf)
        l_ref[...] = jnp.zeros_like(l_ref)
        acc_ref[...] = jnp.zeros_like(acc_ref)

    for i in range(npg):
        sl = slice(i * PAGE_SIZE, (i + 1) * PAGE_SIZE)
        kc_ref[sl, :] = ckv_refs[i][...].astype(BF16)
        pc_ref[:, sl] = kpe_refs[i][...].astype(BF16)
    chunk_pages = math.gcd(npg, DEC_PAGES_PER_CHUNK)
    width = chunk_pages * PAGE_SIZE
    chunk = lambda c: slice(c * width, (c + 1) * width)
    scores = lambda c: _dot_nt(qls_ref[...], kc_ref[chunk(c), :]) + _dot(qps_ref[...], pc_ref[:, chunk(c)])
    values = lambda c: kc_ref[chunk(c), :]
    m, l, acc = _online_softmax_chunks(scores, values, npg // chunk_pages, m_ref[...], l_ref[...], acc_ref[...])
    m_ref[...] = m
    l_ref[...] = l
    acc_ref[...] = acc

    @pl.when(g == ngroups - 1)
    def _():
        kb = cnew_ref[...].astype(BF16)
        tq = jnp.bitwise_and(lax.broadcasted_iota(jnp.int32, (qls_ref.shape[0], kb.shape[0]), 0), new_len - 1)
        tk = lax.broadcasted_iota(jnp.int32, (qls_ref.shape[0], kb.shape[0]), 1)
        visible = jnp.logical_and(tk < new_len, tk <= tq)
        m2, l2, acc2 = _online_softmax_chunks(
            lambda c: _dot_nt(qls_ref[...], kb) + _dot(qps_ref[...], knew_ref[...]), lambda c: kb, 1,
            m, l, acc, mask=lambda c, s: jnp.where(visible, s, -jnp.inf))
        o_ref[...] = acc2 / l2


def _mla_decode(q_lat, qp, cache_ckv, cache_kpe, page_table, ckv_new, kpe_new, layer, *, new_len):
    bsz, n_pages = page_table.shape
    nrow = q_lat.shape[1]
    npg = min(DEC_PAGES_PER_STEP, n_pages)
    ngroups = n_pages // npg
    qp_rb0 = qp.shape[0] // new_len - bsz
    in_specs = [pl.BlockSpec((None, nrow, MLA_KV_RANK), lambda b, g, pt: (b, 0, 0)),
                pl.BlockSpec((new_len, MLA_HEADS * LANES), lambda b, g, pt: (qp_rb0 + b, 0)),
                pl.BlockSpec(memory_space=pl.ANY), pl.BlockSpec(memory_space=pl.ANY),
                pl.BlockSpec((None, PAGE_SIZE, MLA_KV_RANK), lambda b, g, pt: (b, 0, 0)),
                pl.BlockSpec((None, MLA_ROPE, PAGE_SIZE), lambda b, g, pt: (b, 0, 0))]
    grid_spec = pltpu.PrefetchScalarGridSpec(
        num_scalar_prefetch=1,
        grid=(bsz, ngroups),
        in_specs=in_specs,
        out_specs=pl.BlockSpec((None, nrow, MLA_KV_RANK), lambda b, g, pt: (b, 0, 0)),
        scratch_shapes=[pltpu.VMEM((nrow, MLA_KV_RANK), BF16), pltpu.VMEM((nrow, MLA_ROPE), BF16),
                        pltpu.VMEM((nrow, 1), F32), pltpu.VMEM((nrow, 1), F32),
                        pltpu.VMEM((nrow, MLA_KV_RANK), F32),
                        pltpu.VMEM((npg * PAGE_SIZE, MLA_KV_RANK), BF16),
                        pltpu.VMEM((MLA_ROPE, npg * PAGE_SIZE), BF16),
                        pltpu.VMEM((2, npg, PAGE_SIZE, MLA_KV_RANK), F32),
                        pltpu.VMEM((2, npg, MLA_ROPE, PAGE_SIZE), F32),
                        pltpu.SemaphoreType.DMA((2,))],
    )
    return pl.pallas_call(
        functools.partial(_decode_kernel, npg=npg, ngroups=ngroups, nsteps=bsz * ngroups, layer=layer,
                          new_len=new_len, scale=MLA_QK ** -0.5),
        grid_spec=grid_spec,
        out_shape=jax.ShapeDtypeStruct(q_lat.shape, F32),
        compiler_params=_cparams(("arbitrary", "arbitrary")),
        name="mla_decode",
    )(page_table, q_lat, qp, cache_ckv, cache_kpe, ckv_new, kpe_new)


def _lora_kernel(x_ref, p_ref, mu_ref, w1_ref, w2_ref, b_ref, o_ref, *, kind):
    x = x_ref[...]
    xm = x + (p_ref[...] - x) * mu_ref[...]
    t = jnp.dot(xm.astype(BF16), w1_ref[...], preferred_element_type=F32)
    if kind == "decay":
        t = jnp.tanh(t)
    elif kind == "gate":
        t = _sigmoid(t)
    y = jnp.dot(t.astype(BF16), w2_ref[...], preferred_element_type=F32)
    if kind == "decay":
        o_ref[...] = -_softplus(-(b_ref[...] + y)) - 0.5
    elif kind == "aaa":
        o_ref[...] = _sigmoid(b_ref[...] + y)
    else:
        o_ref[...] = y


def _lora(h, prev, mu, w1, w2, bias, *, kind, tm):
    t, d = h.shape
    r = w1.shape[1]
    row = pl.BlockSpec((tm, d), lambda i: (i, 0))
    vec = pl.BlockSpec((1, d), lambda i: (0, 0))
    return pl.pallas_call(
        functools.partial(_lora_kernel, kind=kind),
        grid=(t // tm,),
        in_specs=[row, row, vec, pl.BlockSpec((d, r), lambda i: (0, 0)),
                  pl.BlockSpec((r, d), lambda i: (0, 0)), vec],
        out_specs=row,
        out_shape=jax.ShapeDtypeStruct((t, d), F32),
        compiler_params=_cparams(("parallel",)),
        name=f"rwkv_lora_{kind}",
    )(h, prev, mu.reshape(1, d), w1, w2, bias.reshape(1, d))


def _rwkv_scan_kernel(*refs, rows, seq_rows, nseq, nchunks, has_h0, pb):
    r_ref, k_ref, v_ref, w_ref, a_ref, g_ref, kk_ref, ka_ref, rk_ref, lw_ref, lb_ref = refs[:11]
    if has_h0:
        h0_ref, y_ref, hout_ref, h_scr = refs[11:]
    else:
        y_ref, hout_ref, h_scr = refs[11:]
    c = pl.program_id(2)
    nb = nseq * pb

    @pl.when(c == 0)
    def _():
        if has_h0:
            h_scr[...] = h0_ref[...].reshape(nb, LANES, LANES)
        else:
            h_scr[...] = jnp.zeros_like(h_scr)

    ii, jj, incl, strict, last = _chunk_masks(rows, seq_rows)
    head0 = lax.broadcasted_iota(jnp.int32, (1, 1, LANES), 2) < RWKV_HEAD

    def head_sum(x):
        s0 = jnp.sum(jnp.where(head0, x, 0.0), axis=-1, keepdims=True)
        s1 = jnp.sum(jnp.where(head0, 0.0, x), axis=-1, keepdims=True)
        return jnp.where(head0, s0, s1)

    log_d2 = -jnp.exp(w_ref[...])
    cum2 = _dot_exact_lhs(jnp.where(incl, 1.0, 0.0).astype(BF16), log_d2)
    if nseq == 1:
        tot2 = cum2[rows - 1:rows]
    else:
        tot2 = _dot_exact_lhs(jnp.where(last, 1.0, 0.0).astype(BF16), cum2)
    stack = lambda x: _stack_lanes(x, pb)
    e_pos = stack(jnp.exp(cum2))
    e_neg = stack(jnp.exp(-cum2))
    e_prev = stack(jnp.exp(cum2 - log_d2))
    e_end = stack(jnp.exp(tot2 - cum2))
    gamma = stack(jnp.exp(tot2))

    r = stack(r_ref[...])
    k = stack(k_ref[...])
    v = stack(v_ref[...])
    a = stack(a_ref[...])
    kx = k * stack(kk_ref[...])
    kk = kx * lax.rsqrt(head_sum(kx * kx) + L2_EPS)
    k = k * (1.0 + (a - 1.0) * stack(ka_ref[...]))
    b = kk * a
    a_t = -kk * e_prev
    b_t = b * e_neg
    k_t = k * e_neg
    r_t = r * e_pos
    bd_t = jnp.swapaxes(b * e_end, 1, 2)
    kd_t = jnp.swapaxes(k * e_end, 1, 2)

    i2 = lax.broadcasted_iota(jnp.int32, (LANES, LANES), 0)
    j2 = lax.broadcasted_iota(jnp.int32, (LANES, LANES), 1)
    eye2 = i2 == j2
    block = (i2 < RWKV_HEAD) == (j2 < RWKV_HEAD)

    halves = lambda x: jnp.concatenate([jnp.where(head0, x, 0.0), jnp.where(head0, 0.0, x)], axis=0)
    twice = lambda x: jnp.concatenate([x, x], axis=0)
    join = lambda x2: jnp.where(head0, x2[:pb], x2[pb:])
    a_m = halves(a_t)
    r_m = halves(r_t)
    b2, k2, v2 = twice(b_t), twice(k_t), twice(v)
    a_ab = jnp.where(strict, _mm_nt(a_m, b2), 0.0)
    a_ak = jnp.where(strict, _mm_nt(a_m, k2), 0.0)
    p_rb = jnp.where(incl, _mm_nt(r_m, b2), 0.0)
    p_rk = jnp.where(incl, _mm_nt(r_m, k2), 0.0)
    t_inv = _unit_lower_inverse(-a_ab, ii, jj, seq_rows)
    akv = _mm(a_ak, v2)
    prkv = _mm(p_rk, v2)

    hs = h_scr[...]
    hbf = hs.astype(BF16)
    if nseq == 1:
        h2 = twice(hbf)
        u2 = _mm(t_inv, _mm(a_m, h2) + akv)
        y2 = _mm(r_m, h2) + _mm(p_rb, u2) + prkv
        u = join(u2)
        y = join(y2)
        g_col = jnp.sum(jnp.where(eye2, gamma, 0.0), axis=2, keepdims=True)
        h_scr[...] = g_col * hs + jnp.where(block, _mm(bd_t, u) + _mm(kd_t, v), 0.0)
    else:
        mine = _seq_mask(nseq, rows, seq_rows)
        h4 = hbf.reshape(nseq, pb, LANES, LANES)
        h2 = jnp.concatenate([h4, h4], axis=1).reshape(nseq * 2 * pb, LANES, LANES)
        pick = lambda x: jnp.sum(jnp.where(mine, x.reshape((nseq, 2 * pb) + x.shape[1:]), 0.0), axis=0)
        u2 = _mm(t_inv, pick(_mm(_over_seqs(a_m, nseq), h2)) + akv)
        y2 = pick(_mm(_over_seqs(r_m, nseq), h2)) + _mm(p_rb, u2) + prkv
        u = join(u2)
        y = join(y2)
        u_n = jnp.where(mine, u[None], 0.0).reshape(nb, rows, LANES)
        v_n = jnp.where(mine, v[None], 0.0).reshape(nb, rows, LANES)
        g_rows = jnp.stack([gamma[:, (n + 1) * seq_rows - 1:(n + 1) * seq_rows] for n in range(nseq)])
        g_col = jnp.sum(jnp.where(eye2, g_rows.reshape(nb, 1, LANES), 0.0), axis=2, keepdims=True)
        h_scr[...] = g_col * hs + jnp.where(
            block, _mm(_over_seqs(bd_t, nseq), u_n) + _mm(_over_seqs(kd_t, nseq), v_n), 0.0)

    mean = head_sum(y) * (1.0 / RWKV_HEAD)
    d = y - mean
    var = head_sum(d * d) * (1.0 / RWKV_HEAD)
    yn = d * lax.rsqrt(var + RWKV_GN_EPS) * stack(lw_ref[...]) + stack(lb_ref[...])
    bonus = head_sum(r * k * stack(rk_ref[...])) * v
    out = (yn + bonus) * stack(g_ref[...])
    for p in range(pb):
        y_ref[:, p * LANES:(p + 1) * LANES] = out[p]

    @pl.when(c == nchunks - 1)
    def _():
        hout_ref[...] = h_scr[...].reshape(hout_ref.shape)


def _rwkv_scan(r, k, v, w, a, g, k_k, k_a, r_k, ln_w, ln_b, h0, *, batch, seq_len, row0, sample):
    d = r.shape[1]
    npairs = d // LANES
    pb = min(RWKV_PB_SAMPLE if sample else RWKV_PB_PROMPT, npairs)
    if sample:
        nseq, seq_rows, nchunks = SEQ_GROUP, seq_len, 1
        rows = nseq * seq_rows
        grid = (batch // nseq, npairs // pb, 1)
    else:
        nseq, seq_rows, rows = 1, RWKV_CHUNK, RWKV_CHUNK
        nchunks = seq_len // rows
        grid = (batch, npairs // pb, nchunks)
    rb0 = row0 // rows
    tile = pl.BlockSpec((rows, pb * LANES), lambda b, p, c: (b * nchunks + c + rb0, p))
    vec = pl.BlockSpec((1, pb * LANES), lambda b, p, c: (0, p))
    state = pl.BlockSpec((nseq, pb, LANES, LANES), lambda b, p, c: (b, p, 0, 0))
    in_specs = [tile] * 6 + [vec] * 5
    args = [r, k, v, w, a, g] + [x.reshape(1, d) for x in (k_k, k_a, r_k, ln_w, ln_b)]
    if sample:
        in_specs.append(state)
        args.append(h0)
    return pl.pallas_call(
        functools.partial(_rwkv_scan_kernel, rows=rows, seq_rows=seq_rows, nseq=nseq, nchunks=nchunks,
                          has_h0=sample, pb=pb),
        grid=grid,
        in_specs=in_specs,
        out_specs=[pl.BlockSpec((rows, pb * LANES), lambda b, p, c: (b * nchunks + c, p)), state],
        out_shape=[jax.ShapeDtypeStruct((batch * seq_len, d), F32),
                   jax.ShapeDtypeStruct((batch, npairs, LANES, LANES), F32)],
        scratch_shapes=[pltpu.VMEM((nseq * pb, LANES, LANES), F32)],
        compiler_params=_cparams(("parallel", "parallel", "arbitrary")),
        name="rwkv_scan_sample" if sample else "rwkv_scan_prompt",
    )(*args)


def _pairs_from_heads(s):
    b, h, n, _ = s.shape
    st = jnp.swapaxes(s, -1, -2).reshape(b, h // 2, 2, n, n)
    top = jnp.pad(st[:, :, 0], ((0, 0), (0, 0), (0, 0), (0, n)))
    bottom = jnp.pad(st[:, :, 1], ((0, 0), (0, 0), (0, 0), (n, 0)))
    return jnp.concatenate([top, bottom], axis=2)


def _heads_from_pairs(hp):
    b, p, n2, _ = hp.shape
    n = n2 // 2
    hr = hp.reshape(b, p, 2, n, 2, n)
    st = jnp.stack([hr[:, :, 0, :, 0, :], hr[:, :, 1, :, 1, :]], axis=2).reshape(b, 2 * p, n, n)
    return jnp.swapaxes(st, -1, -2)


def _pad_cols(w, n):
    return jnp.pad(w, ((0, 0), (0, n - w.shape[1])))


def kernel(x_prompt, x_sample, cache_mla_ckv, cache_mla_kpe, page_table, state_gdn_s, state_gdn_conv, state_rwkv_wkv, state_rwkv_shift, norm_w, gdn_w_in, gdn_conv_w, gdn_a_log, gdn_dt_bias, gdn_o_norm, gdn_w_out, mla_w_in, mla_q_norm, mla_w_uq, mla_kv_norm, mla_w_uk, mla_w_uv, mla_w_o, rw_mu, rw_w_rkv, rw_w0, rw_w1, rw_w2, rw_a0, rw_a1, rw_a2, rw_g1, rw_g2, rw_k_k, rw_k_a, rw_r_k, rw_ln_w, rw_ln_b, rw_w_o, mlp_w_up, mlp_w_down):
    bp, lp, d = x_prompt.shape
    bs, ls, _ = x_sample.shape
    tp, ts = bp * lp, bs * ls
    depth = norm_w.shape[0]
    n_pages = page_table.shape[1]
    past_len = n_pages * PAGE_SIZE
    tm = _row_tile(tp, ts)
    assert ls == SUBLANES and bs % SEQ_GROUP == 0 and lp % GDN_CHUNK == 0 and lp % RWKV_CHUNK == 0

    x = jnp.concatenate([x_prompt.reshape(tp, d), x_sample.reshape(ts, d)], axis=0)
    mlp_up, mlp_down = mlp_w_up.astype(BF16), mlp_w_down.astype(BF16)
    gdn_w_nk = jnp.swapaxes(gdn_w_in, 1, 2)

    half = MLA_ROPE // 2
    inv_freq = 1.0 / (ROPE_THETA ** (jnp.arange(half, dtype=F32) / half))
    pos = jnp.concatenate([jnp.tile(jnp.arange(lp), bp), jnp.tile(past_len + jnp.arange(ls), bs)])
    ang = pos.astype(F32)[:, None] * inv_freq[None, :]
    cos, sin = jnp.cos(ang), jnp.sin(ang)
    zpad = jnp.zeros((tp + ts, LANES - MLA_ROPE), F32)
    cos2 = jnp.concatenate([cos, cos, zpad], axis=1)
    sin2 = jnp.concatenate([-sin, sin, zpad], axis=1)

    outs = {k: [] for k in ("ckv_p", "kpe_p", "ckv_s", "kpe_s", "gs_p", "gc_p", "gs_s", "gc_s",
                            "rs_p", "rx_p", "rs_s", "rx_s")}
    for i in range(depth):
        kind, j = i % 3, i // 3
        if kind == 0:
            nb = GDN_CONV_DIM + GDN_VAL_DIM
            proj = _linear(x, gdn_w_nk, wsel=j, ncols=nb, mode="norm", g=norm_w[i, 0], tm=tm)
            ba = _linear(x, gdn_w_nk, wsel=j, ncols=2 * GDN_V_HEADS, col0=nb, mode="norm", g=norm_w[i, 0], tm=tm)
            gates = _gdn_gates(ba, gdn_a_log[j], gdn_dt_bias[j], tm=tm)
            hist = jnp.pad(state_gdn_conv[:, j], ((0, 0), (SUBLANES - GDN_CONV + 1, 0), (0, 0)))
            hist = hist.reshape(ts, GDN_CONV_DIM)
            act_p = _gdn_conv(proj, None, gdn_conv_w[j], row0=0, rows=tp, seq_len=lp, sample=False, tm=tm)
            act_s = _gdn_conv(proj, hist, gdn_conv_w[j], row0=tp, rows=ts, seq_len=ls, sample=True, tm=tm)
            y_p, s_p = _gdn_scan(act_p, proj, gates, gdn_o_norm[j], None, j, batch=bp, seq_len=lp,
                                 row0=0, sample=False)
            earlier = tuple(outs["gs_s"]) if j == gdn_w_in.shape[0] - 1 else ()
            y_s, s_s = _gdn_scan(act_s, proj, gates, gdn_o_norm[j], state_gdn_s, j, batch=bs, seq_len=ls,
                                 row0=tp, sample=True, prev_states=earlier)
            if earlier:
                outs["gs_s"] = []
            w_out = gdn_w_out[j].astype(BF16)
            tail = GDN_CONV - 1
            conv_p = jnp.stack([lax.slice(proj, ((b + 1) * lp - tail, 0), ((b + 1) * lp, GDN_CONV_DIM))
                                for b in range(bp)])
            conv_s = lax.slice(proj, (tp, 0), (tp + ts, GDN_CONV_DIM)).reshape(bs, ls, GDN_CONV_DIM)[:, ls - tail:]
            outs["gs_p"].append(s_p)
            outs["gc_p"].append(conv_p)
            outs["gs_s"].append(s_s)
            outs["gc_s"].append(conv_s)
        elif kind == 1:
            w_in = mla_w_in[j]
            w_kpe = w_in[:, MLA_Q_RANK + MLA_KV_RANK:]
            w_c = jnp.concatenate([w_in[:, :MLA_Q_RANK + MLA_KV_RANK], _pad_cols(w_kpe, LANES),
                                   _pad_cols(jnp.roll(w_kpe, half, axis=1), LANES)], axis=1).astype(BF16)
            uq = mla_w_uq[j].reshape(MLA_Q_RANK, MLA_HEADS, MLA_QK)
            uq_pe = uq[:, :, MLA_NOPE:]
            padh = ((0, 0), (0, 0), (0, LANES - MLA_ROPE))
            w_q = jnp.concatenate([
                uq[:, :, :MLA_NOPE].reshape(MLA_Q_RANK, -1),
                jnp.pad(uq_pe, padh).reshape(MLA_Q_RANK, -1),
                jnp.pad(jnp.roll(uq_pe, half, axis=2), padh).reshape(MLA_Q_RANK, -1)], axis=1).astype(BF16)
            c = _linear(x, w_c, mode="norm", g=norm_w[i, 0], tm=tm)
            q_raw = _linear(c, w_q, mode="norm", g=mla_q_norm[j], tm=tm)
            ckv, kpe, qp = _mla_post(c, q_raw, cos2, sin2, mla_kv_norm[j], tm=min(tm, 256))
            w_kv = jnp.concatenate([mla_w_uk[j].reshape(MLA_KV_RANK, -1),
                                    mla_w_uv[j].reshape(MLA_KV_RANK, -1)], axis=1).astype(BF16)
            kv = _linear(ckv, w_kv, rows=tp, tm=tm)
            y_p = _mla_flash(q_raw, qp, kv, kpe, batch=bp, seq_len=lp)
            w_ukt = jnp.transpose(mla_w_uk[j], (1, 2, 0)).astype(BF16)
            w_uvh = jnp.transpose(mla_w_uv[j], (1, 0, 2)).astype(BF16)
            q_lat = _head_linear(q_raw, w_ukt, seq_len=ls, row0=tp, rows=ts, out_by_head=True, tm=tm)
            pad_keys = ((0, 0), (0, PAGE_SIZE - ls), (0, 0))
            ckv_new = jnp.pad(ckv[tp:].reshape(bs, ls, MLA_KV_RANK), pad_keys)
            kpe_new = jnp.pad(kpe[tp:, :MLA_ROPE].reshape(bs, ls, MLA_ROPE), pad_keys)
            o_lat = _mla_decode(q_lat.reshape(bs, MLA_HEADS * ls, MLA_KV_RANK), qp, cache_mla_ckv,
                                jnp.swapaxes(cache_mla_kpe, 2, 3), page_table, ckv_new,
                                jnp.swapaxes(kpe_new, 1, 2), j, new_len=ls)
            y_s = _head_linear(o_lat.reshape(bs, MLA_HEADS, ls, MLA_KV_RANK), w_uvh, seq_len=ls,
                               x_by_head=True, tm=tm)
            w_out = mla_w_o[j].astype(BF16)
            outs["ckv_p"].append(ckv[:tp].reshape(bp, lp, MLA_KV_RANK))
            outs["kpe_p"].append(kpe[:tp, :MLA_ROPE].reshape(bp, lp, MLA_ROPE))
            outs["ckv_s"].append(ckv[tp:].reshape(bs, ls, MLA_KV_RANK))
            outs["kpe_s"].append(kpe[tp:, :MLA_ROPE].reshape(bs, ls, MLA_ROPE))
        else:
            shift_rows = jnp.pad(state_rwkv_shift[:, j][:, None, :], ((0, 0), (0, ls - 1), (0, 0)))
            h, prev = _norm_shift(x, norm_w[i, 0], shift_rows.reshape(ts, d), tp=tp, seq_len=lp, tm=tm)
            mu = rw_mu[j]
            wr, wk, wv = (rw_w_rkv[j, s].astype(BF16) for s in range(3))
            r = _linear(h, wr, mode="mix", prev=prev, mu=mu[0], tm=tm)
            k = _linear(h, wk, mode="mix", prev=prev, mu=mu[1], tm=tm)
            v = _linear(h, wv, mode="mix", prev=prev, mu=mu[2], tm=tm)
            rank = lambda n: -(-n // LANES) * LANES
            lora_w = lambda w1, w2: (_pad_cols(w1, rank(w1.shape[1])).astype(BF16),
                                     jnp.pad(w2, ((0, rank(w2.shape[0]) - w2.shape[0]), (0, 0))).astype(BF16))
            w = _lora(h, prev, mu[3], *lora_w(rw_w1[j], rw_w2[j]), rw_w0[j], kind="decay", tm=tm)
            a = _lora(h, prev, mu[4], *lora_w(rw_a1[j], rw_a2[j]), rw_a0[j], kind="aaa", tm=tm)
            g = _lora(h, prev, mu[5], *lora_w(rw_g1[j], rw_g2[j]), jnp.zeros((d,), F32), kind="gate", tm=tm)
            params = (rw_k_k[j], rw_k_a[j], rw_r_k[j].reshape(d), rw_ln_w[j], rw_ln_b[j])
            y_p, hp_p = _rwkv_scan(r, k, v, w, a, g, *params, None, batch=bp, seq_len=lp, row0=0, sample=False)
            y_s, hp_s = _rwkv_scan(r, k, v, w, a, g, *params, _pairs_from_heads(state_rwkv_wkv[:, j]),
                                   batch=bs, seq_len=ls, row0=tp, sample=True)
            w_out = rw_w_o[j].astype(BF16)
            outs["rs_p"].append(_heads_from_pairs(hp_p))
            outs["rx_p"].append(jnp.concatenate([lax.slice(h, ((b + 1) * lp - 1, 0), ((b + 1) * lp, d))
                                                 for b in range(bp)]))
            outs["rs_s"].append(_heads_from_pairs(hp_s))
            outs["rx_s"].append(lax.slice(h, (tp, 0), (tp + ts, d)).reshape(bs, ls, d)[:, -1])
        x = _out_proj(y_p, y_s, w_out, x, norm_w[i, 1], tm=tm)
        x = _mlp(x, norm_w[i, 2], mlp_up, mlp_down, i, norm_w[i, 3], tm=tm)

    def stack(key):
        rows = outs[key]
        if key == "gs_s" and rows[0].ndim == state_gdn_s.ndim:
            return rows[0]
        return jnp.stack(rows, axis=1)

    return (x[:tp].reshape(bp, lp, d), x[tp:].reshape(bs, ls, d),
            stack("ckv_p"), stack("kpe_p"), stack("gs_p"), stack("gc_p"), stack("rs_p"), stack("rx_p"),
            stack("ckv_s"), stack("kpe_s"), stack("gs_s"), stack("gc_s"), stack("rs_s"), stack("rx_s"))
```

```python
import functools
import math

import jax
import jax.numpy as jnp
from jax import lax
from jax.experimental import pallas as pl
from jax.experimental.pallas import tpu as pltpu

F32 = jnp.float32
BF16 = jnp.bfloat16

NORM_EPS = 1e-6
L2_EPS = 1e-6
RWKV_GN_EPS = 64e-5
ROPE_THETA = 10000.0

LANES = 128
SUBLANES = 8

GDN_QK_HEADS = 16
GDN_V_HEADS = 32
GDN_HEAD = 128
GDN_CONV = 4
GDN_CHUNK = 64
GDN_KEY_DIM = GDN_QK_HEADS * GDN_HEAD
GDN_VAL_DIM = GDN_V_HEADS * GDN_HEAD
GDN_CONV_DIM = 2 * GDN_KEY_DIM + GDN_VAL_DIM
GDN_HB_PROMPT = 32
GDN_HB_SAMPLE = 8

MLA_HEADS = 16
MLA_Q_RANK = 512
MLA_KV_RANK = 512
MLA_NOPE = 128
MLA_ROPE = 64
MLA_V = 128
MLA_QK = MLA_NOPE + MLA_ROPE
PAGE_SIZE = 128
DEC_PAGES_PER_STEP = 16
DEC_PAGES_PER_CHUNK = 16
MLA_FLASH_HEADS = 2
MLA_FLASH_TILE = 1024

RWKV_HEAD = 64
RWKV_CHUNK = 64
RWKV_PB_PROMPT = 16
RWKV_PB_SAMPLE = 4
SEQ_GROUP = 8

VMEM_LIMIT = 48 * 1024 * 1024
WIDE_ROW_TILE = 1024


def _cparams(sem):
    return pltpu.CompilerParams(dimension_semantics=sem, vmem_limit_bytes=VMEM_LIMIT)


def _row_tile(*row_counts):
    for t in (512, 256, 128, 64, 32, 16, 8):
        if all(r % t == 0 for r in row_counts):
            return t
    raise ValueError(f"row counts {row_counts} are not multiples of {SUBLANES}")


def _col_tile(n):
    for t in (1024, 640, 512, 384, 256, 128):
        if n % t == 0:
            return t
    return n


def _dot(a, b):
    return jnp.dot(a.astype(BF16), b.astype(BF16), preferred_element_type=F32)


def _dot_nt(a, b):
    return lax.dot_general(a.astype(BF16), b.astype(BF16), (((1,), (1,)), ((), ())),
                           preferred_element_type=F32)


def _mm(a, b):
    if a.ndim == 2:
        return _dot(a, b)
    return jnp.einsum("bmk,bkn->bmn", a.astype(BF16), b.astype(BF16), preferred_element_type=F32)


def _mm_nt(a, b):
    if a.ndim == 2:
        return _dot_nt(a, b)
    return jnp.einsum("bmk,bnk->bmn", a.astype(BF16), b.astype(BF16), preferred_element_type=F32)


def _split3(x):
    hi = x.astype(BF16)
    r1 = x - hi.astype(F32)
    mid = r1.astype(BF16)
    lo = (r1 - mid.astype(F32)).astype(BF16)
    return hi, mid, lo


def _dot_exact_lhs(a_bf16, x):
    hi, mid, lo = _split3(x)
    return (jnp.dot(a_bf16, hi, preferred_element_type=F32)
            + jnp.dot(a_bf16, mid, preferred_element_type=F32)
            + jnp.dot(a_bf16, lo, preferred_element_type=F32))


def _dot_exact_rhs(x, b_bf16):
    hi, mid, lo = _split3(x)
    return (jnp.dot(hi, b_bf16, preferred_element_type=F32)
            + jnp.dot(mid, b_bf16, preferred_element_type=F32)
            + jnp.dot(lo, b_bf16, preferred_element_type=F32))


def _rms(x, g, eps=NORM_EPS):
    return x * lax.rsqrt(jnp.mean(x * x, axis=-1, keepdims=True) + eps) * g


def _sigmoid(x):
    return 1.0 / (1.0 + jnp.exp(-x))


def _softplus(x):
    return jnp.maximum(x, 0.0) + jnp.log(1.0 + jnp.exp(-jnp.abs(x)))


def _unit_lower_inverse(a, ii, jj, seq_rows):
    mm = _mm
    base = min(SUBLANES, seq_rows)
    sh = int(math.log2(base))
    a0 = jnp.where(jnp.right_shift(ii, sh) == jnp.right_shift(jj, sh), a, 0.0)
    x = jnp.where(ii == jj, 1.0, 0.0) - a0
    p = a0
    k = 1
    while 2 * k < base:
        p = mm(p, p)
        x = x + mm(x, p)
        k *= 2
    s = base
    while s < seq_rows:
        sh = int(math.log2(s))
        lower_left = jnp.logical_and(
            jnp.right_shift(ii, sh + 1) == jnp.right_shift(jj, sh + 1),
            jnp.logical_and(jnp.bitwise_and(jnp.right_shift(ii, sh), 1) == 1,
                            jnp.bitwise_and(jnp.right_shift(jj, sh), 1) == 0))
        e = jnp.where(lower_left, a, 0.0)
        x = x - mm(mm(x, e), x)
        s *= 2
    return x


def _linear_kernel(*refs, mode, w_is_nk):
    if mode == "norm":
        x_ref, g_ref, w_ref, o_ref, xs_ref = refs
    elif mode == "mix":
        x_ref, p_ref, mu_ref, w_ref, o_ref, xs_ref = refs
    else:
        x_ref, w_ref, o_ref, xs_ref = refs

    @pl.when(pl.program_id(1) == 0)
    def _():
        x = x_ref[...]
        if mode == "norm":
            x = _rms(x, g_ref[...])
        elif mode == "mix":
            x = x + (p_ref[...] - x) * mu_ref[...]
        xs_ref[...] = x.astype(BF16)

    if w_is_nk:
        o_ref[...] = _dot_nt(xs_ref[...], w_ref[...])
    else:
        o_ref[...] = jnp.dot(xs_ref[...], w_ref[...].astype(BF16), preferred_element_type=F32)


def _linear(x, w, *, mode="none", g=None, prev=None, mu=None, xcol=0, row0=0, rows=None, wsel=None, ncols=None,
            col0=0, tm):
    stacked = w.ndim == 3
    k = w.shape[-1] if stacked else w.shape[0]
    n = ncols if stacked else w.shape[1]
    rows = x.shape[0] - row0 if rows is None else rows
    if stacked and rows % WIDE_ROW_TILE == 0 and row0 % WIDE_ROW_TILE == 0:
        tm = WIDE_ROW_TILE
    tn = _col_tile(n) if not stacked else next(t for t in (512, 256, 128, n) if n % t == 0)
    cb0 = col0 // tn
    rb0 = row0 // tm
    xmap = lambda i, j: (i + rb0, xcol)
    vec = pl.BlockSpec((1, k), lambda i, j: (0, 0))
    in_specs, args = [pl.BlockSpec((tm, k), xmap)], [x]
    if mode == "norm":
        in_specs.append(vec)
        args.append(g.reshape(1, k))
    elif mode == "mix":
        in_specs += [pl.BlockSpec((tm, k), xmap), vec]
        args += [prev, mu.reshape(1, k)]
    if stacked:
        in_specs.append(pl.BlockSpec((None, tn, k), lambda i, j: (wsel, cb0 + j, 0)))
    else:
        in_specs.append(pl.BlockSpec((k, tn), lambda i, j: (0, j)))
    args.append(w)
    return pl.pallas_call(
        functools.partial(_linear_kernel, mode=mode, w_is_nk=stacked),
        grid=(rows // tm, n // tn),
        in_specs=in_specs,
        out_specs=pl.BlockSpec((tm, tn), lambda i, j: (i, j)),
        out_shape=jax.ShapeDtypeStruct((rows, n), F32),
        scratch_shapes=[pltpu.VMEM((tm, k), BF16)],
        compiler_params=_cparams(("parallel", "arbitrary")),
        name=f"linear_{mode}",
    )(*args)


def _norm_shift_kernel(x_ref, x8_ref, st_ref, g_ref, h_ref, p_ref, *, tm, npb, seq_len):
    i = pl.program_id(0)
    g = g_ref[...]
    h = _rms(x_ref[...], g)
    h_ref[...] = h
    rolled = pltpu.roll(h, 1, 0)
    row = lax.broadcasted_iota(jnp.int32, h.shape, 0)
    first = lax.rem(i * tm, seq_len) == 0
    carry = jnp.where(first, 0.0, _rms(x8_ref[...], g)[SUBLANES - 1:SUBLANES])
    prev_p = jnp.where(row == 0, carry, rolled)
    prev_s = jnp.where(jnp.bitwise_and(row, SUBLANES - 1) == 0, st_ref[...], rolled)
    p_ref[...] = jnp.where(i < npb, prev_p, prev_s)


def _norm_shift(x, g, shift_rows, *, tp, seq_len, tm):
    t, d = x.shape
    npb = tp // tm
    per = tm // SUBLANES
    row = pl.BlockSpec((tm, d), lambda i: (i, 0))
    return pl.pallas_call(
        functools.partial(_norm_shift_kernel, tm=tm, npb=npb, seq_len=seq_len),
        grid=(t // tm,),
        in_specs=[row, pl.BlockSpec((SUBLANES, d), lambda i: (jnp.maximum(i * per - 1, 0), 0)),
                  pl.BlockSpec((tm, d), lambda i: (jnp.maximum(i - npb, 0), 0)),
                  pl.BlockSpec((1, d), lambda i: (0, 0))],
        out_specs=[row, row],
        out_shape=[jax.ShapeDtypeStruct((t, d), F32), jax.ShapeDtypeStruct((t, d), F32)],
        compiler_params=_cparams(("parallel",)),
        name="rms_norm_shift",
    )(x, x, shift_rows, g.reshape(1, d))


def _out_proj_kernel(yp_ref, ys_ref, w_ref, r_ref, g_ref, o_ref, acc_ref, *, npb, nk):
    i = pl.program_id(0)
    k = pl.program_id(1)

    @pl.when(k == 0)
    def _():
        acc_ref[...] = jnp.zeros_like(acc_ref)

    @pl.when(i < npb)
    def _():
        acc_ref[...] += jnp.dot(yp_ref[...].astype(BF16), w_ref[...], preferred_element_type=F32)

    @pl.when(i >= npb)
    def _():
        acc_ref[...] += jnp.dot(ys_ref[...].astype(BF16), w_ref[...], preferred_element_type=F32)

    @pl.when(k == nk - 1)
    def _():
        o_ref[...] = r_ref[...] + _rms(acc_ref[...], g_ref[...])


def _out_proj(y_p, y_s, w, resid, g, *, tm):
    kdim, d = w.shape
    tk = _col_tile(kdim)
    npb, nsb, nk = y_p.shape[0] // tm, y_s.shape[0] // tm, kdim // tk
    return pl.pallas_call(
        functools.partial(_out_proj_kernel, npb=npb, nk=nk),
        grid=(npb + nsb, nk),
        in_specs=[
            pl.BlockSpec((tm, tk), lambda i, k: (jnp.minimum(i, npb - 1), jnp.where(i < npb, k, nk - 1))),
            pl.BlockSpec((tm, tk), lambda i, k: (jnp.maximum(i - npb, 0), jnp.where(i >= npb, k, 0))),
            pl.BlockSpec((tk, d), lambda i, k: (k, 0)),
            pl.BlockSpec((tm, d), lambda i, k: (i, 0)),
            pl.BlockSpec((1, d), lambda i, k: (0, 0)),
        ],
        out_specs=pl.BlockSpec((tm, d), lambda i, k: (i, 0)),
        out_shape=jax.ShapeDtypeStruct(resid.shape, F32),
        scratch_shapes=[pltpu.VMEM((tm, d), F32)],
        compiler_params=_cparams(("parallel", "arbitrary")),
        name="out_proj_resnorm",
    )(y_p, y_s, w, resid, g.reshape(1, d))


def _mlp_kernel(x_ref, g2_ref, wu_ref, wd_ref, g3_ref, o_ref, xs_ref, *, nf):
    j = pl.program_id(1)

    @pl.when(j == 0)
    def _():
        xs_ref[...] = _rms(x_ref[...], g2_ref[...]).astype(BF16)
        o_ref[...] = jnp.zeros_like(o_ref)

    h = jnp.dot(xs_ref[...], wu_ref[...].astype(BF16), preferred_element_type=F32)
    h = jnp.square(jnp.maximum(h, 0.0))
    o_ref[...] += jnp.dot(h.astype(BF16), wd_ref[...].astype(BF16), preferred_element_type=F32)

    @pl.when(j == nf - 1)
    def _():
        o_ref[...] = x_ref[...] + _rms(o_ref[...], g3_ref[...])


def _mlp(x, g2, w_up, w_down, layer, g3, *, tm):
    t, d = x.shape
    f = w_up.shape[-1]
    tf = _col_tile(f)
    vec = pl.BlockSpec((1, d), lambda i, j: (0, 0))
    return pl.pallas_call(
        functools.partial(_mlp_kernel, nf=f // tf),
        grid=(t // tm, f // tf),
        in_specs=[pl.BlockSpec((tm, d), lambda i, j: (i, 0)), vec,
                  pl.BlockSpec((None, d, tf), lambda i, j: (layer, 0, j)),
                  pl.BlockSpec((None, tf, d), lambda i, j: (layer, j, 0)), vec],
        out_specs=pl.BlockSpec((tm, d), lambda i, j: (i, 0)),
        out_shape=jax.ShapeDtypeStruct((t, d), F32),
        scratch_shapes=[pltpu.VMEM((tm, d), BF16)],
        compiler_params=_cparams(("parallel", "arbitrary")),
        name="sq_relu_mlp",
    )(x, g2.reshape(1, d), w_up, w_down, g3.reshape(1, d))


def _head_linear_kernel(x_ref, w_ref, o_ref):
    x = x_ref[...]
    y = jnp.dot(x.reshape(-1, x.shape[-1]).astype(BF16), w_ref[...], preferred_element_type=F32)
    o_ref[...] = y.reshape(o_ref.shape)


def _head_linear(x, w, *, seq_len, row0=0, rows=None, x_by_head=False, out_by_head=False, tm):
    nh, k, n = w.shape
    rows = x.shape[0] * x.shape[2] if x_by_head else rows
    rb0 = row0 // tm
    by_head = lambda width: pl.BlockSpec((tm // seq_len, None, seq_len, width), lambda i, h: (i, h, 0, 0))
    x_spec = by_head(k) if x_by_head else pl.BlockSpec((tm, k), lambda i, h: (i + rb0, h))
    if out_by_head:
        out_spec, out_shape = by_head(n), (rows // seq_len, nh, seq_len, n)
    else:
        out_spec, out_shape = pl.BlockSpec((tm, n), lambda i, h: (i, h)), (rows, nh * n)
    return pl.pallas_call(
        _head_linear_kernel,
        grid=(rows // tm, nh),
        in_specs=[x_spec, pl.BlockSpec((None, k, n), lambda i, h: (h, 0, 0))],
        out_specs=out_spec,
        out_shape=jax.ShapeDtypeStruct(out_shape, F32),
        compiler_params=_cparams(("parallel", "parallel")),
        name="head_linear",
    )(x, w)


def _gdn_gate_kernel(x_ref, alog_ref, dtb_ref, e_ref, o_ref):
    x = x_ref[...]
    lane = lax.broadcasted_iota(jnp.int32, x.shape, 1)
    g = -jnp.exp(alog_ref[...]) * _softplus(x + dtb_ref[...])
    bg = jnp.where(lane < GDN_V_HEADS, _sigmoid(x), g)
    o_ref[...] = _dot_exact_rhs(bg, e_ref[...])


def _gdn_gates(ba, a_log, dt_bias, *, tm):
    t, n = ba.shape
    tn = 16 * LANES
    pad = jnp.zeros((GDN_V_HEADS,), F32)
    vec = pl.BlockSpec((1, n), lambda i, j: (0, 0))
    expand = jnp.repeat(jnp.roll(jnp.eye(n, dtype=BF16), GDN_V_HEADS, axis=1), LANES, axis=1)
    return pl.pallas_call(
        _gdn_gate_kernel,
        grid=(t // tm, n * LANES // tn),
        in_specs=[pl.BlockSpec((tm, n), lambda i, j: (i, 0)), vec, vec,
                  pl.BlockSpec((n, tn), lambda i, j: (0, j))],
        out_specs=pl.BlockSpec((tm, tn), lambda i, j: (i, j)),
        out_shape=jax.ShapeDtypeStruct((t, n * LANES), F32),
        compiler_params=_cparams(("parallel", "parallel")),
        name="gdn_gates",
    )(ba, jnp.concatenate([pad, a_log]).reshape(1, n), jnp.concatenate([pad, dt_bias]).reshape(1, n), expand)


def _gdn_conv_kernel(x_ref, prev_ref, w_ref, o_ref, *, tm, tc, seq_len, nq, nqk, sample):
    i = pl.program_id(0)
    j = pl.program_id(1)
    x = x_ref[...]
    w = w_ref[...]
    tap = lambda s: w[GDN_CONV - 1 - s:GDN_CONV - s]
    is_qk = j < nqk
    qscale = jnp.where(j < nq, GDN_HEAD ** -0.5, 1.0)

    def finish(acc, rows):
        y = acc * _sigmoid(acc)
        for hh in range(tc // GDN_HEAD):
            sl = slice(hh * GDN_HEAD, (hh + 1) * GDN_HEAD)
            yh = y[:, sl]
            nrm = lax.rsqrt(jnp.sum(yh * yh, axis=-1, keepdims=True) + L2_EPS) * qscale
            o_ref[rows, sl] = (yh * jnp.where(is_qk, nrm, 1.0))[rows]

    acc = x * tap(0)
    if sample:
        hist = prev_ref[...]
        tok = jnp.bitwise_and(lax.broadcasted_iota(jnp.int32, x.shape, 0), SUBLANES - 1)
        for s in range(1, GDN_CONV):
            xs = jnp.where(tok >= s, pltpu.roll(x, s, 0), pltpu.roll(hist, tm - SUBLANES + s, 0))
            acc = acc + xs * tap(s)
        finish(acc, slice(None))
    else:
        first = lax.rem(i * tm, seq_len) == 0
        prev = jnp.where(first, 0.0, prev_ref[...])
        row = lax.broadcasted_iota(jnp.int32, prev.shape, 0)
        head_rows = slice(0, SUBLANES)
        acc8 = x[head_rows] * tap(0)
        for s in range(1, GDN_CONV):
            xs = pltpu.roll(x, s, 0)
            acc = acc + xs * tap(s)
            acc8 = acc8 + jnp.where(row < s, pltpu.roll(prev, s, 0), xs[head_rows]) * tap(s)
        if tm > SUBLANES:
            finish(acc, slice(SUBLANES, None))
        finish(acc8, head_rows)


def _gdn_conv(proj, hist, conv_w, *, row0, rows, seq_len, sample, tm):
    tc = GDN_KEY_DIM
    rb0 = row0 // tm
    if sample:
        prev_spec = pl.BlockSpec((tm, tc), lambda i, j: (i, j))
        prev = hist
    else:
        per = tm // SUBLANES
        prev_spec = pl.BlockSpec((SUBLANES, tc), lambda i, j: (jnp.maximum((i + rb0) * per - 1, 0), j))
        prev = proj
    return pl.pallas_call(
        functools.partial(_gdn_conv_kernel, tm=tm, tc=tc, seq_len=seq_len, nq=GDN_KEY_DIM // tc,
                          nqk=2 * GDN_KEY_DIM // tc, sample=sample),
        grid=(rows // tm, GDN_CONV_DIM // tc),
        in_specs=[pl.BlockSpec((tm, tc), lambda i, j: (i + rb0, j)), prev_spec,
                  pl.BlockSpec((GDN_CONV, tc), lambda i, j: (0, j))],
        out_specs=pl.BlockSpec((tm, tc), lambda i, j: (i, j)),
        out_shape=jax.ShapeDtypeStruct((rows, GDN_CONV_DIM), F32),
        compiler_params=_cparams(("parallel", "parallel")),
        name="gdn_conv_sample" if sample else "gdn_conv_prompt",
    )(proj, prev, conv_w)


def _stack_lanes(x, n):
    return jnp.stack([x[:, i * LANES:(i + 1) * LANES] for i in range(n)])


def _stack_cols(x, lo, n):
    return jnp.stack([x[:, lo + i:lo + i + 1] for i in range(n)])


def _over_seqs(x, nseq):
    return jnp.broadcast_to(x[None], (nseq,) + x.shape).reshape((nseq * x.shape[0],) + x.shape[1:])


def _chunk_masks(rows, seq_rows):
    ii = lax.broadcasted_iota(jnp.int32, (rows, rows), 0)
    jj = lax.broadcasted_iota(jnp.int32, (rows, rows), 1)
    shift = int(math.log2(seq_rows))
    same = jnp.right_shift(ii, shift) == jnp.right_shift(jj, shift)
    incl = jnp.logical_and(same, ii >= jj)
    strict = jnp.logical_and(same, ii > jj)
    last = jnp.logical_and(same, jnp.bitwise_and(jj, seq_rows - 1) == seq_rows - 1)
    return ii, jj, incl, strict, last


def _seq_mask(nseq, rows, seq_rows):
    shape = (nseq, 1, rows, 1)
    n = lax.broadcasted_iota(jnp.int32, shape, 0)
    r = lax.broadcasted_iota(jnp.int32, shape, 2)
    return jnp.right_shift(r, int(math.log2(seq_rows))) == n


def _gdn_scan_kernel(*refs, rows, seq_rows, nseq, nchunks, has_s0, hb, nprev):
    q_ref, k_ref, v_ref, z_ref, g_ref, b_ref, on_ref = refs[:7]
    s0_ref = refs[7] if has_s0 else None
    prev_refs = refs[7 + has_s0:7 + has_s0 + nprev]
    o_ref, sout_ref, s_scr = refs[7 + has_s0 + nprev:]
    c = pl.program_id(2)
    nb = nseq * hb

    @pl.when(c == 0)
    def _():
        if has_s0:
            s_scr[...] = s0_ref[...].reshape(nb, GDN_HEAD, GDN_HEAD)
        else:
            s_scr[...] = jnp.zeros_like(s_scr)

    ii, jj, incl, strict, last = _chunk_masks(rows, seq_rows)
    cum_all = _dot_exact_lhs(jnp.where(incl, 1.0, 0.0).astype(BF16), g_ref[...])
    if nseq == 1:
        tot_all = cum_all[rows - 1:rows]
    else:
        tot_all = _dot_exact_lhs(jnp.where(last, 1.0, 0.0).astype(BF16), cum_all)
    cum = _stack_lanes(cum_all, hb)
    e_cum = _stack_lanes(jnp.exp(cum_all), hb)
    e_tail = _stack_lanes(jnp.exp(tot_all - cum_all), hb)
    g_tot = _stack_lanes(jnp.exp(tot_all), hb)
    beta = _stack_lanes(b_ref[...], hb)
    cum_c = cum[:, :, :rows]
    cum_r = jnp.sum(jnp.where(ii == jj, cum_c, 0.0), axis=1, keepdims=True)
    decay = jnp.where(incl, jnp.exp(jnp.where(incl, cum_c - cum_r, 0.0)), 0.0)

    hq = hb // 2
    q2 = _stack_lanes(q_ref[...], hq)
    k2 = _stack_lanes(k_ref[...], hq)
    per_v_head = lambda x: jnp.stack([x[h // 2] for h in range(hb)])
    kk = per_v_head(_mm_nt(k2, k2))
    qk = per_v_head(_mm_nt(q2, k2))
    q = per_v_head(q2)
    k = per_v_head(k2)
    v = _stack_lanes(v_ref[...], hb)
    z = _stack_lanes(z_ref[...], hb)

    a_mat = jnp.where(strict, beta[:, :, :rows] * decay * kk, 0.0)
    t_inv = _unit_lower_inverse(a_mat, ii, jj, seq_rows)
    uw = _mm(t_inv, jnp.concatenate([beta * v, (beta * e_cum) * k], axis=2))
    u_v = uw[:, :, :GDN_HEAD]
    w_k = uw[:, :, GDN_HEAD:]
    p_qk = decay * qk
    q_g = q * e_cum
    k_d_t = jnp.swapaxes(k * e_tail, 1, 2)
    s = s_scr[...]
    sb = s.astype(BF16)
    if nseq == 1:
        u = u_v - _mm(w_k, sb)
        o = _mm(q_g, sb) + _mm(p_qk, u)
        s_scr[...] = g_tot * s + _mm(k_d_t, u)
    else:
        mine = _seq_mask(nseq, rows, seq_rows)
        pick = lambda x: jnp.sum(jnp.where(mine, x.reshape((nseq, hb) + x.shape[1:]), 0.0), axis=0)
        u = u_v - pick(_mm(_over_seqs(w_k, nseq), sb))
        o = pick(_mm(_over_seqs(q_g, nseq), sb)) + _mm(p_qk, u)
        u_n = jnp.where(mine, u[None], 0.0).reshape(nb, rows, GDN_HEAD)
        gt = jnp.stack([g_tot[:, (n + 1) * seq_rows - 1:(n + 1) * seq_rows] for n in range(nseq)])
        s_scr[...] = gt.reshape(nb, 1, GDN_HEAD) * s + _mm(_over_seqs(k_d_t, nseq), u_n)
    o = _rms(o, on_ref[...]) * (z * _sigmoid(z))
    for h in range(hb):
        o_ref[:, h * GDN_HEAD:(h + 1) * GDN_HEAD] = o[h]

    @pl.when(c == nchunks - 1)
    def _():
        new = s_scr[...].reshape(nseq, hb, GDN_HEAD, GDN_HEAD)
        if nprev:
            for l, p_ref in enumerate(prev_refs):
                sout_ref[:, l] = p_ref[...]
            sout_ref[:, nprev] = new
        else:
            sout_ref[...] = new


def _gdn_scan(act, proj, gates, o_norm, s0, layer, *, batch, seq_len, row0, sample, prev_states=()):
    hb = GDN_HB_SAMPLE if sample else GDN_HB_PROMPT
    ng = GDN_V_HEADS // hb
    hgw = hb * GDN_HEAD
    qkw = hgw // 2
    if sample:
        nseq, seq_rows, nchunks = SEQ_GROUP, seq_len, 1
        rows = nseq * seq_rows
        grid = (batch // nseq, ng, 1)
    else:
        nseq, seq_rows, rows = 1, GDN_CHUNK, GDN_CHUNK
        nchunks = seq_len // rows
        grid = (batch, ng, nchunks)
    rb0 = row0 // rows
    rmap = lambda b, g, c: b * nchunks + c
    in_specs = [
        pl.BlockSpec((rows, qkw), lambda b, g, c: (rmap(b, g, c), g)),
        pl.BlockSpec((rows, qkw), lambda b, g, c: (rmap(b, g, c), GDN_KEY_DIM // qkw + g)),
        pl.BlockSpec((rows, hgw), lambda b, g, c: (rmap(b, g, c), 2 * GDN_KEY_DIM // hgw + g)),
        pl.BlockSpec((rows, hgw), lambda b, g, c: (rmap(b, g, c) + rb0, GDN_CONV_DIM // hgw + g)),
        pl.BlockSpec((rows, hgw), lambda b, g, c: (rmap(b, g, c) + rb0, g)),
        pl.BlockSpec((rows, hgw), lambda b, g, c: (rmap(b, g, c) + rb0, ng + g)),
        pl.BlockSpec((1, GDN_HEAD), lambda b, g, c: (0, 0)),
    ]
    args = [act, act, act, proj, gates, gates, o_norm.reshape(1, GDN_HEAD)]
    if sample:
        in_specs.append(pl.BlockSpec((nseq, None, hb, GDN_HEAD, GDN_HEAD),
                                     lambda b, g, c: (b, layer, g, 0, 0)))
        args.append(s0)
    state_spec = pl.BlockSpec((nseq, hb, GDN_HEAD, GDN_HEAD), lambda b, g, c: (b, g, 0, 0))
    state_shape = jax.ShapeDtypeStruct((batch, GDN_V_HEADS, GDN_HEAD, GDN_HEAD), F32)
    nprev = len(prev_states)
    if nprev:
        in_specs += [state_spec] * nprev
        args += list(prev_states)
        state_spec = pl.BlockSpec((nseq, nprev + 1, hb, GDN_HEAD, GDN_HEAD), lambda b, g, c: (b, 0, g, 0, 0))
        state_shape = jax.ShapeDtypeStruct((batch, nprev + 1, GDN_V_HEADS, GDN_HEAD, GDN_HEAD), F32)
    return pl.pallas_call(
        functools.partial(_gdn_scan_kernel, rows=rows, seq_rows=seq_rows, nseq=nseq, nchunks=nchunks,
                          has_s0=sample, hb=hb, nprev=nprev),
        grid=grid,
        in_specs=in_specs,
        out_specs=[pl.BlockSpec((rows, hgw), lambda b, g, c: (rmap(b, g, c), g)), state_spec],
        out_shape=[jax.ShapeDtypeStruct((batch * seq_len, GDN_VAL_DIM), F32), state_shape],
        scratch_shapes=[pltpu.VMEM((nseq * hb, GDN_HEAD, GDN_HEAD), F32)],
        compiler_params=_cparams(("parallel", "parallel", "arbitrary")),
        name="gdn_scan_sample" if sample else "gdn_scan_prompt",
    )(*args)


def _mla_post_kernel(ckv_in_ref, kpe_in_ref, kpe_sw_ref, qpe_ref, qsw_ref, cos_ref, sin_ref, g_ref,
                     ckv_ref, kpe_ref, qp_ref):
    ckv_ref[...] = _rms(ckv_in_ref[...], g_ref[...])
    cos = cos_ref[...]
    sin = sin_ref[...]
    kpe_ref[...] = kpe_in_ref[...] * cos + kpe_sw_ref[...] * sin
    for h in range(MLA_HEADS):
        sl = slice(h * LANES, (h + 1) * LANES)
        qp_ref[:, sl] = qpe_ref[:, sl] * cos + qsw_ref[:, sl] * sin


def _mla_post(c, q_raw, cos2, sin2, kv_norm, *, tm):
    t = c.shape[0]
    hw = MLA_HEADS * LANES
    c0 = MLA_Q_RANK // MLA_KV_RANK
    k0 = (MLA_Q_RANK + MLA_KV_RANK) // LANES
    row = lambda w, col: pl.BlockSpec((tm, w), lambda i: (i, col))
    return pl.pallas_call(
        _mla_post_kernel,
        grid=(t // tm,),
        in_specs=[row(MLA_KV_RANK, c0), row(LANES, k0), row(LANES, k0 + 1),
                  row(hw, 1), row(hw, 2), row(LANES, 0), row(LANES, 0),
                  pl.BlockSpec((1, MLA_KV_RANK), lambda i: (0, 0))],
        out_specs=[row(MLA_KV_RANK, 0), row(LANES, 0), row(hw, 0)],
        out_shape=[jax.ShapeDtypeStruct((t, MLA_KV_RANK), F32), jax.ShapeDtypeStruct((t, LANES), F32),
                   jax.ShapeDtypeStruct((t, hw), F32)],
        compiler_params=_cparams(("parallel",)),
        name="mla_post",
    )(c, c, c, q_raw, q_raw, cos2, sin2, kv_norm.reshape(1, MLA_KV_RANK))


def _online_softmax_chunks(scores, values, nchunks, m, l, acc, mask=None):
    s_next = scores(0)
    for c in range(nchunks):
        s = s_next
        if c + 1 < nchunks:
            s_next = scores(c + 1)
        if mask is not None:
            s = mask(c, s)
        m_new = jnp.maximum(m, jnp.max(s, axis=-1, keepdims=True))
        alpha = jnp.exp(m - m_new)
        p = jnp.exp(s - m_new)
        l = alpha * l + jnp.sum(p, axis=-1, keepdims=True)
        acc = alpha * acc + jnp.dot(p.astype(BF16), values(c), preferred_element_type=F32)
        m = m_new
    return m, l, acc


def _flash_kernel(qi_ref, ki_ref, qn_ref, qp_ref, kn_ref, kp_ref, v_ref, o_ref, qs_ref, kc_ref,
                  m_ref, l_ref, acc_ref, *, scale, nh):
    t = pl.program_id(2)
    qi = qi_ref[t]
    ki = ki_ref[t]
    tq = qn_ref.shape[0]
    head = lambda e: slice(e * LANES, (e + 1) * LANES)

    @pl.when(ki == 0)
    def _():
        for e in range(nh):
            qs_ref[e, :, :LANES] = (qn_ref[:, head(e)] * scale).astype(BF16)
            qs_ref[e, :, LANES:] = (qp_ref[:, head(e)] * scale).astype(BF16)
        m_ref[...] = jnp.full_like(m_ref, -jnp.inf)
        l_ref[...] = jnp.zeros_like(l_ref)
        acc_ref[...] = jnp.zeros_like(acc_ref)

    kpb = kp_ref[...].astype(BF16)
    for e in range(nh):
        kc_ref[e, :, :LANES] = kn_ref[:, head(e)].astype(BF16)
        kc_ref[e, :, LANES:] = kpb

    def block(diagonal):
        nr = 2 if tq >= 2 * LANES else 1
        tr = tq // nr
        streams = [(e, r) for e in range(nh) for r in range(nr)]
        rows = lambda r: slice(r * tr, (r + 1) * tr)
        score = lambda e, r: _dot_nt(qs_ref[e, rows(r), :], kc_ref[e])
        ahead = 2
        pending = [score(*st) for st in streams[:ahead]]
        for i, (e, r) in enumerate(streams):
            s = pending.pop(0)
            if i + ahead < len(streams):
                pending.append(score(*streams[i + ahead]))
            if diagonal:
                row = lax.broadcasted_iota(jnp.int32, s.shape, 0) + r * tr
                col = lax.broadcasted_iota(jnp.int32, s.shape, 1)
                s = jnp.where(row >= col, s, -jnp.inf)
            m_old = m_ref[e, rows(r), :]
            m_new = jnp.maximum(m_old, jnp.max(s, axis=-1, keepdims=True))
            alpha = jnp.exp(m_old - m_new)
            p = jnp.exp(s - m_new)
            l_ref[e, rows(r), :] = alpha * l_ref[e, rows(r), :] + jnp.sum(p, axis=-1, keepdims=True)
            acc_ref[e, rows(r), :] = alpha * acc_ref[e, rows(r), :] + _dot(p, v_ref[:, head(e)])
            m_ref[e, rows(r), :] = m_new

    @pl.when(ki < qi)
    def _():
        block(False)

    @pl.when(ki == qi)
    def _():
        block(True)
        for e in range(nh):
            o_ref[:, head(e)] = acc_ref[e] / l_ref[e]


def _mla_flash(q_raw, qp, kv, kpe, *, batch, seq_len):
    tq = next(t for t in (1024, 512, 256, 128, 64) if t <= MLA_FLASH_TILE and seq_len % t == 0)
    nq = seq_len // tq
    pairs = [(qi, ki) for qi in range(nq) for ki in range(qi + 1)]
    qi_of = jnp.asarray([p[0] for p in pairs], jnp.int32)
    ki_of = jnp.asarray([p[1] for p in pairs], jnp.int32)
    nh = MLA_FLASH_HEADS
    hw = nh * LANES
    qmap = lambda b, h, t, qi, ki: (b * nq + qi[t], h)
    kmap = lambda off: (lambda b, h, t, qi, ki: (b * nq + ki[t], off + h))
    grid_spec = pltpu.PrefetchScalarGridSpec(
        num_scalar_prefetch=2,
        grid=(batch, MLA_HEADS // nh, len(pairs)),
        in_specs=[pl.BlockSpec((tq, hw), qmap), pl.BlockSpec((tq, hw), qmap),
                  pl.BlockSpec((tq, hw), kmap(0)),
                  pl.BlockSpec((tq, LANES), lambda b, h, t, qi, ki: (b * nq + ki[t], 0)),
                  pl.BlockSpec((tq, hw), kmap(MLA_HEADS // nh))],
        out_specs=pl.BlockSpec((tq, hw), qmap),
        scratch_shapes=[pltpu.VMEM((nh, tq, 2 * LANES), BF16), pltpu.VMEM((nh, tq, 2 * LANES), BF16),
                        pltpu.VMEM((nh, tq, 1), F32), pltpu.VMEM((nh, tq, 1), F32),
                        pltpu.VMEM((nh, tq, MLA_V), F32)],
    )
    return pl.pallas_call(
        functools.partial(_flash_kernel, scale=MLA_QK ** -0.5, nh=nh),
        grid_spec=grid_spec,
        out_shape=jax.ShapeDtypeStruct((batch * seq_len, MLA_HEADS * MLA_V), F32),
        compiler_params=_cparams(("parallel", "parallel", "arbitrary")),
        name="mla_flash_prompt",
    )(qi_of, ki_of, q_raw, qp, kv, kpe, kv)


def _decode_kernel(pt_ref, ql_ref, qp_ref, ckv_hbm, kpe_hbm, cnew_ref, knew_ref, o_ref,
                   qls_ref, qps_ref, m_ref, l_ref, acc_ref, kc_ref, pc_ref, kbuf, pbuf, sems,
                   *, npg, ngroups, nsteps, layer, new_len, scale):
    b = pl.program_id(0)
    g = pl.program_id(1)
    step = b * ngroups + g
    slot = jnp.bitwise_and(step, 1)

    def page_copies(bb, gg, sl):
        out = []
        for i in range(npg):
            page = pt_ref[bb, gg * npg + i]
            out.append(pltpu.make_async_copy(ckv_hbm.at[page, layer], kbuf.at[sl, i], sems.at[sl]))
            out.append(pltpu.make_async_copy(kpe_hbm.at[page, layer], pbuf.at[sl, i], sems.at[sl]))
        return out

    def start_all(copies):
        for n, cp in enumerate(copies):
            cp.start(priority=(n // 2) % 2)

    @pl.when(step == 0)
    def _():
        start_all(page_copies(b, g, slot))

    @pl.when(step + 1 < nsteps)
    def _():
        wrap = g == ngroups - 1
        start_all(page_copies(jnp.where(wrap, b + 1, b), jnp.where(wrap, 0, g + 1), 1 - slot))

    for cp in page_copies(b, g, slot):
        cp.wait()
    ckv_refs = [kbuf.at[slot, i] for i in range(npg)]
    kpe_refs = [pbuf.at[slot, i] for i in range(npg)]

    @pl.when(g == 0)
    def _():
        qls_ref[...] = (ql_ref[...] * scale).astype(BF16)
        qpe = _stack_lanes(qp_ref[...], MLA_HEADS).reshape(qps_ref.shape[0], LANES)
        qps_ref[...] = (qpe[:, :MLA_ROPE] * scale).astype(BF16)
        m_ref[...] = jnp.full_like(m_ref, -jnp.inf)
        l_ref[...] = jnp.zeros_like(l_ref)
        acc_ref[...] = jnp.zeros_like(acc_ref)

    for i in range(npg):
        sl = slice(i * PAGE_SIZE, (i + 1) * PAGE_SIZE)
        kc_ref[sl, :] = ckv_refs[i][...].astype(BF16)
        pc_ref[:, sl] = kpe_refs[i][...].astype(BF16)
    chunk_pages = math.gcd(npg, DEC_PAGES_PER_CHUNK)
    width = chunk_pages * PAGE_SIZE
    chunk = lambda c: slice(c * width, (c + 1) * width)
    scores = lambda c: _dot_nt(qls_ref[...], kc_ref[chunk(c), :]) + _dot(qps_ref[...], pc_ref[:, chunk(c)])
    values = lambda c: kc_ref[chunk(c), :]
    m, l, acc = _online_softmax_chunks(scores, values, npg // chunk_pages, m_ref[...], l_ref[...], acc_ref[...])
    m_ref[...] = m
    l_ref[...] = l
    acc_ref[...] = acc

    @pl.when(g == ngroups - 1)
    def _():
        kb = cnew_ref[...].astype(BF16)
        tq = jnp.bitwise_and(lax.broadcasted_iota(jnp.int32, (qls_ref.shape[0], kb.shape[0]), 0), new_len - 1)
        tk = lax.broadcasted_iota(jnp.int32, (qls_ref.shape[0], kb.shape[0]), 1)
        visible = jnp.logical_and(tk < new_len, tk <= tq)
        m2, l2, acc2 = _online_softmax_chunks(
            lambda c: _dot_nt(qls_ref[...], kb) + _dot(qps_ref[...], knew_ref[...]), lambda c: kb, 1,
            m, l, acc, mask=lambda c, s: jnp.where(visible, s, -jnp.inf))
        o_ref[...] = acc2 / l2


def _mla_decode(q_lat, qp, cache_ckv, cache_kpe, page_table, ckv_new, kpe_new, layer, *, new_len):
    bsz, n_pages = page_table.shape
    nrow = q_lat.shape[1]
    npg = min(DEC_PAGES_PER_STEP, n_pages)
    ngroups = n_pages // npg
    qp_rb0 = qp.shape[0] // new_len - bsz
    in_specs = [pl.BlockSpec((None, nrow, MLA_KV_RANK), lambda b, g, pt: (b, 0, 0)),
                pl.BlockSpec((new_len, MLA_HEADS * LANES), lambda b, g, pt: (qp_rb0 + b, 0)),
                pl.BlockSpec(memory_space=pl.ANY), pl.BlockSpec(memory_space=pl.ANY),
                pl.BlockSpec((None, PAGE_SIZE, MLA_KV_RANK), lambda b, g, pt: (b, 0, 0)),
                pl.BlockSpec((None, MLA_ROPE, PAGE_SIZE), lambda b, g, pt: (b, 0, 0))]
    grid_spec = pltpu.PrefetchScalarGridSpec(
        num_scalar_prefetch=1,
        grid=(bsz, ngroups),
        in_specs=in_specs,
        out_specs=pl.BlockSpec((None, nrow, MLA_KV_RANK), lambda b, g, pt: (b, 0, 0)),
        scratch_shapes=[pltpu.VMEM((nrow, MLA_KV_RANK), BF16), pltpu.VMEM((nrow, MLA_ROPE), BF16),
                        pltpu.VMEM((nrow, 1), F32), pltpu.VMEM((nrow, 1), F32),
                        pltpu.VMEM((nrow, MLA_KV_RANK), F32),
                        pltpu.VMEM((npg * PAGE_SIZE, MLA_KV_RANK), BF16),
                        pltpu.VMEM((MLA_ROPE, npg * PAGE_SIZE), BF16),
                        pltpu.VMEM((2, npg, PAGE_SIZE, MLA_KV_RANK), F32),
                        pltpu.VMEM((2, npg, MLA_ROPE, PAGE_SIZE), F32),
                        pltpu.SemaphoreType.DMA((2,))],
    )
    return pl.pallas_call(
        functools.partial(_decode_kernel, npg=npg, ngroups=ngroups, nsteps=bsz * ngroups, layer=layer,
                          new_len=new_len, scale=MLA_QK ** -0.5),
        grid_spec=grid_spec,
        out_shape=jax.ShapeDtypeStruct(q_lat.shape, F32),
        compiler_params=_cparams(("arbitrary", "arbitrary")),
        name="mla_decode",
    )(page_table, q_lat, qp, cache_ckv, cache_kpe, ckv_new, kpe_new)


def _lora_kernel(x_ref, p_ref, mu_ref, w1_ref, w2_ref, b_ref, o_ref, *, kind):
    x = x_ref[...]
    xm = x + (p_ref[...] - x) * mu_ref[...]
    t = jnp.dot(xm.astype(BF16), w1_ref[...], preferred_element_type=F32)
    if kind == "decay":
        t = jnp.tanh(t)
    elif kind == "gate":
        t = _sigmoid(t)
    y = jnp.dot(t.astype(BF16), w2_ref[...], preferred_element_type=F32)
    if kind == "decay":
        o_ref[...] = -_softplus(-(b_ref[...] + y)) - 0.5
    elif kind == "aaa":
        o_ref[...] = _sigmoid(b_ref[...] + y)
    else:
        o_ref[...] = y


def _lora(h, prev, mu, w1, w2, bias, *, kind, tm):
    t, d = h.shape
    r = w1.shape[1]
    row = pl.BlockSpec((tm, d), lambda i: (i, 0))
    vec = pl.BlockSpec((1, d), lambda i: (0, 0))
    return pl.pallas_call(
        functools.partial(_lora_kernel, kind=kind),
        grid=(t // tm,),
        in_specs=[row, row, vec, pl.BlockSpec((d, r), lambda i: (0, 0)),
                  pl.BlockSpec((r, d), lambda i: (0, 0)), vec],
        out_specs=row,
        out_shape=jax.ShapeDtypeStruct((t, d), F32),
        compiler_params=_cparams(("parallel",)),
        name=f"rwkv_lora_{kind}",
    )(h, prev, mu.reshape(1, d), w1, w2, bias.reshape(1, d))


def _rwkv_scan_kernel(*refs, rows, seq_rows, nseq, nchunks, has_h0, pb):
    r_ref, k_ref, v_ref, w_ref, a_ref, g_ref, kk_ref, ka_ref, rk_ref, lw_ref, lb_ref = refs[:11]
    if has_h0:
        h0_ref, y_ref, hout_ref, h_scr = refs[11:]
    else:
        y_ref, hout_ref, h_scr = refs[11:]
    c = pl.program_id(2)
    nb = nseq * pb

    @pl.when(c == 0)
    def _():
        if has_h0:
            h_scr[...] = h0_ref[...].reshape(nb, LANES, LANES)
        else:
            h_scr[...] = jnp.zeros_like(h_scr)

    ii, jj, incl, strict, last = _chunk_masks(rows, seq_rows)
    head0 = lax.broadcasted_iota(jnp.int32, (1, 1, LANES), 2) < RWKV_HEAD

    def head_sum(x):
        s0 = jnp.sum(jnp.where(head0, x, 0.0), axis=-1, keepdims=True)
        s1 = jnp.sum(jnp.where(head0, 0.0, x), axis=-1, keepdims=True)
        return jnp.where(head0, s0, s1)

    log_d2 = -jnp.exp(w_ref[...])
    cum2 = _dot_exact_lhs(jnp.where(incl, 1.0, 0.0).astype(BF16), log_d2)
    if nseq == 1:
        tot2 = cum2[rows - 1:rows]
    else:
        tot2 = _dot_exact_lhs(jnp.where(last, 1.0, 0.0).astype(BF16), cum2)
    stack = lambda x: _stack_lanes(x, pb)
    e_pos = stack(jnp.exp(cum2))
    e_neg = stack(jnp.exp(-cum2))
    e_prev = stack(jnp.exp(cum2 - log_d2))
    e_end = stack(jnp.exp(tot2 - cum2))
    gamma = stack(jnp.exp(tot2))

    r = stack(r_ref[...])
    k = stack(k_ref[...])
    v = stack(v_ref[...])
    a = stack(a_ref[...])
    kx = k * stack(kk_ref[...])
    kk = kx * lax.rsqrt(head_sum(kx * kx) + L2_EPS)
    k = k * (1.0 + (a - 1.0) * stack(ka_ref[...]))
    b = kk * a
    a_t = -kk * e_prev
    b_t = b * e_neg
    k_t = k * e_neg
    r_t = r * e_pos
    bd_t = jnp.swapaxes(b * e_end, 1, 2)
    kd_t = jnp.swapaxes(k * e_end, 1, 2)

    i2 = lax.broadcasted_iota(jnp.int32, (LANES, LANES), 0)
    j2 = lax.broadcasted_iota(jnp.int32, (LANES, LANES), 1)
    eye2 = i2 == j2
    block = (i2 < RWKV_HEAD) == (j2 < RWKV_HEAD)

    halves = lambda x: jnp.concatenate([jnp.where(head0, x, 0.0), jnp.where(head0, 0.0, x)], axis=0)
    twice = lambda x: jnp.concatenate([x, x], axis=0)
    join = lambda x2: jnp.where(head0, x2[:pb], x2[pb:])
    a_m = halves(a_t)
    r_m = halves(r_t)
    b2, k2, v2 = twice(b_t), twice(k_t), twice(v)
    a_ab = jnp.where(strict, _mm_nt(a_m, b2), 0.0)
    a_ak = jnp.where(strict, _mm_nt(a_m, k2), 0.0)
    p_rb = jnp.where(incl, _mm_nt(r_m, b2), 0.0)
    p_rk = jnp.where(incl, _mm_nt(r_m, k2), 0.0)
    t_inv = _unit_lower_inverse(-a_ab, ii, jj, seq_rows)
    akv = _mm(a_ak, v2)
    prkv = _mm(p_rk, v2)

    hs = h_scr[...]
    hbf = hs.astype(BF16)
    if nseq == 1:
        h2 = twice(hbf)
        u2 = _mm(t_inv, _mm(a_m, h2) + akv)
        y2 = _mm(r_m, h2) + _mm(p_rb, u2) + prkv
        u = join(u2)
        y = join(y2)
        g_col = jnp.sum(jnp.where(eye2, gamma, 0.0), axis=2, keepdims=True)
        h_scr[...] = g_col * hs + jnp.where(block, _mm(bd_t, u) + _mm(kd_t, v), 0.0)
    else:
        mine = _seq_mask(nseq, rows, seq_rows)
        h4 = hbf.reshape(nseq, pb, LANES, LANES)
        h2 = jnp.concatenate([h4, h4], axis=1).reshape(nseq * 2 * pb, LANES, LANES)
        pick = lambda x: jnp.sum(jnp.where(mine, x.reshape((nseq, 2 * pb) + x.shape[1:]), 0.0), axis=0)
        u2 = _mm(t_inv, pick(_mm(_over_seqs(a_m, nseq), h2)) + akv)
        y2 = pick(_mm(_over_seqs(r_m, nseq), h2)) + _mm(p_rb, u2) + prkv
        u = join(u2)
        y = join(y2)
        u_n = jnp.where(mine, u[None], 0.0).reshape(nb, rows, LANES)
        v_n = jnp.where(mine, v[None], 0.0).reshape(nb, rows, LANES)
        g_rows = jnp.stack([gamma[:, (n + 1) * seq_rows - 1:(n + 1) * seq_rows] for n in range(nseq)])
        g_col = jnp.sum(jnp.where(eye2, g_rows.reshape(nb, 1, LANES), 0.0), axis=2, keepdims=True)
        h_scr[...] = g_col * hs + jnp.where(
            block, _mm(_over_seqs(bd_t, nseq), u_n) + _mm(_over_seqs(kd_t, nseq), v_n), 0.0)

    mean = head_sum(y) * (1.0 / RWKV_HEAD)
    d = y - mean
    var = head_sum(d * d) * (1.0 / RWKV_HEAD)
    yn = d * lax.rsqrt(var + RWKV_GN_EPS) * stack(lw_ref[...]) + stack(lb_ref[...])
    bonus = head_sum(r * k * stack(rk_ref[...])) * v
    out = (yn + bonus) * stack(g_ref[...])
    for p in range(pb):
        y_ref[:, p * LANES:(p + 1) * LANES] = out[p]

    @pl.when(c == nchunks - 1)
    def _():
        hout_ref[...] = h_scr[...].reshape(hout_ref.shape)


def _rwkv_scan(r, k, v, w, a, g, k_k, k_a, r_k, ln_w, ln_b, h0, *, batch, seq_len, row0, sample):
    d = r.shape[1]
    npairs = d // LANES
    pb = min(RWKV_PB_SAMPLE if sample else RWKV_PB_PROMPT, npairs)
    if sample:
        nseq, seq_rows, nchunks = SEQ_GROUP, seq_len, 1
        rows = nseq * seq_rows
        grid = (batch // nseq, npairs // pb, 1)
    else:
        nseq, seq_rows, rows = 1, RWKV_CHUNK, RWKV_CHUNK
        nchunks = seq_len // rows
        grid = (batch, npairs // pb, nchunks)
    rb0 = row0 // rows
    tile = pl.BlockSpec((rows, pb * LANES), lambda b, p, c: (b * nchunks + c + rb0, p))
    vec = pl.BlockSpec((1, pb * LANES), lambda b, p, c: (0, p))
    state = pl.BlockSpec((nseq, pb, LANES, LANES), lambda b, p, c: (b, p, 0, 0))
    in_specs = [tile] * 6 + [vec] * 5
    args = [r, k, v, w, a, g] + [x.reshape(1, d) for x in (k_k, k_a, r_k, ln_w, ln_b)]
    if sample:
        in_specs.append(state)
        args.append(h0)
    return pl.pallas_call(
        functools.partial(_rwkv_scan_kernel, rows=rows, seq_rows=seq_rows, nseq=nseq, nchunks=nchunks,
                          has_h0=sample, pb=pb),
        grid=grid,
        in_specs=in_specs,
        out_specs=[pl.BlockSpec((rows, pb * LANES), lambda b, p, c: (b * nchunks + c, p)), state],
        out_shape=[jax.ShapeDtypeStruct((batch * seq_len, d), F32),
                   jax.ShapeDtypeStruct((batch, npairs, LANES, LANES), F32)],
        scratch_shapes=[pltpu.VMEM((nseq * pb, LANES, LANES), F32)],
        compiler_params=_cparams(("parallel", "parallel", "arbitrary")),
        name="rwkv_scan_sample" if sample else "rwkv_scan_prompt",
    )(*args)


def _pairs_from_heads(s):
    b, h, n, _ = s.shape
    st = jnp.swapaxes(s, -1, -2).reshape(b, h // 2, 2, n, n)
    top = jnp.pad(st[:, :, 0], ((0, 0), (0, 0), (0, 0), (0, n)))
    bottom = jnp.pad(st[:, :, 1], ((0, 0), (0, 0), (0, 0), (n, 0)))
    return jnp.concatenate([top, bottom], axis=2)


def _heads_from_pairs(hp):
    b, p, n2, _ = hp.shape
    n = n2 // 2
    hr = hp.reshape(b, p, 2, n, 2, n)
    st = jnp.stack([hr[:, :, 0, :, 0, :], hr[:, :, 1, :, 1, :]], axis=2).reshape(b, 2 * p, n, n)
    return jnp.swapaxes(st, -1, -2)


def _pad_cols(w, n):
    return jnp.pad(w, ((0, 0), (0, n - w.shape[1])))


def kernel(x_prompt, x_sample, cache_mla_ckv, cache_mla_kpe, page_table, state_gdn_s, state_gdn_conv, state_rwkv_wkv, state_rwkv_shift, norm_w, gdn_w_in, gdn_conv_w, gdn_a_log, gdn_dt_bias, gdn_o_norm, gdn_w_out, mla_w_in, mla_q_norm, mla_w_uq, mla_kv_norm, mla_w_uk, mla_w_uv, mla_w_o, rw_mu, rw_w_rkv, rw_w0, rw_w1, rw_w2, rw_a0, rw_a1, rw_a2, rw_g1, rw_g2, rw_k_k, rw_k_a, rw_r_k, rw_ln_w, rw_ln_b, rw_w_o, mlp_w_up, mlp_w_down):
    bp, lp, d = x_prompt.shape
    bs, ls, _ = x_sample.shape
    tp, ts = bp * lp, bs * ls
    depth = norm_w.shape[0]
    n_pages = page_table.shape[1]
    past_len = n_pages * PAGE_SIZE
    tm = _row_tile(tp, ts)
    assert ls == SUBLANES and bs % SEQ_GROUP == 0 and lp % GDN_CHUNK == 0 and lp % RWKV_CHUNK == 0

    x = jnp.concatenate([x_prompt.reshape(tp, d), x_sample.reshape(ts, d)], axis=0)
    mlp_up, mlp_down = mlp_w_up.astype(BF16), mlp_w_down.astype(BF16)
    gdn_w_nk = jnp.swapaxes(gdn_w_in, 1, 2)

    half = MLA_ROPE // 2
    inv_freq = 1.0 / (ROPE_THETA ** (jnp.arange(half, dtype=F32) / half))
    pos = jnp.concatenate([jnp.tile(jnp.arange(lp), bp), jnp.tile(past_len + jnp.arange(ls), bs)])
    ang = pos.astype(F32)[:, None] * inv_freq[None, :]
    cos, sin = jnp.cos(ang), jnp.sin(ang)
    zpad = jnp.zeros((tp + ts, LANES - MLA_ROPE), F32)
    cos2 = jnp.concatenate([cos, cos, zpad], axis=1)
    sin2 = jnp.concatenate([-sin, sin, zpad], axis=1)

    outs = {k: [] for k in ("ckv_p", "kpe_p", "ckv_s", "kpe_s", "gs_p", "gc_p", "gs_s", "gc_s",
                            "rs_p", "rx_p", "rs_s", "rx_s")}
    for i in range(depth):
        kind, j = i % 3, i // 3
        if kind == 0:
            nb = GDN_CONV_DIM + GDN_VAL_DIM
            proj = _linear(x, gdn_w_nk, wsel=j, ncols=nb, mode="norm", g=norm_w[i, 0], tm=tm)
            ba = _linear(x, gdn_w_nk, wsel=j, ncols=2 * GDN_V_HEADS, col0=nb, mode="norm", g=norm_w[i, 0], tm=tm)
            gates = _gdn_gates(ba, gdn_a_log[j], gdn_dt_bias[j], tm=tm)
            hist = jnp.pad(state_gdn_conv[:, j], ((0, 0), (SUBLANES - GDN_CONV + 1, 0), (0, 0)))
            hist = hist.reshape(ts, GDN_CONV_DIM)
            act_p = _gdn_conv(proj, None, gdn_conv_w[j], row0=0, rows=tp, seq_len=lp, sample=False, tm=tm)
            act_s = _gdn_conv(proj, hist, gdn_conv_w[j], row0=tp, rows=ts, seq_len=ls, sample=True, tm=tm)
            y_p, s_p = _gdn_scan(act_p, proj, gates, gdn_o_norm[j], None, j, batch=bp, seq_len=lp,
                                 row0=0, sample=False)
            earlier = tuple(outs["gs_s"]) if j == gdn_w_in.shape[0] - 1 else ()
            y_s, s_s = _gdn_scan(act_s, proj, gates, gdn_o_norm[j], state_gdn_s, j, batch=bs, seq_len=ls,
                                 row0=tp, sample=True, prev_states=earlier)
            if earlier:
                outs["gs_s"] = []
            w_out = gdn_w_out[j].astype(BF16)
            tail = GDN_CONV - 1
            conv_p = jnp.stack([lax.slice(proj, ((b + 1) * lp - tail, 0), ((b + 1) * lp, GDN_CONV_DIM))
                                for b in range(bp)])
            conv_s = lax.slice(proj, (tp, 0), (tp + ts, GDN_CONV_DIM)).reshape(bs, ls, GDN_CONV_DIM)[:, ls - tail:]
            outs["gs_p"].append(s_p)
            outs["gc_p"].append(conv_p)
            outs["gs_s"].append(s_s)
            outs["gc_s"].append(conv_s)
        elif kind == 1:
            w_in = mla_w_in[j]
            w_kpe = w_in[:, MLA_Q_RANK + MLA_KV_RANK:]
            w_c = jnp.concatenate([w_in[:, :MLA_Q_RANK + MLA_KV_RANK], _pad_cols(w_kpe, LANES),
                                   _pad_cols(jnp.roll(w_kpe, half, axis=1), LANES)], axis=1).astype(BF16)
            uq = mla_w_uq[j].reshape(MLA_Q_RANK, MLA_HEADS, MLA_QK)
            uq_pe = uq[:, :, MLA_NOPE:]
            padh = ((0, 0), (0, 0), (0, LANES - MLA_ROPE))
            w_q = jnp.concatenate([
                uq[:, :, :MLA_NOPE].reshape(MLA_Q_RANK, -1),
                jnp.pad(uq_pe, padh).reshape(MLA_Q_RANK, -1),
                jnp.pad(jnp.roll(uq_pe, half, axis=2), padh).reshape(MLA_Q_RANK, -1)], axis=1).astype(BF16)
            c = _linear(x, w_c, mode="norm", g=norm_w[i, 0], tm=tm)
            q_raw = _linear(c, w_q, mode="norm", g=mla_q_norm[j], tm=tm)
            ckv, kpe, qp = _mla_post(c, q_raw, cos2, sin2, mla_kv_norm[j], tm=min(tm, 256))
            w_kv = jnp.concatenate([mla_w_uk[j].reshape(MLA_KV_RANK, -1),
                                    mla_w_uv[j].reshape(MLA_KV_RANK, -1)], axis=1).astype(BF16)
            kv = _linear(ckv, w_kv, rows=tp, tm=tm)
            y_p = _mla_flash(q_raw, qp, kv, kpe, batch=bp, seq_len=lp)
            w_ukt = jnp.transpose(mla_w_uk[j], (1, 2, 0)).astype(BF16)
            w_uvh = jnp.transpose(mla_w_uv[j], (1, 0, 2)).astype(BF16)
            q_lat = _head_linear(q_raw, w_ukt, seq_len=ls, row0=tp, rows=ts, out_by_head=True, tm=tm)
            pad_keys = ((0, 0), (0, PAGE_SIZE - ls), (0, 0))
            ckv_new = jnp.pad(ckv[tp:].reshape(bs, ls, MLA_KV_RANK), pad_keys)
            kpe_new = jnp.pad(kpe[tp:, :MLA_ROPE].reshape(bs, ls, MLA_ROPE), pad_keys)
            o_lat = _mla_decode(q_lat.reshape(bs, MLA_HEADS * ls, MLA_KV_RANK), qp, cache_mla_ckv,
                                jnp.swapaxes(cache_mla_kpe, 2, 3), page_table, ckv_new,
                                jnp.swapaxes(kpe_new, 1, 2), j, new_len=ls)
            y_s = _head_linear(o_lat.reshape(bs, MLA_HEADS, ls, MLA_KV_RANK), w_uvh, seq_len=ls,
                               x_by_head=True, tm=tm)
            w_out = mla_w_o[j].astype(BF16)
            outs["ckv_p"].append(ckv[:tp].reshape(bp, lp, MLA_KV_RANK))
            outs["kpe_p"].append(kpe[:tp, :MLA_ROPE].reshape(bp, lp, MLA_ROPE))
            outs["ckv_s"].append(ckv[tp:].reshape(bs, ls, MLA_KV_RANK))
            outs["kpe_s"].append(kpe[tp:, :MLA_ROPE].reshape(bs, ls, MLA_ROPE))
        else:
            shift_rows = jnp.pad(state_rwkv_shift[:, j][:, None, :], ((0, 0), (0, ls - 1), (0, 0)))
            h, prev = _norm_shift(x, norm_w[i, 0], shift_rows.reshape(ts, d), tp=tp, seq_len=lp, tm=tm)
            mu = rw_mu[j]
            wr, wk, wv = (rw_w_rkv[j, s].astype(BF16) for s in range(3))
            r = _linear(h, wr, mode="mix", prev=prev, mu=mu[0], tm=tm)
            k = _linear(h, wk, mode="mix", prev=prev, mu=mu[1], tm=tm)
            v = _linear(h, wv, mode="mix", prev=prev, mu=mu[2], tm=tm)
            rank = lambda n: -(-n // LANES) * LANES
            lora_w = lambda w1, w2: (_pad_cols(w1, rank(w1.shape[1])).astype(BF16),
                                     jnp.pad(w2, ((0, rank(w2.shape[0]) - w2.shape[0]), (0, 0))).astype(BF16))
            w = _lora(h, prev, mu[3], *lora_w(rw_w1[j], rw_w2[j]), rw_w0[j], kind="decay", tm=tm)
            a = _lora(h, prev, mu[4], *lora_w(rw_a1[j], rw_a2[j]), rw_a0[j], kind="aaa", tm=tm)
            g = _lora(h, prev, mu[5], *lora_w(rw_g1[j], rw_g2[j]), jnp.zeros((d,), F32), kind="gate", tm=tm)
            params = (rw_k_k[j], rw_k_a[j], rw_r_k[j].reshape(d), rw_ln_w[j], rw_ln_b[j])
            y_p, hp_p = _rwkv_scan(r, k, v, w, a, g, *params, None, batch=bp, seq_len=lp, row0=0, sample=False)
            y_s, hp_s = _rwkv_scan(r, k, v, w, a, g, *params, _pairs_from_heads(state_rwkv_wkv[:, j]),
                                   batch=bs, seq_len=ls, row0=tp, sample=True)
            w_out = rw_w_o[j].astype(BF16)
            outs["rs_p"].append(_heads_from_pairs(hp_p))
            outs["rx_p"].append(jnp.concatenate([lax.slice(h, ((b + 1) * lp - 1, 0), ((b + 1) * lp, d))
                                                 for b in range(bp)]))
            outs["rs_s"].append(_heads_from_pairs(hp_s))
            outs["rx_s"].append(lax.slice(h, (tp, 0), (tp + ts, d)).reshape(bs, ls, d)[:, -1])
        x = _out_proj(y_p, y_s, w_out, x, norm_w[i, 1], tm=tm)
        x = _mlp(x, norm_w[i, 2], mlp_up, mlp_down, i, norm_w[i, 3], tm=tm)

    def stack(key):
        rows = outs[key]
        if key == "gs_s" and rows[0].ndim == state_gdn_s.ndim:
            return rows[0]
        return jnp.stack(rows, axis=1)

    return (x[:tp].reshape(bp, lp, d), x[tp:].reshape(bs, ls, d),
            stack("ckv_p"), stack("kpe_p"), stack("gs_p"), stack("gc_p"), stack("rs_p"), stack("rx_p"),
            stack("ckv_s"), stack("kpe_s"), stack("gs_s"), stack("gc_s"), stack("rs_s"), stack("rx_s"))
```
